```python
import jax
import jax.numpy as jnp
from jax import lax
import numpy as np

D_MODEL = 1024
BATCH = 8
SEQ = 4096
DEPTH = 4
DEC_BATCH = 16
DEC_SEQ = 16
PAST_LEN = 2048

CHUNK = 64
Q_BLOCK = 128
N_BRANCH = 4
BRANCH_WIDTH = D_MODEL // 4
HEAD_DIM = 64
N_HEADS = BRANCH_WIDTH // HEAD_DIM

RWKV_DECAY_LORA = 64
RWKV_A_LORA = 64
RWKV_SHIFT_COLS = 3 * BRANCH_WIDTH + RWKV_DECAY_LORA + RWKV_A_LORA
RWKV_GN_EPS = 64e-5

MLA_Q_LORA = 256
MLA_KV_LORA = 128
MLA_NOPE = 64
MLA_ROPE = 32
MLA_V = 64
MLA_QK = MLA_NOPE + MLA_ROPE

ROPE_BASE = 10000.0
NORM_EPS = 1e-6
HEAD_NORM_EPS = 1e-5

GATE_COLS = N_BRANCH * D_MODEL
A_COLS = RWKV_SHIFT_COLS + BRANCH_WIDTH
B_COLS = 4 * BRANCH_WIDTH
C_COLS = MLA_Q_LORA + MLA_KV_LORA + MLA_ROPE + BRANCH_WIDTH
D_COLS = 4 * BRANCH_WIDTH
IN_COLS = GATE_COLS + A_COLS + B_COLS + C_COLS + D_COLS
IN_SPLITS = (GATE_COLS, GATE_COLS + A_COLS, GATE_COLS + A_COLS + B_COLS, GATE_COLS + A_COLS + B_COLS + C_COLS)

kernel_name = 'hybrid_stream_rwkv7_hgrn2_mla_retention'


def rms_norm(x, g, eps=NORM_EPS):
    xf = x.astype(jnp.float32)
    y = xf * lax.rsqrt(jnp.mean(xf * xf, axis=-1, keepdims=True) + eps)
    return (y * g.astype(jnp.float32)).astype(x.dtype)


def head_rms_norm(x, g, eps=HEAD_NORM_EPS):
    y = x * lax.rsqrt(jnp.mean(x * x, axis=-1, keepdims=True) + eps)
    return y * g.astype(jnp.float32)


def head_layer_norm(x, w, b, eps):
    xc = x - jnp.mean(x, axis=-1, keepdims=True)
    y = xc * lax.rsqrt(jnp.mean(xc * xc, axis=-1, keepdims=True) + eps)
    return y * w.astype(jnp.float32) + b.astype(jnp.float32)


def rope(x, pos):
    d = x.shape[-1]
    inv_freq = jnp.power(ROPE_BASE, -jnp.arange(0, d, 2, dtype=jnp.float32) / d)
    ang = pos.astype(jnp.float32)[:, None] * inv_freq[None, :]
    ang = ang.reshape((ang.shape[0],) + (1,) * (x.ndim - 3) + (d // 2,))
    cos = jnp.cos(ang).astype(x.dtype)
    sin = jnp.sin(ang).astype(x.dtype)
    x1, x2 = x[..., : d // 2], x[..., d // 2:]
    return jnp.concatenate([x1 * cos - x2 * sin, x1 * sin + x2 * cos], axis=-1)


def split_heads(z):
    return z.reshape(z.shape[0], z.shape[1], N_HEADS, -1).astype(jnp.float32)


def chunked_gated_recurrence(q, k, v, log_f, S0, L):
    B, T, H, dk = q.shape
    n = T // L

    def to_chunks(a):
        return jnp.moveaxis(a.reshape(B, n, L, H, a.shape[-1]), 1, 0)

    causal = jnp.tril(jnp.ones((L, L), dtype=bool))

    def step(S, inp):
        qc, kc, vc, gc = inp
        b = jnp.cumsum(gc, axis=1)
        o_inter = jnp.einsum('blhk,bhkv->blhv', qc * jnp.exp(b), S)
        diff = b[:, :, None] - b[:, None, :]
        decay = jnp.exp(jnp.where(causal[None, :, :, None, None], diff, -jnp.inf))
        scores = jnp.einsum('bthk,btshk,bshk->bhts', qc, decay, kc)
        o_intra = jnp.einsum('bhts,bshv->bthv', scores, vc)
        b_last = b[:, -1]
        S_new = jnp.exp(b_last)[..., None] * S + jnp.einsum('bshk,bshv->bhkv', kc * jnp.exp(b_last[:, None] - b), vc)
        return S_new, o_inter + o_intra

    S_T, o = lax.scan(step, S0.astype(jnp.float32), (to_chunks(q), to_chunks(k), to_chunks(v), to_chunks(log_f)))
    return jnp.moveaxis(o, 0, 1).reshape(B, T, H, v.shape[-1]), S_T


def rwkv7_time_mix(p, shift_prev, S0, mu, w0, w_up, a0, a_up, k_k, k_a, r_k, ln_w, ln_b):
    B, T, _ = p.shape
    BW = BRANCH_WIDTH
    p_prev = jnp.concatenate([shift_prev[:, None, :].astype(p.dtype), p[:, :-1]], axis=1)
    xs = p + (p_prev - p) * mu
    r, k, v, wd, ad = jnp.split(xs, [BW, 2 * BW, 3 * BW, 3 * BW + RWKV_DECAY_LORA], axis=-1)
    w_raw = -jax.nn.softplus(-(w0 + jnp.tanh(wd) @ w_up)) - 0.5
    decay = jnp.exp(-jnp.exp(w_raw.astype(jnp.float32)))
    a = jax.nn.sigmoid(a0 + ad @ a_up)
    r, k, v, a, decay = [split_heads(z) for z in (r, k, v, a, decay)]
    kk = k * k_k.reshape(N_HEADS, HEAD_DIM).astype(jnp.float32)
    kk = kk / jnp.maximum(jnp.sqrt(jnp.sum(kk * kk, axis=-1, keepdims=True)), 1e-12)
    k = k * (1.0 + (a - 1.0) * k_a.reshape(N_HEADS, HEAD_DIM).astype(jnp.float32))

    def step(S, inp):
        r_t, w_t, k_t, v_t, kk_t, b_t = inp
        sa = jnp.einsum('bhvk,bhk->bhv', S, -kk_t)
        S = S * w_t[:, :, None, :] + sa[..., None] * b_t[:, :, None, :] + v_t[..., None] * k_t[:, :, None, :]
        return S, jnp.einsum('bhvk,bhk->bhv', S, r_t)

    tmaj = lambda z: jnp.swapaxes(z, 0, 1)
    S_T, y = lax.scan(step, S0.astype(jnp.float32),
                      (tmaj(r), tmaj(decay), tmaj(k), tmaj(v), tmaj(kk), tmaj(kk * a)))
    y = head_layer_norm(tmaj(y), ln_w, ln_b, RWKV_GN_EPS)
    y = y + jnp.sum(r * k * r_k.astype(jnp.float32), axis=-1, keepdims=True) * v
    return y.reshape(B, T, BW), p[:, -1], S_T


def hgrn2_mix(q, f_pre, i, lb, S0, L, norm_g):
    B, T, _ = q.shape
    z = split_heads(f_pre)
    lb_h = lb.reshape(N_HEADS, HEAD_DIM).astype(jnp.float32)
    log_f = jnp.logaddexp(jnp.log(lb_h), jnp.log1p(-lb_h) + jax.nn.log_sigmoid(z))
    k = (1.0 - lb_h) * jax.nn.sigmoid(-z)
    qh = jax.nn.silu(split_heads(q)) * HEAD_DIM ** -0.5
    o, S_T = chunked_gated_recurrence(qh, k, split_heads(i), log_f, S0, L)
    return head_rms_norm(o, norm_g).reshape(B, T, BRANCH_WIDTH), S_T


def retention_mix(q, k, v, pos, S0, L, norm_g):
    B, T, _ = q.shape
    qh = rope(split_heads(q), pos)
    kh = rope(split_heads(k), pos) * HEAD_DIM ** -0.5
    log_gamma = jnp.log1p(-jnp.exp2(-5.0 - jnp.arange(N_HEADS, dtype=jnp.float32)))
    log_f = jnp.broadcast_to(log_gamma[:, None], qh.shape)
    o, S_T = chunked_gated_recurrence(qh, kh, split_heads(v), log_f, S0, L)
    return head_rms_norm(o, norm_g).reshape(B, T, BRANCH_WIDTH), S_T


def mla_block_attention(q_nope, q_rope, k_nope, k_rope, v, q_start):
    Tq, Tk = q_nope.shape[1], k_nope.shape[1]
    s = jnp.einsum('bqhd,bkhd->bhqk', q_nope, k_nope) + jnp.einsum('bqhd,bkd->bhqk', q_rope, k_rope)
    s = s.astype(jnp.float32) * MLA_QK ** -0.5
    q_chunk = (q_start + jnp.arange(Tq)) // CHUNK
    k_chunk = jnp.arange(Tk) // CHUNK
    s = jnp.where(k_chunk[None, :] <= q_chunk[:, None], s, -jnp.inf)
    p = jax.nn.softmax(s, axis=-1).astype(v.dtype)
    return jnp.einsum('bhqk,bkhd->bqhd', p, v)


def mla_mix(cq, ckv, kr, pos, past_latent, past_krope, q_norm_g, w_q_up, kv_norm_g, w_kv_up):
    B, T, _ = cq.shape
    q = (rms_norm(cq, q_norm_g) @ w_q_up).reshape(B, T, N_HEADS, MLA_QK)
    q_nope, q_rope = q[..., :MLA_NOPE], rope(q[..., MLA_NOPE:], pos)
    latent = rms_norm(ckv, kv_norm_g)
    k_rope = rope(kr, pos)
    lat_all = jnp.concatenate([past_latent.astype(latent.dtype), latent], axis=1)
    kr_all = jnp.concatenate([past_krope.astype(k_rope.dtype), k_rope], axis=1)
    Tk = lat_all.shape[1]
    kv = (lat_all @ w_kv_up).reshape(B, Tk, N_HEADS, MLA_NOPE + MLA_V)
    k_nope, v = kv[..., :MLA_NOPE], kv[..., MLA_NOPE:]
    q0 = Tk - T
    outs = []
    for s0 in range(0, T, Q_BLOCK):
        e = min(s0 + Q_BLOCK, T)
        kend = min(Tk, -(-(q0 + e) // CHUNK) * CHUNK)
        outs.append(mla_block_attention(q_nope[:, s0:e], q_rope[:, s0:e], k_nope[:, :kend],
                                        kr_all[:, :kend], v[:, :kend], q0 + s0))
    o = jnp.concatenate(outs, axis=1).reshape(B, T, N_HEADS * MLA_V)
    return o, latent, k_rope


def trunk_layer(h, shift_prev, rwkv_S0, hgrn_S0, ret_S0, past_latent, past_krope, lb, lw):
    (norm_g, w_in, rwkv_mu, rwkv_w0, rwkv_w_up, rwkv_a0, rwkv_a_up, rwkv_k_k, rwkv_k_a, rwkv_r_k,
     rwkv_ln_w, rwkv_ln_b, hgrn_norm_g, mla_q_norm_g, mla_w_q_up, mla_kv_norm_g, mla_w_kv_up,
     ret_norm_g, w_branch, w_out) = lw
    B, T, _ = h.shape
    dt = h.dtype
    pos = past_latent.shape[1] + jnp.arange(T, dtype=jnp.int32)
    L = min(T, CHUNK)
    u = rms_norm(h, norm_g)
    gate_pre, pa, pb, pc, pd = jnp.split(u @ w_in, IN_SPLITS, axis=-1)
    oa, shift_new, rwkv_S = rwkv7_time_mix(pa[..., :RWKV_SHIFT_COLS], shift_prev, rwkv_S0, rwkv_mu, rwkv_w0,
                                           rwkv_w_up, rwkv_a0, rwkv_a_up, rwkv_k_k, rwkv_k_a, rwkv_r_k,
                                           rwkv_ln_w, rwkv_ln_b)
    oa = oa.astype(dt) * jax.nn.silu(pa[..., RWKV_SHIFT_COLS:])
    bq, bf, bi, bg = jnp.split(pb, 4, axis=-1)
    ob, hgrn_S = hgrn2_mix(bq, bf, bi, lb, hgrn_S0, L, hgrn_norm_g)
    ob = ob.astype(dt) * jax.nn.silu(bg)
    cq, ckv, ckr, cg = jnp.split(pc, [MLA_Q_LORA, MLA_Q_LORA + MLA_KV_LORA, MLA_Q_LORA + MLA_KV_LORA + MLA_ROPE], axis=-1)
    oc, latent, k_rope = mla_mix(cq, ckv, ckr, pos, past_latent, past_krope, mla_q_norm_g, mla_w_q_up,
                                 mla_kv_norm_g, mla_w_kv_up)
    oc = oc.astype(dt) * jax.nn.silu(cg)
    dq, dk, dv, dg = jnp.split(pd, 4, axis=-1)
    od, ret_S = retention_mix(dq, dk, dv, pos, ret_S0, L, ret_norm_g)
    od = od.astype(dt) * jax.nn.silu(dg)
    branch_out = (oa, ob, oc, od)
    merged = jax.nn.sigmoid(gate_pre[..., :D_MODEL]) * (oa @ w_branch[0])
    for n in range(1, N_BRANCH):
        merged = merged + jax.nn.sigmoid(gate_pre[..., n * D_MODEL:(n + 1) * D_MODEL]) * (branch_out[n] @ w_branch[n])
    h = h + merged @ w_out
    return h, (shift_new, rwkv_S.astype(dt), hgrn_S.astype(dt), ret_S.astype(dt), latent, k_rope)


def run_trunk(x, shift0, rwkv0, hgrn0, ret0, lat0, kr0, lower_bounds, layer_weights):
    h = x
    per_layer = []
    for l in range(DEPTH):
        h, st = trunk_layer(h, shift0[l], rwkv0[l], hgrn0[l], ret0[l], lat0[l], kr0[l], lower_bounds[l],
                            [w[l] for w in layer_weights])
        per_layer.append(st)
    stacked = [jnp.stack([st[j] for st in per_layer]) for j in range(6)]
    return h, stacked


def setup_inputs(seed: int = 0) -> dict:
    key = jax.random.key(seed)
    ks = list(jax.random.split(key, 40))

    def nrm(shape, scale):
        return scale * jax.random.normal(ks.pop(), shape, jnp.float32)

    hs = (DEPTH, DEC_BATCH, N_HEADS, HEAD_DIM, HEAD_DIM)
    hd = (DEPTH, N_HEADS, HEAD_DIM)
    return {
        'x_prompt': nrm((BATCH, SEQ, D_MODEL), 1.0),
        'x_sample': nrm((DEC_BATCH, DEC_SEQ, D_MODEL), 1.0),
        'state_rwkv_shift': nrm((DEPTH, DEC_BATCH, RWKV_SHIFT_COLS), 1.0),
        'state_rwkv': nrm(hs, 0.1),
        'state_hgrn': nrm(hs, 0.1),
        'cache_mla_latent': nrm((DEPTH, DEC_BATCH, PAST_LEN, MLA_KV_LORA), 1.0),
        'cache_mla_krope': nrm((DEPTH, DEC_BATCH, PAST_LEN, MLA_ROPE), 1.0),
        'state_ret': nrm(hs, 0.1),
        'norm_g': 1.0 + nrm((DEPTH, D_MODEL), 0.01),
        'w_in': nrm((DEPTH, D_MODEL, IN_COLS), D_MODEL ** -0.5),
        'rwkv_mu': jax.random.uniform(ks.pop(), (DEPTH, RWKV_SHIFT_COLS), jnp.float32),
        'rwkv_w0': nrm((DEPTH, BRANCH_WIDTH), 0.5),
        'rwkv_w_up': nrm((DEPTH, RWKV_DECAY_LORA, BRANCH_WIDTH), 0.1),
        'rwkv_a0': nrm((DEPTH, BRANCH_WIDTH), 0.1),
        'rwkv_a_up': nrm((DEPTH, RWKV_A_LORA, BRANCH_WIDTH), RWKV_A_LORA ** -0.5),
        'rwkv_k_k': 0.85 + nrm((DEPTH, BRANCH_WIDTH), 0.02),
        'rwkv_k_a': 1.0 + nrm((DEPTH, BRANCH_WIDTH), 0.02),
        'rwkv_r_k': nrm(hd, 0.1),
        'rwkv_ln_w': 1.0 + nrm(hd, 0.01),
        'rwkv_ln_b': nrm(hd, 0.01),
        'hgrn_lb_logits': nrm((DEPTH, BRANCH_WIDTH), 0.5),
        'hgrn_norm_g': 1.0 + nrm(hd, 0.01),
        'mla_q_norm_g': 1.0 + nrm((DEPTH, MLA_Q_LORA), 0.01),
        'mla_w_q_up': nrm((DEPTH, MLA_Q_LORA, N_HEADS * MLA_QK), MLA_Q_LORA ** -0.5),
        'mla_kv_norm_g': 1.0 + nrm((DEPTH, MLA_KV_LORA), 0.01),
        'mla_w_kv_up': nrm((DEPTH, MLA_KV_LORA, N_HEADS * (MLA_NOPE + MLA_V)), MLA_KV_LORA ** -0.5),
        'ret_norm_g': 1.0 + nrm(hd, 0.01),
        'w_branch': nrm((DEPTH, N_BRANCH, BRANCH_WIDTH, D_MODEL), BRANCH_WIDTH ** -0.5),
        'w_out': nrm((DEPTH, D_MODEL, D_MODEL), D_MODEL ** -0.5),
        'final_norm_g': 1.0 + nrm((D_MODEL,), 0.01),
    }


def reference(x_prompt, x_sample, state_rwkv_shift, state_rwkv, state_hgrn, cache_mla_latent, cache_mla_krope,
              state_ret, norm_g, w_in, rwkv_mu, rwkv_w0, rwkv_w_up, rwkv_a0, rwkv_a_up, rwkv_k_k, rwkv_k_a,
              rwkv_r_k, rwkv_ln_w, rwkv_ln_b, hgrn_lb_logits, hgrn_norm_g, mla_q_norm_g, mla_w_q_up,
              mla_kv_norm_g, mla_w_kv_up, ret_norm_g, w_branch, w_out, final_norm_g):
    lb_cum = jnp.cumsum(jax.nn.softmax(hgrn_lb_logits.astype(jnp.float32), axis=0), axis=0)
    lower_bounds = lb_cum - lb_cum[0:1]
    layer_weights = (norm_g, w_in, rwkv_mu, rwkv_w0, rwkv_w_up, rwkv_a0, rwkv_a_up, rwkv_k_k, rwkv_k_a,
                     rwkv_r_k, rwkv_ln_w, rwkv_ln_b, hgrn_norm_g, mla_q_norm_g, mla_w_q_up, mla_kv_norm_g,
                     mla_w_kv_up, ret_norm_g, w_branch, w_out)

    Bp, dt = x_prompt.shape[0], x_prompt.dtype
    zs = (DEPTH, Bp, N_HEADS, HEAD_DIM, HEAD_DIM)
    h_p, (p_shift, p_rwkv, p_hgrn, p_ret, p_lat, p_kr) = run_trunk(
        x_prompt, jnp.zeros((DEPTH, Bp, RWKV_SHIFT_COLS), dt), jnp.zeros(zs, dt), jnp.zeros(zs, dt),
        jnp.zeros(zs, dt), jnp.zeros((DEPTH, Bp, 0, MLA_KV_LORA), dt), jnp.zeros((DEPTH, Bp, 0, MLA_ROPE), dt),
        lower_bounds, layer_weights)

    h_s, (s_shift, s_rwkv, s_hgrn, s_ret, s_lat, s_kr) = run_trunk(
        x_sample, state_rwkv_shift, state_rwkv, state_hgrn, state_ret, cache_mla_latent, cache_mla_krope,
        lower_bounds, layer_weights)

    y_prompt = rms_norm(h_p, final_norm_g)
    y_sample = rms_norm(h_s, final_norm_g)
    return (y_prompt, y_sample, p_shift, s_shift, p_rwkv, s_rwkv, p_hgrn, s_hgrn, p_lat, s_lat, p_kr, s_kr, p_ret, s_ret)
```

```python
import functools

import jax
import jax.numpy as jnp
import numpy as np
from jax import lax
from jax.experimental import pallas as pl
from jax.experimental.pallas import tpu as pltpu

F32 = jnp.float32
BF16 = jnp.bfloat16

D_MODEL = 1024
DEPTH = 4
N_HEADS = 4
HEAD_DIM = 64
BW = N_HEADS * HEAD_DIM
N_BRANCH = 4
CHUNK = 64
SHIFT_COLS = 3 * BW + 64 + 64
GN_EPS = 64e-5
NORM_EPS = 1e-6
HEAD_NORM_EPS = 1e-5
ROPE_BASE = 10000.0
Q_LORA, KV_LORA, NOPE, ROPE_D, V_D = 256, 128, 64, 32, 64
QK_D = NOPE + ROPE_D
HEAD_PAD = 128
QP = N_HEADS * HEAD_PAD
LANES = 128
SUBLANES = 8

A0, A1 = 0, SHIFT_COLS + BW
B0, B1 = A1, A1 + 4 * BW
C0, C1 = B1, B1 + Q_LORA + KV_LORA + LANES + BW
D0, D1 = C1, C1 + 4 * BW
PROJ_COLS = D1

T_CD, T_SD, T_CQ, T_SQ, T_CK, T_SK, TAB_COLS = 0, 256, 512, 1024, 1536, 1664, 1792

(R_NORM, R_MU, R_W0, R_A0, R_KK, R_KA, R_RK, R_LB, R_QG, R_KVG,
 R_LNW, R_LNB, R_HG, R_RG, R_FG, N_ROWS) = (0, 1, 2, 3, 4, 5, 6, 7, 11, 12, 13, 14, 15, 16, 17, 24)

VMEM_LIMIT = 50 * 1024 * 1024


def _cparams(sem):
    return pltpu.CompilerParams(dimension_semantics=sem, vmem_limit_bytes=VMEM_LIMIT)


def _sigmoid(x):
    return 1.0 / (1.0 + jnp.exp(-x))


def _silu(x):
    return x * _sigmoid(x)


def _softplus(x):
    return jnp.maximum(x, 0.0) + jnp.log(1.0 + jnp.exp(-jnp.abs(x)))


def _dot(a, b):
    return jnp.dot(a, b, preferred_element_type=F32)


def _split(x):
    hi = x.astype(BF16)
    lo = (x - hi.astype(F32)).astype(BF16)
    return hi, lo


def _dot_lhs2(x, m_bf16):
    hi, lo = _split(x)
    return _dot(hi, m_bf16) + _dot(lo, m_bf16)


def _dot3(x, m_hi, m_lo):
    hi, lo = _split(x)
    return _dot(hi, m_hi) + _dot(lo, m_hi) + _dot(hi, m_lo)


def _rms(x, g, eps):
    return x * lax.rsqrt(jnp.mean(x * x, axis=-1, keepdims=True) + eps) * g


def _rope(x, cos, sin_signed, half, group, start=0):
    w = x.shape[-1]
    lane = lax.broadcasted_iota(jnp.int32, (1, w), 1) % group
    up = pltpu.roll(x, w - half, axis=1)
    dn = pltpu.roll(x, half, axis=1)
    return x * cos + jnp.where(lane < start + half, up, dn) * sin_signed


def _proj_kernel(layer, tt, h_ref, sh0_ref, tab_ref, par_ref, w_ref, wl_hi_ref, wl_lo_ref, seg_ref, wq_ref,
                 ar_ref, aw_ref, ak_ref, av_ref, akk_ref, ab_ref, posta_ref,
                 bq_ref, bf_ref, bk_ref, bv_ref, gb_ref,
                 qp_ref, lat_ref, kr_ref, gc_ref,
                 dq_ref, dk_ref, dv_ref, gd_ref, shn_ref, carry_ref):
    t = pl.program_id(0)
    b = pl.program_id(1)
    par = lambda r, n: par_ref[r:r + 1, 0:n]
    seg = seg_ref[...]

    u = _rms(h_ref[0], par(R_NORM, D_MODEL), NORM_EPS).astype(BF16)

    pa_all = _dot(u, w_ref[:, A0:A1])
    pa = pa_all[:, :SHIFT_COLS]
    prev_row = jnp.where(t == 0, sh0_ref[b], carry_ref[b])
    row = lax.broadcasted_iota(jnp.int32, (tt, 1), 0)
    p_prev = jnp.where(row == 0, prev_row, pltpu.roll(pa, 1, axis=0))
    last = pa[tt - 1:tt, :]
    carry_ref[b] = last
    shn_ref[0, 0] = last
    xs = pa + (p_prev - pa) * par(R_MU, SHIFT_COLS)
    r, k, v, wdad = xs[:, :BW], xs[:, BW:2 * BW], xs[:, 2 * BW:3 * BW], xs[:, 3 * BW:]
    lane = lax.broadcasted_iota(jnp.int32, (1, LANES), 1)
    lora = _dot3(jnp.where(lane < 64, jnp.tanh(wdad), wdad), wl_hi_ref[...], wl_lo_ref[...])
    w_raw = -_softplus(-(par(R_W0, BW) + lora[:, :BW])) - 0.5
    a = _sigmoid(par(R_A0, BW) + lora[:, BW:])
    kk = k * par(R_KK, BW)
    kk = kk / jnp.maximum(jnp.sqrt(_dot_lhs2(kk * kk, seg)), 1e-12)
    k2 = k * (1.0 + (a - 1.0) * par(R_KA, BW))
    ar_ref[...] = r
    aw_ref[...] = jnp.exp(-jnp.exp(w_raw))
    ak_ref[...] = k2
    av_ref[...] = v
    akk_ref[...] = kk
    ab_ref[...] = kk * a
    posta_ref[:, :BW] = _dot_lhs2(r * k2 * par(R_RK, BW), seg) * v
    posta_ref[:, BW:] = _silu(pa_all[:, SHIFT_COLS:])

    pb = _dot(u, w_ref[:, B0:B1])
    lg = par_ref[R_LB:R_LB + DEPTH, 0:BW]
    e = jnp.exp(lg - jnp.max(lg, axis=0, keepdims=True))
    lb = jnp.zeros((1, BW), F32)
    for j in range(1, layer + 1):
        lb = lb + e[j:j + 1]
    lb = lb / jnp.sum(e, axis=0, keepdims=True)
    z = pb[:, BW:2 * BW]
    bq_ref[...] = _silu(pb[:, :BW]) * (HEAD_DIM ** -0.5)
    bf_ref[...] = lb + (1.0 - lb) * _sigmoid(z)
    bk_ref[...] = (1.0 - lb) * _sigmoid(-z)
    bv_ref[...] = pb[:, 2 * BW:3 * BW]
    gb_ref[...] = _silu(pb[:, 3 * BW:])

    pc = _dot(u, w_ref[:, C0:C1])
    qn = _rms(pc[:, :Q_LORA], par(R_QG, Q_LORA), NORM_EPS).astype(BF16)
    q = _dot(qn, wq_ref[...])
    qp_ref[0] = _rope(q, tab_ref[:, T_CQ:T_CQ + QP], tab_ref[:, T_SQ:T_SQ + QP], ROPE_D // 2, HEAD_PAD,
                      NOPE).astype(BF16)
    lat_ref[0] = _rms(pc[:, Q_LORA:Q_LORA + KV_LORA], par(R_KVG, KV_LORA), NORM_EPS)
    kr_ref[0] = _rope(pc[:, Q_LORA + KV_LORA:Q_LORA + KV_LORA + LANES], tab_ref[:, T_CK:T_CK + LANES],
                      tab_ref[:, T_SK:T_SK + LANES], ROPE_D // 2, LANES)
    gc_ref[0] = _silu(pc[:, Q_LORA + KV_LORA + LANES:])

    pd = _dot(u, w_ref[:, D0:D1])
    cos, sin = tab_ref[:, T_CD:T_CD + BW], tab_ref[:, T_SD:T_SD + BW]
    dq_ref[...] = _rope(pd[:, :BW], cos, sin, HEAD_DIM // 2, HEAD_DIM)
    dk_ref[...] = _rope(pd[:, BW:2 * BW], cos, sin, HEAD_DIM // 2, HEAD_DIM) * (HEAD_DIM ** -0.5)
    dv_ref[...] = pd[:, 2 * BW:3 * BW]
    gd_ref[...] = _silu(pd[:, 3 * BW:])


def _proj(layer, h, shift0, tab, par, w_rest, wl_hi, wl_lo, seg, wq):
    B, T, _ = h.shape
    tt = min(256, T)
    nt = T // tt
    bspec = lambda c: pl.BlockSpec((1, tt, c), lambda t, b: (b, t, 0))
    tspec = lambda c: pl.BlockSpec((tt, c), lambda t, b: (t, b))
    full = lambda a: pl.BlockSpec(a.shape, lambda t, b: (0,) * a.ndim)
    bm = lambda c, dt=F32: jax.ShapeDtypeStruct((B, T, c), dt)
    tm = lambda c: jax.ShapeDtypeStruct((T, B * c), F32)
    outs = ([(tm(BW), tspec(BW))] * 6 + [(tm(2 * BW), tspec(2 * BW))]
            + [(tm(BW), tspec(BW))] * 5
            + [(bm(QP, BF16), bspec(QP)), (bm(KV_LORA), bspec(KV_LORA)), (bm(LANES), bspec(LANES)),
               (bm(BW), bspec(BW))]
            + [(tm(BW), tspec(BW))] * 4
            + [(jax.ShapeDtypeStruct((nt, B, 1, SHIFT_COLS), F32),
                pl.BlockSpec((1, 1, 1, SHIFT_COLS), lambda t, b: (t, b, 0, 0)))])
    return pl.pallas_call(
        functools.partial(_proj_kernel, layer, tt),
        grid=(nt, B),
        in_specs=[bspec(D_MODEL), full(shift0), pl.BlockSpec((tt, TAB_COLS), lambda t, b: (t, 0)), full(par),
                  full(w_rest), full(wl_hi), full(wl_lo), full(seg), full(wq)],
        out_specs=tuple(o[1] for o in outs),
        out_shape=tuple(o[0] for o in outs),
        scratch_shapes=[pltpu.VMEM((B, 1, SHIFT_COLS), F32)],
        compiler_params=_cparams(("arbitrary", "arbitrary")),
        name="proj",
    )(h, shift0, tab, par, w_rest, wl_hi, wl_lo, seg, wq)


N_ACC = 4


def _seq_kernel(kind, tc, bh, *refs):
    ncol = {"rwkv": 5, "gla": 3, "ret": 2}[kind]
    col_refs = refs[:ncol]
    if kind == "ret":
        f_ref, v_ref, s0_ref, y_ref, st_out_ref, st, slot_a, slot_b, vslot_a, vslot_b, yslot_a, yslot_b = refs[ncol:]
    else:
        v_ref, s0_ref, y_ref, st_out_ref, st, slot_a, slot_b, vslot_a, vslot_b, yslot_a, yslot_b = refs[ncol:]
    g_sz = bh // 2
    n_grp = LANES // g_sz
    rep = n_grp // 2
    vr = HEAD_DIM // rep
    gid = lax.broadcasted_iota(jnp.int32, (1, LANES), 1) // g_sz
    odd = (gid % 2) == 1
    grid_step = pl.program_id(0)

    @pl.when(grid_step == 0)
    def _():
        st[...] = s0_ref[...]

    def transposed(ref, t):
        x = ref[t]
        return jnp.concatenate([x] * n_grp, axis=0).T

    def prepare(t, slot, vslot):
        for o, ref in enumerate(col_refs):
            xt = transposed(ref, t)
            slot[o] = jnp.where(odd, xt[HEAD_DIM:], xt[:HEAD_DIM])
        xt = transposed(v_ref, t)
        vv = None
        for g in range(n_grp):
            blk = xt[(g % 2) * HEAD_DIM + (g // 2) * vr:(g % 2) * HEAD_DIM + (g // 2) * vr + vr]
            vv = blk if vv is None else jnp.where(gid == g, blk, vv)
        vslot[...] = vv

    def emit(t, yv):
        blocks = []
        for parity in range(2):
            for hi in range(rep):
                blocks.append(jnp.where(gid == hi * 2 + parity, yv, 0.0))
        zt = jnp.concatenate(blocks, axis=0).T
        y = zt[0:g_sz]
        for g in range(1, n_grp):
            y = y + zt[g * g_sz:(g + 1) * g_sz]
        y_ref[t] = y

    def bc(ref, o, k):
        return jnp.broadcast_to(ref[o, pl.ds(k, 1), :], (vr, LANES))

    def tree(xs):
        while len(xs) > 1:
            xs = [xs[i] + xs[i + 1] for i in range(0, len(xs), 2)]
        return xs[0]

    def step(t, slot, vslot, nslot, nvslot, yslot, yprev):
        emit(jnp.maximum(t - 1, 0), yprev[...])
        prepare(jnp.minimum(t + 1, tc - 1), nslot, nvslot)
        vv = vslot[...]
        acc = [None] * N_ACC
        if kind == "rwkv":
            for k in range(HEAD_DIM):
                p = st[k] * bc(slot, 3, k)
                acc[k % N_ACC] = p if acc[k % N_ACC] is None else acc[k % N_ACC] + p
            sa = -tree(acc)
            acc = [None] * N_ACC
            for k in range(HEAD_DIM):
                new = st[k] * bc(slot, 1, k) + sa * bc(slot, 4, k) + vv * bc(slot, 2, k)
                st[k] = new
                p = new * bc(slot, 0, k)
                acc[k % N_ACC] = p if acc[k % N_ACC] is None else acc[k % N_ACC] + p
        else:
            for k in range(HEAD_DIM):
                f = bc(f_ref, 0, k) if kind == "ret" else bc(slot, 1, k)
                new = st[k] * f + vv * bc(slot, ncol - 1, k)
                st[k] = new
                p = new * bc(slot, 0, k)
                acc[k % N_ACC] = p if acc[k % N_ACC] is None else acc[k % N_ACC] + p
        yslot[...] = tree(acc)

    prepare(0, slot_a, vslot_a)
    yslot_b[...] = jnp.zeros((vr, LANES), F32)

    def pair(i, carry):
        step(2 * i, slot_a, vslot_a, slot_b, vslot_b, yslot_a, yslot_b)
        step(2 * i + 1, slot_b, vslot_b, slot_a, vslot_a, yslot_b, yslot_a)
        return carry

    lax.fori_loop(0, tc // 2, pair, 0)
    emit(tc - 1, yslot_b[...])

    @pl.when(grid_step == pl.num_programs(0) - 1)
    def _():
        st_out_ref[...] = st[...]


def _seq(kind, cols, v_rows, s0, bh, f_const=None):
    T = v_rows.shape[0]
    g_sz = bh // 2
    vr = s0.shape[1]
    tc = min(128, T)
    view = lambda a: a.reshape(T, g_sz, LANES)
    ospec = pl.BlockSpec((tc, g_sz, LANES), lambda g: (g, 0, 0))
    sspec = pl.BlockSpec((HEAD_DIM, vr, LANES), lambda g: (0, 0, 0))
    args = [view(c) for c in cols]
    in_specs = [ospec] * len(cols)
    if kind == "ret":
        args.append(f_const)
        in_specs.append(pl.BlockSpec((1, HEAD_DIM, LANES), lambda g: (0, 0, 0)))
    args += [view(v_rows), s0]
    in_specs += [ospec, sspec]
    y, s_out = pl.pallas_call(
        functools.partial(_seq_kernel, kind, tc, bh),
        grid=(T // tc,),
        in_specs=in_specs,
        out_specs=(ospec, sspec),
        out_shape=(jax.ShapeDtypeStruct((T, g_sz, LANES), F32), jax.ShapeDtypeStruct((HEAD_DIM, vr, LANES), F32)),
        scratch_shapes=[pltpu.VMEM((HEAD_DIM, vr, LANES), F32),
                        pltpu.VMEM((len(cols), HEAD_DIM, LANES), F32), pltpu.VMEM((len(cols), HEAD_DIM, LANES), F32),
                        pltpu.VMEM((vr, LANES), F32), pltpu.VMEM((vr, LANES), F32),
                        pltpu.VMEM((vr, LANES), F32), pltpu.VMEM((vr, LANES), F32)],
        compiler_params=_cparams(("arbitrary",)),
        name="seq_" + kind,
    )(*args)
    return y.reshape(T, bh * HEAD_DIM), s_out


def _kvprep_kernel(lat_ref, kr_ref, wk_ref, wv_ref, place_ref, k_ref, v_ref):
    lat = lat_ref[0].astype(BF16)
    k_ref[0] = (_dot(lat, wk_ref[...]) + _dot(kr_ref[0].astype(BF16), place_ref[...])).astype(BF16)
    v_ref[0] = _dot(lat, wv_ref[...]).astype(BF16)


def _kvprep(lat_all, kr_all, wk, wv, place, tr):
    B, tk, _ = lat_all.shape
    rspec = lambda c: pl.BlockSpec((1, tr, c), lambda b, i: (b, i, 0))
    full = lambda a: pl.BlockSpec(a.shape, lambda b, i: (0,) * a.ndim)
    return pl.pallas_call(
        _kvprep_kernel,
        grid=(B, tk // tr),
        in_specs=[rspec(KV_LORA), rspec(LANES), full(wk), full(wv), full(place)],
        out_specs=(rspec(QP), rspec(QP)),
        out_shape=(jax.ShapeDtypeStruct((B, tk, QP), BF16), jax.ShapeDtypeStruct((B, tk, QP), BF16)),
        compiler_params=_cparams(("arbitrary", "arbitrary")),
        name="kvprep",
    )(lat_all, kr_all, wk, wv, place)


def _attn_kernel(bq, bk, q0, kv_len, last_block, diag_only, q_ref, k_ref, v_ref, o_ref, m_ref, l_ref, acc_ref):
    i = pl.program_id(1)
    j = pl.program_id(2)

    @pl.when(j == 0)
    def _():
        m_ref[...] = jnp.full(m_ref.shape, -jnp.inf, F32)
        l_ref[...] = jnp.zeros(l_ref.shape, F32)
        acc_ref[...] = jnp.zeros(acc_ref.shape, F32)

    def block(masked):
        if masked:
            qi = q0 + i * bq + lax.broadcasted_iota(jnp.int32, (bq, bk), 0)
            ki = j * bk + lax.broadcasted_iota(jnp.int32, (bq, bk), 1)
            keep = jnp.logical_and(ki // CHUNK <= qi // CHUNK, ki < kv_len)
        for h in range(N_HEADS):
            hs = slice(h * HEAD_PAD, (h + 1) * HEAD_PAD)
            s = lax.dot_general(q_ref[0, :, hs], k_ref[0, :, hs], (((1,), (1,)), ((), ())),
                                preferred_element_type=F32) * (QK_D ** -0.5)
            if masked:
                s = jnp.where(keep, s, -jnp.inf)
            m_old = m_ref[h]
            m_new = jnp.maximum(m_old, jnp.max(s, axis=-1, keepdims=True))
            alpha = jnp.exp(m_old - m_new)
            p = jnp.exp(s - m_new[:, 0:1])
            l_ref[h] = alpha * l_ref[h] + jnp.sum(p, axis=-1, keepdims=True)
            acc_ref[:, hs] = alpha * acc_ref[:, hs] + _dot(p.astype(BF16), v_ref[0, :, hs])
            m_ref[h] = m_new

    last_needed = (q0 + (i + 1) * bq - 1) // CHUNK * CHUNK + CHUNK - 1
    if diag_only:
        @pl.when(j < i)
        def _():
            block(False)

        @pl.when(j == i)
        def _():
            block(True)
    else:
        @pl.when(j * bk <= last_needed)
        def _():
            block(True)

    @pl.when(j == last_block)
    def _():
        for h in range(N_HEADS):
            o = acc_ref[:, h * HEAD_PAD:(h + 1) * HEAD_PAD] / l_ref[h]
            o_ref[0, :, h * V_D:(h + 1) * V_D] = o[:, :V_D]


def _attn(qp, kp, vp, q0, kv_len, bk):
    B, T, _ = qp.shape
    tk = kp.shape[1]
    bq = min(512, T)
    nq, nk = T // bq, tk // bk
    diag_only = q0 == 0 and bq == bk and T == tk and bq % CHUNK == 0
    if diag_only:
        kv_map = lambda b, i, j: (b, jnp.minimum(j, i), 0)
    else:
        kv_map = lambda b, i, j: (b, j, 0)
    return pl.pallas_call(
        functools.partial(_attn_kernel, bq, bk, q0, kv_len, nk - 1, diag_only),
        grid=(B, nq, nk),
        in_specs=[pl.BlockSpec((1, bq, QP), lambda b, i, j: (b, i, 0)),
                  pl.BlockSpec((1, bk, QP), kv_map), pl.BlockSpec((1, bk, QP), kv_map)],
        out_specs=pl.BlockSpec((1, bq, BW), lambda b, i, j: (b, i, 0)),
        out_shape=jax.ShapeDtypeStruct((B, T, BW), F32),
        scratch_shapes=[pltpu.VMEM((N_HEADS, bq, LANES), F32), pltpu.VMEM((N_HEADS, bq, LANES), F32),
                        pltpu.VMEM((bq, QP), F32)],
        compiler_params=_cparams(("arbitrary", "arbitrary", "arbitrary")),
        name="attn",
    )(qp, kp, vp)


def _merge_kernel(final, h_ref, ya_ref, posta_ref, ob_ref, gb_ref, oc_ref, gc_ref, od_ref, gd_ref,
                  par_ref, seg_ref, wg_ref, wb_ref, wo_ref, out_ref):
    par = lambda r, n: par_ref[r:r + 1, 0:n]
    seg = seg_ref[...]
    h = h_ref[0]
    u = _rms(h, par(R_NORM, D_MODEL), NORM_EPS).astype(BF16)
    inv = 1.0 / HEAD_DIM

    ya = ya_ref[...]
    xc = ya - _dot_lhs2(ya, seg) * inv
    yn = xc * lax.rsqrt(_dot_lhs2(xc * xc, seg) * inv + GN_EPS) * par(R_LNW, BW) + par(R_LNB, BW)
    oa = (yn + posta_ref[:, :BW]) * posta_ref[:, BW:]

    def head_rms(o, g):
        return o * lax.rsqrt(_dot_lhs2(o * o, seg) * inv + HEAD_NORM_EPS) * g

    ob = head_rms(ob_ref[...], par(R_HG, BW)) * gb_ref[...]
    oc = oc_ref[0] * gc_ref[0]
    od = head_rms(od_ref[...], par(R_RG, BW)) * gd_ref[...]

    merged = None
    for n, o in enumerate((oa, ob, oc, od)):
        gate = _sigmoid(_dot(u, wg_ref[:, n * D_MODEL:(n + 1) * D_MODEL]))
        term = gate * _dot(o.astype(BF16), wb_ref[n * BW:(n + 1) * BW, :])
        merged = term if merged is None else merged + term
    out = h + _dot(merged.astype(BF16), wo_ref[...])
    if final:
        out = _rms(out, par(R_FG, D_MODEL), NORM_EPS)
    out_ref[0] = out


def _merge(final, h, ya, posta, ob, gb, oc, gc, od, gd, par, seg, wg, wb, wo):
    B, T, _ = h.shape
    tt = min(512, T)
    bspec = lambda c: pl.BlockSpec((1, tt, c), lambda t, b: (b, t, 0))
    tspec = lambda c: pl.BlockSpec((tt, c), lambda t, b: (t, b))
    full = lambda a: pl.BlockSpec(a.shape, lambda t, b: (0,) * a.ndim)
    consts = (par, seg, wg, wb, wo)
    return pl.pallas_call(
        functools.partial(_merge_kernel, final),
        grid=(T // tt, B),
        in_specs=[bspec(D_MODEL), tspec(BW), tspec(2 * BW), tspec(BW), tspec(BW), bspec(BW), bspec(BW), tspec(BW),
                  tspec(BW)] + [full(a) for a in consts],
        out_specs=bspec(D_MODEL),
        out_shape=jax.ShapeDtypeStruct((B, T, D_MODEL), F32),
        compiler_params=_cparams(("arbitrary", "arbitrary")),
        name="merge",
    )(h, ya, posta, ob, gb, oc, gc, od, gd, *consts)


def _rope_tables(past, T):
    pos = (past + jnp.arange(T, dtype=jnp.int32)).astype(F32)[:, None]

    def tables(width, group, start, d):
        lane = np.arange(width) % group - start
        on = (lane >= 0) & (lane < d)
        idx = np.where(on, lane % (d // 2), 0)
        inv_freq = jnp.power(ROPE_BASE, -jnp.arange(0, d, 2, dtype=F32) / d)
        ang = pos * inv_freq[None, :]
        cos = jnp.where(on[None, :], jnp.cos(ang)[:, idx], 1.0)
        sign = np.where(lane < d // 2, -1.0, 1.0).astype(np.float32)
        sin = jnp.where(on[None, :], jnp.sin(ang)[:, idx] * sign[None, :], 0.0)
        return cos, sin

    cd, sd = tables(BW, HEAD_DIM, 0, HEAD_DIM)
    cq, sq = tables(QP, HEAD_PAD, NOPE, ROPE_D)
    ck, sk = tables(LANES, LANES, 0, ROPE_D)
    return jnp.concatenate([cd, sd, cq, sq, ck, sk], axis=1)


def _layer_weights(l, p):
    w_in = p["w_in"][l]
    gate_cols = N_BRANCH * D_MODEL
    c_kr = gate_cols + SHIFT_COLS + BW + 4 * BW + Q_LORA + KV_LORA
    c_g = c_kr + ROPE_D
    w_rest = jnp.concatenate([
        w_in[:, gate_cols:c_kr],
        jnp.pad(w_in[:, c_kr:c_g], ((0, 0), (0, LANES - ROPE_D))),
        w_in[:, c_g:],
    ], axis=1).astype(BF16)
    wg = w_in[:, :gate_cols].astype(BF16)

    z = jnp.zeros((64, BW), F32)
    wl = jnp.concatenate([jnp.concatenate([p["rwkv_w_up"][l], z], axis=1),
                          jnp.concatenate([z, p["rwkv_a_up"][l]], axis=1)], axis=0)
    wl_hi = wl.astype(BF16)
    wl_lo = (wl - wl_hi.astype(F32)).astype(BF16)

    wq = p["mla_w_q_up"][l].reshape(Q_LORA, N_HEADS, QK_D)
    wq = jnp.pad(wq, ((0, 0), (0, 0), (0, HEAD_PAD - QK_D))).reshape(Q_LORA, QP).astype(BF16)
    wkv = p["mla_w_kv_up"][l].reshape(KV_LORA, N_HEADS, NOPE + V_D)
    wk = jnp.pad(wkv[:, :, :NOPE], ((0, 0), (0, 0), (0, HEAD_PAD - NOPE))).reshape(KV_LORA, QP).astype(BF16)
    wv = jnp.pad(wkv[:, :, NOPE:], ((0, 0), (0, 0), (0, HEAD_PAD - V_D))).reshape(KV_LORA, QP).astype(BF16)

    wb = p["w_branch"][l].reshape(N_BRANCH * BW, D_MODEL).astype(BF16)
    wo = p["w_out"][l].astype(BF16)

    def row(v):
        v = v.reshape(1, -1).astype(F32)
        return jnp.pad(v, ((0, 0), (0, D_MODEL - v.shape[1])))

    rows = [row(p["norm_g"][l]), row(p["rwkv_mu"][l]), row(p["rwkv_w0"][l]), row(p["rwkv_a0"][l]),
            row(p["rwkv_k_k"][l]), row(p["rwkv_k_a"][l]), row(p["rwkv_r_k"][l])]
    rows += [row(p["hgrn_lb_logits"][j]) for j in range(DEPTH)]
    rows += [row(p["mla_q_norm_g"][l]), row(p["mla_kv_norm_g"][l]), row(p["rwkv_ln_w"][l]), row(p["rwkv_ln_b"][l]),
             row(p["hgrn_norm_g"][l]), row(p["ret_norm_g"][l]), row(p["final_norm_g"])]
    par = jnp.concatenate(rows + [jnp.zeros((N_ROWS - len(rows), D_MODEL), F32)], axis=0)
    return dict(w_rest=w_rest, wg=wg, wl_hi=wl_hi, wl_lo=wl_lo, wq=wq, wk=wk, wv=wv, wb=wb, wo=wo, par=par)


def _constants():
    head = np.arange(BW) // HEAD_DIM
    seg = jnp.asarray((head[:, None] == head[None, :]).astype(np.float32), dtype=BF16)
    place = np.zeros((LANES, QP), np.float32)
    for h in range(N_HEADS):
        for j in range(ROPE_D):
            place[j, h * HEAD_PAD + NOPE + j] = 1.0
    return seg, jnp.asarray(place, dtype=BF16)


def _state_in(s, B, rep, key_last):
    vr = HEAD_DIM // rep
    if key_last:
        y = s.reshape(B, 2, 2, rep, vr, HEAD_DIM).transpose(5, 4, 3, 2, 0, 1)
    else:
        y = s.reshape(B, 2, 2, HEAD_DIM, rep, vr).transpose(3, 5, 4, 2, 0, 1)
    return y.reshape(HEAD_DIM, vr, LANES).astype(F32)


def _state_out(y, B, rep, key_last):
    vr = HEAD_DIM // rep
    y = y.reshape(HEAD_DIM, vr, rep, 2, B, 2)
    if key_last:
        y = y.transpose(4, 5, 3, 2, 1, 0)
    else:
        y = y.transpose(4, 5, 3, 0, 2, 1)
    return y.reshape(B, N_HEADS, HEAD_DIM, HEAD_DIM)


def _run_trunk(x, shift0, rwkv0, hgrn0, ret0, lat0, kr0, p, weights, consts):
    B, T, _ = x.shape
    past = lat0.shape[2]
    bh = B * N_HEADS
    rep = LANES // bh
    assert rep * bh == LANES and HEAD_DIM % rep == 0 and (HEAD_DIM // rep) % SUBLANES == 0 and N_HEADS == 4
    seg, place = consts
    tab = _rope_tables(past, T)
    gamma = 1.0 - jnp.exp2(-5.0 - jnp.arange(N_HEADS, dtype=F32))
    lane = np.arange(LANES)
    head_of_lane = ((lane % (bh // 2)) % 2) * 2 + (lane // (bh // 2)) % 2
    f_ret = jnp.broadcast_to(gamma[head_of_lane][None, None, :], (1, HEAD_DIM, LANES))
    h = x
    per_layer = []
    for l in range(DEPTH):
        w = weights[l]
        (ar, aw, ak, av, akk, ab, posta, bq, bf, bk, bv, gb, qp, lat, kr, gc, dq, dk, dv, gd, shn) = _proj(
            l, h, shift0[l][:, None, :], tab, w["par"], w["w_rest"], w["wl_hi"], w["wl_lo"], seg, w["wq"])

        ya, s_rwkv = _seq("rwkv", [ar, aw, ak, akk, ab], av, _state_in(rwkv0[l], B, rep, True), bh)
        ob, s_hgrn = _seq("gla", [bq, bf, bk], bv, _state_in(hgrn0[l], B, rep, False), bh)
        od, s_ret = _seq("ret", [dq, dk], dv, _state_in(ret0[l], B, rep, False), bh, f_const=f_ret)

        lat_all = jnp.concatenate([lat0[l], lat], axis=1)
        kr_all = jnp.concatenate([jnp.pad(kr0[l], ((0, 0), (0, 0), (0, LANES - ROPE_D))), kr], axis=1)
        tk = past + T
        blk = min(512, T) if past == 0 else 256
        tk_pad = -(-tk // blk) * blk
        if tk_pad != tk:
            lat_all = jnp.pad(lat_all, ((0, 0), (0, tk_pad - tk), (0, 0)))
            kr_all = jnp.pad(kr_all, ((0, 0), (0, tk_pad - tk), (0, 0)))
        kp, vp = _kvprep(lat_all, kr_all, w["wk"], w["wv"], place, blk)
        oc = _attn(qp, kp, vp, past, tk, blk)

        h = _merge(l == DEPTH - 1, h, ya, posta, ob, gb, oc, gc, od, gd, w["par"], seg, w["wg"], w["wb"], w["wo"])
        per_layer.append((shn[-1, :, 0, :], _state_out(s_rwkv, B, rep, True), _state_out(s_hgrn, B, rep, False),
                          _state_out(s_ret, B, rep, False), lat, kr[:, :, :ROPE_D]))
    stacked = [jnp.stack([st[j] for st in per_layer]) for j in range(6)]
    return h, stacked


def kernel(x_prompt, x_sample, state_rwkv_shift, state_rwkv, state_hgrn, cache_mla_latent, cache_mla_krope, state_ret, norm_g, w_in, rwkv_mu, rwkv_w0, rwkv_w_up, rwkv_a0, rwkv_a_up, rwkv_k_k, rwkv_k_a, rwkv_r_k, rwkv_ln_w, rwkv_ln_b, hgrn_lb_logits, hgrn_norm_g, mla_q_norm_g, mla_w_q_up, mla_kv_norm_g, mla_w_kv_up, ret_norm_g, w_branch, w_out, final_norm_g):
    p = dict(norm_g=norm_g, w_in=w_in, rwkv_mu=rwkv_mu, rwkv_w0=rwkv_w0, rwkv_w_up=rwkv_w_up, rwkv_a0=rwkv_a0,
             rwkv_a_up=rwkv_a_up, rwkv_k_k=rwkv_k_k, rwkv_k_a=rwkv_k_a, rwkv_r_k=rwkv_r_k, rwkv_ln_w=rwkv_ln_w,
             rwkv_ln_b=rwkv_ln_b, hgrn_lb_logits=hgrn_lb_logits, hgrn_norm_g=hgrn_norm_g, mla_q_norm_g=mla_q_norm_g,
             mla_w_q_up=mla_w_q_up, mla_kv_norm_g=mla_kv_norm_g, mla_w_kv_up=mla_w_kv_up, ret_norm_g=ret_norm_g,
             w_branch=w_branch, w_out=w_out, final_norm_g=final_norm_g)
    weights = [_layer_weights(l, p) for l in range(DEPTH)]
    consts = _constants()
    bp, dt = x_prompt.shape[0], x_prompt.dtype
    zs = jnp.zeros((DEPTH, bp, N_HEADS, HEAD_DIM, HEAD_DIM), dt)
    y_p, (p_shift, p_rwkv, p_hgrn, p_ret, p_lat, p_kr) = _run_trunk(
        x_prompt, jnp.zeros((DEPTH, bp, SHIFT_COLS), dt), zs, zs, zs, jnp.zeros((DEPTH, bp, 0, KV_LORA), dt),
        jnp.zeros((DEPTH, bp, 0, ROPE_D), dt), p, weights, consts)
    y_s, (s_shift, s_rwkv, s_hgrn, s_ret, s_lat, s_kr) = _run_trunk(
        x_sample, state_rwkv_shift, state_rwkv, state_hgrn, state_ret, cache_mla_latent, cache_mla_krope,
        p, weights, consts)
    return (y_p, y_s, p_shift, s_shift, p_rwkv, s_rwkv, p_hgrn, s_hgrn, p_lat, s_lat, p_kr, s_kr, p_ret, s_ret)
```

```python
import functools

import jax
import jax.numpy as jnp
import numpy as np
from jax import lax
from jax.experimental import pallas as pl
from jax.experimental.pallas import tpu as pltpu

F32 = jnp.float32
BF16 = jnp.bfloat16

D_MODEL = 1024
DEPTH = 4
N_HEADS = 4
HEAD_DIM = 64
BW = N_HEADS * HEAD_DIM
N_BRANCH = 4
CHUNK = 64
SHIFT_COLS = 3 * BW + 64 + 64
GN_EPS = 64e-5
NORM_EPS = 1e-6
HEAD_NORM_EPS = 1e-5
ROPE_BASE = 10000.0
Q_LORA, KV_LORA, NOPE, ROPE_D, V_D = 256, 128, 64, 32, 64
QK_D = NOPE + ROPE_D
HEAD_PAD = 128
Q_SCALE = QK_D ** -0.5 * 1.4426950408889634
QP = N_HEADS * HEAD_PAD
LANES = 128
SUBLANES = 8

A0, A1 = 0, SHIFT_COLS + BW
B0, B1 = A1, A1 + 4 * BW
C0, C1 = B1, B1 + Q_LORA + KV_LORA + LANES + BW
D0, D1 = C1, C1 + 4 * BW
PROJ_COLS = D1

T_CD, T_SD, T_CQ, T_SQ, T_CK, T_SK, TAB_COLS = 0, 256, 512, 1024, 1536, 1664, 1792

(R_NORM, R_MU, R_W0, R_A0, R_KK, R_KA, R_RK, R_LB, R_QG, R_KVG,
 R_LNW, R_LNB, R_HG, R_RG, R_FG, N_ROWS) = (0, 1, 2, 3, 4, 5, 6, 7, 11, 12, 13, 14, 15, 16, 17, 24)

VMEM_LIMIT = 50 * 1024 * 1024


def _cparams(sem):
    return pltpu.CompilerParams(dimension_semantics=sem, vmem_limit_bytes=VMEM_LIMIT)


def _sigmoid(x):
    return 1.0 / (1.0 + jnp.exp(-x))


def _silu(x):
    return x * _sigmoid(x)


def _softplus(x):
    return jnp.maximum(x, 0.0) + jnp.log(1.0 + jnp.exp(-jnp.abs(x)))


def _dot(a, b):
    return jnp.dot(a, b, preferred_element_type=F32)


def _split(x):
    hi = x.astype(BF16)
    lo = (x - hi.astype(F32)).astype(BF16)
    return hi, lo


def _dot_lhs2(x, m_bf16):
    hi, lo = _split(x)
    return _dot(hi, m_bf16) + _dot(lo, m_bf16)


def _dot3(x, m_hi, m_lo):
    hi, lo = _split(x)
    return _dot(hi, m_hi) + _dot(lo, m_hi) + _dot(hi, m_lo)


def _rms(x, g, eps):
    return x * lax.rsqrt(jnp.mean(x * x, axis=-1, keepdims=True) + eps) * g


def _rope(x, cos, sin_signed, half, group, start=0):
    w = x.shape[-1]
    lane = lax.broadcasted_iota(jnp.int32, (1, w), 1) % group
    up = pltpu.roll(x, w - half, axis=1)
    dn = pltpu.roll(x, half, axis=1)
    return x * cos + jnp.where(lane < start + half, up, dn) * sin_signed


def _proj_kernel(layer, tt, h_ref, sh0_ref, tab_ref, par_ref, w_ref, wl_hi_ref, wl_lo_ref, seg_ref, wq_ref,
                 ar_ref, aw_ref, ak_ref, av_ref, akk_ref, ab_ref, posta_ref,
                 bq_ref, bf_ref, bv_ref, gb_ref,
                 qp_ref, lat_ref, kr_ref, gc_ref,
                 dq_ref, dk_ref, dv_ref, gd_ref, shn_ref, carry_ref):
    t = pl.program_id(0)
    b = pl.program_id(1)
    par = lambda r, n: par_ref[r:r + 1, 0:n]
    seg = seg_ref[...]

    u = _rms(h_ref[0], par(R_NORM, D_MODEL), NORM_EPS).astype(BF16)

    pa_all = _dot(u, w_ref[:, A0:A1])
    pa = pa_all[:, :SHIFT_COLS]
    prev_row = jnp.where(t == 0, sh0_ref[b], carry_ref[b])
    row = lax.broadcasted_iota(jnp.int32, (tt, 1), 0)
    p_prev = jnp.where(row == 0, prev_row, pltpu.roll(pa, 1, axis=0))
    last = pa[tt - 1:tt, :]
    carry_ref[b] = last
    shn_ref[0, 0] = last
    xs = pa + (p_prev - pa) * par(R_MU, SHIFT_COLS)
    r, k, v, wdad = xs[:, :BW], xs[:, BW:2 * BW], xs[:, 2 * BW:3 * BW], xs[:, 3 * BW:]
    lane = lax.broadcasted_iota(jnp.int32, (1, LANES), 1)
    lora = _dot3(jnp.where(lane < 64, jnp.tanh(wdad), wdad), wl_hi_ref[...], wl_lo_ref[...])
    w_raw = -_softplus(-(par(R_W0, BW) + lora[:, :BW])) - 0.5
    a = _sigmoid(par(R_A0, BW) + lora[:, BW:])
    kk = k * par(R_KK, BW)
    kk = kk / jnp.maximum(jnp.sqrt(_dot_lhs2(kk * kk, seg)), 1e-12)
    k2 = k * (1.0 + (a - 1.0) * par(R_KA, BW))
    ar_ref[...] = r
    aw_ref[...] = jnp.exp(-jnp.exp(w_raw))
    ak_ref[...] = k2
    av_ref[...] = v
    akk_ref[...] = kk
    ab_ref[...] = kk * a
    posta_ref[:, :BW] = _dot_lhs2(r * k2 * par(R_RK, BW), seg) * v
    posta_ref[:, BW:] = _silu(pa_all[:, SHIFT_COLS:])

    pb = _dot(u, w_ref[:, B0:B1])
    lg = par_ref[R_LB:R_LB + DEPTH, 0:BW]
    e = jnp.exp(lg - jnp.max(lg, axis=0, keepdims=True))
    lb = jnp.zeros((1, BW), F32)
    for j in range(1, layer + 1):
        lb = lb + e[j:j + 1]
    lb = lb / jnp.sum(e, axis=0, keepdims=True)
    z = pb[:, BW:2 * BW]
    bq_ref[...] = _silu(pb[:, :BW]) * (HEAD_DIM ** -0.5)
    bf_ref[...] = lb + (1.0 - lb) * _sigmoid(z)
    bv_ref[...] = pb[:, 2 * BW:3 * BW]
    gb_ref[...] = _silu(pb[:, 3 * BW:])

    pc = _dot(u, w_ref[:, C0:C1])
    qn = _rms(pc[:, :Q_LORA], par(R_QG, Q_LORA), NORM_EPS).astype(BF16)
    q = _dot(qn, wq_ref[...]) * Q_SCALE
    qp_ref[0] = _rope(q, tab_ref[:, T_CQ:T_CQ + QP], tab_ref[:, T_SQ:T_SQ + QP], ROPE_D // 2, HEAD_PAD,
                      NOPE).astype(BF16)
    lat_ref[0] = _rms(pc[:, Q_LORA:Q_LORA + KV_LORA], par(R_KVG, KV_LORA), NORM_EPS)
    kr_ref[0] = _rope(pc[:, Q_LORA + KV_LORA:Q_LORA + KV_LORA + LANES], tab_ref[:, T_CK:T_CK + LANES],
                      tab_ref[:, T_SK:T_SK + LANES], ROPE_D // 2, LANES)
    gc_ref[0] = _silu(pc[:, Q_LORA + KV_LORA + LANES:])

    pd = _dot(u, w_ref[:, D0:D1])
    cos, sin = tab_ref[:, T_CD:T_CD + BW], tab_ref[:, T_SD:T_SD + BW]
    dq_ref[0] = _rope(pd[:, :BW], cos, sin, HEAD_DIM // 2, HEAD_DIM)
    dk_ref[0] = _rope(pd[:, BW:2 * BW], cos, sin, HEAD_DIM // 2, HEAD_DIM) * (HEAD_DIM ** -0.5)
    dv_ref[0] = pd[:, 2 * BW:3 * BW]
    gd_ref[0] = _silu(pd[:, 3 * BW:])


def _proj(layer, h, shift0, tab, par, w_rest, wl_hi, wl_lo, seg, wq):
    B, T, _ = h.shape
    tt = min(256, T)
    nt = T // tt
    bspec = lambda c: pl.BlockSpec((1, tt, c), lambda t, b: (b, t, 0))
    tspec = lambda c: pl.BlockSpec((tt, c), lambda t, b: (t, b))
    full = lambda a: pl.BlockSpec(a.shape, lambda t, b: (0,) * a.ndim)
    bm = lambda c, dt=F32: jax.ShapeDtypeStruct((B, T, c), dt)
    tm = lambda c: jax.ShapeDtypeStruct((T, B * c), F32)
    outs = ([(tm(BW), tspec(BW))] * 6 + [(tm(2 * BW), tspec(2 * BW))]
            + [(tm(BW), tspec(BW))] * 4
            + [(bm(QP, BF16), bspec(QP)), (bm(KV_LORA), bspec(KV_LORA)), (bm(LANES), bspec(LANES)),
               (bm(BW), bspec(BW))]
            + [(bm(BW), bspec(BW))] * 4
            + [(jax.ShapeDtypeStruct((nt, B, 1, SHIFT_COLS), F32),
                pl.BlockSpec((1, 1, 1, SHIFT_COLS), lambda t, b: (t, b, 0, 0)))])
    return pl.pallas_call(
        functools.partial(_proj_kernel, layer, tt),
        grid=(nt, B),
        in_specs=[bspec(D_MODEL), full(shift0), pl.BlockSpec((tt, TAB_COLS), lambda t, b: (t, 0)), full(par),
                  full(w_rest), full(wl_hi), full(wl_lo), full(seg), full(wq)],
        out_specs=tuple(o[1] for o in outs),
        out_shape=tuple(o[0] for o in outs),
        scratch_shapes=[pltpu.VMEM((B, 1, SHIFT_COLS), F32)],
        compiler_params=_cparams(("arbitrary", "arbitrary")),
        name="proj",
    )(h, shift0, tab, par, w_rest, wl_hi, wl_lo, seg, wq)


N_ACC = 4


N_COL = 8


def _seq_kernel(tc, bh, r_ref, w_ref, k_ref, kk_ref, b_ref, va_ref, q_ref, f_ref, vb_ref, sa0_ref, sb0_ref,
                ya_ref, yb_ref, sa_out_ref, sb_out_ref,
                st_a, st_b, slot_a, slot_b, vslot_a, vslot_b, yslot_a, yslot_b):
    g_sz = bh // 2
    rep = LANES // bh
    vr = HEAD_DIM // rep
    n_pairs = tc // 2
    lane = lax.broadcasted_iota(jnp.int32, (1, LANES), 1)
    low = lane < HEAD_DIM
    hi_id = lane // bh
    grid_step = pl.program_id(0)

    @pl.when(grid_step == 0)
    def _():
        st_a[...] = sa0_ref[...]
        st_b[...] = sb0_ref[...]

    def paired(ref, p):
        a, b = ref[2 * p], ref[2 * p + 1]
        s0 = jnp.where(low, a, pltpu.roll(b, HEAD_DIM, axis=1))
        s1 = jnp.where(low, pltpu.roll(a, HEAD_DIM, axis=1), b)
        return jnp.concatenate([s0, s1] * rep, axis=0).T

    def prepare(p, slot, vslot):
        for o, ref in enumerate((r_ref, w_ref, k_ref, kk_ref, b_ref, q_ref, f_ref)):
            xt = paired(ref, p)
            slot[o] = xt
            if ref is f_ref:
                slot[o + 1] = 1.0 - xt
        for ri, ref in enumerate((va_ref, vb_ref)):
            xt = paired(ref, p)
            for s in range(2):
                vv = xt[s * HEAD_DIM:s * HEAD_DIM + vr]
                for g in range(1, rep):
                    vv = jnp.where(hi_id == g, xt[s * HEAD_DIM + g * vr:s * HEAD_DIM + (g + 1) * vr], vv)
                vslot[ri, s] = vv

    def emit(p, yslot):
        for ri, y_ref in enumerate((ya_ref, yb_ref)):
            blocks = [jnp.where(hi_id == g, yslot[ri, s], 0.0) for s in range(2) for g in range(rep)]
            zt = jnp.concatenate(blocks, axis=0).T
            yp = zt[0:bh]
            for g in range(1, rep):
                yp = yp + zt[g * bh:(g + 1) * bh]
            p0, p1 = yp[:g_sz], yp[g_sz:]
            y_ref[2 * p] = jnp.where(low, p0, pltpu.roll(p1, HEAD_DIM, axis=1))
            y_ref[2 * p + 1] = jnp.where(low, pltpu.roll(p0, HEAD_DIM, axis=1), p1)

    def tree(xs):
        while len(xs) > 1:
            xs = [xs[i] + xs[i + 1] for i in range(0, len(xs), 2)]
        return xs[0]

    def step(s, slot, vslot, yslot):
        def bc(o, k):
            return jnp.broadcast_to(slot[o, pl.ds(s * HEAD_DIM + k, 1), :], (vr, LANES))

        def accumulate(acc, k, p):
            acc[k % N_ACC] = p if acc[k % N_ACC] is None else acc[k % N_ACC] + p

        vv = vslot[0, s]
        acc = [None] * N_ACC
        for k in range(HEAD_DIM):
            accumulate(acc, k, st_a[k] * bc(3, k))
        sa = -tree(acc)
        acc = [None] * N_ACC
        for k in range(HEAD_DIM):
            new = st_a[k] * bc(1, k) + sa * bc(4, k) + vv * bc(2, k)
            st_a[k] = new
            accumulate(acc, k, new * bc(0, k))
        yslot[0, s] = tree(acc)
        vv = vslot[1, s]
        acc = [None] * N_ACC
        for k in range(HEAD_DIM):
            new = st_b[k] * bc(6, k) + vv * bc(7, k)
            st_b[k] = new
            accumulate(acc, k, new * bc(5, k))
        yslot[1, s] = tree(acc)

    def pair_of_steps(p, slot, vslot, nslot, nvslot, yslot, yprev):
        emit(jnp.maximum(p - 1, 0), yprev)
        prepare(jnp.minimum(p + 1, n_pairs - 1), nslot, nvslot)
        step(0, slot, vslot, yslot)
        step(1, slot, vslot, yslot)

    prepare(0, slot_a, vslot_a)
    yslot_b[...] = jnp.zeros(yslot_b.shape, F32)

    def body(i, carry):
        pair_of_steps(2 * i, slot_a, vslot_a, slot_b, vslot_b, yslot_a, yslot_b)
        pair_of_steps(2 * i + 1, slot_b, vslot_b, slot_a, vslot_a, yslot_b, yslot_a)
        return carry

    lax.fori_loop(0, n_pairs // 2, body, 0)
    emit(n_pairs - 1, yslot_b)

    @pl.when(grid_step == pl.num_programs(0) - 1)
    def _():
        sa_out_ref[...] = st_a[...]
        sb_out_ref[...] = st_b[...]


def _seq(cols_a, va, cols_b, vb, sa0, sb0, bh):
    T = va.shape[0]
    g_sz = bh // 2
    vr = sa0.shape[1]
    tc = min(128, T)
    assert tc % 4 == 0
    view = lambda a: a.reshape(T, g_sz, LANES)
    ospec = pl.BlockSpec((tc, g_sz, LANES), lambda g: (g, 0, 0))
    sspec = pl.BlockSpec((HEAD_DIM, vr, LANES), lambda g: (0, 0, 0))
    args = [view(c) for c in (*cols_a, va, *cols_b, vb)] + [sa0, sb0]
    y_shape = jax.ShapeDtypeStruct((T, g_sz, LANES), F32)
    s_shape = jax.ShapeDtypeStruct((HEAD_DIM, vr, LANES), F32)
    slot = pltpu.VMEM((N_COL, 2 * HEAD_DIM, LANES), F32)
    small = pltpu.VMEM((2, 2, vr, LANES), F32)
    ya, yb, sa, sb = pl.pallas_call(
        functools.partial(_seq_kernel, tc, bh),
        grid=(T // tc,),
        in_specs=[ospec] * (len(args) - 2) + [sspec, sspec],
        out_specs=(ospec, ospec, sspec, sspec),
        out_shape=(y_shape, y_shape, s_shape, s_shape),
        scratch_shapes=[pltpu.VMEM((HEAD_DIM, vr, LANES), F32), pltpu.VMEM((HEAD_DIM, vr, LANES), F32),
                        slot, slot, small, small, small, small],
        compiler_params=_cparams(("arbitrary",)),
        name="seq",
    )(*args)
    return ya.reshape(T, bh * HEAD_DIM), yb.reshape(T, bh * HEAD_DIM), sa, sb


def _ret_kernel(L, q_ref, k_ref, v_ref, s0_ref, o_ref, s_out_ref, st, dm, gq, gk):
    i = pl.program_id(1)
    log2_gamma = [float(np.log2(1.0 - 2.0 ** (-5.0 - h))) for h in range(N_HEADS)]

    @pl.when(i == 0)
    def _():
        st[...] = s0_ref[0]
        t_idx = lax.broadcasted_iota(jnp.int32, (L, L), 0)
        s_idx = lax.broadcasted_iota(jnp.int32, (L, L), 1)
        row = lax.broadcasted_iota(jnp.int32, (L, HEAD_DIM), 0).astype(F32)
        for h in range(N_HEADS):
            d = (t_idx - s_idx).astype(F32)
            dm[h] = jnp.where(t_idx >= s_idx, jnp.exp2(d * log2_gamma[h]), 0.0)
            gq[h] = jnp.exp2((row + 1.0) * log2_gamma[h])
            gk[h] = jnp.exp2((L - 1.0 - row) * log2_gamma[h])

    for h in range(N_HEADS):
        hs = slice(h * HEAD_DIM, (h + 1) * HEAD_DIM)
        q, k, v = q_ref[0, :, hs], k_ref[0, :, hs], v_ref[0, :, hs].astype(BF16)
        a = lax.dot_general(q.astype(BF16), k.astype(BF16), (((1,), (1,)), ((), ())),
                            preferred_element_type=F32) * dm[h]
        s_prev = st[h]
        o_ref[0, :, hs] = _dot(a.astype(BF16), v) + _dot((q * gq[h]).astype(BF16), s_prev.astype(BF16))
        kd = (k * gk[h]).astype(BF16)
        st[h] = s_prev * (2.0 ** (L * log2_gamma[h])) + lax.dot_general(
            kd, v, (((0,), (0,)), ((), ())), preferred_element_type=F32)

    @pl.when(i == pl.num_programs(1) - 1)
    def _():
        s_out_ref[0] = st[...]


def _ret(q, k, v, s0):
    B, T, _ = q.shape
    L = min(512, T)
    tspec = pl.BlockSpec((1, L, BW), lambda b, i: (b, i, 0))
    sspec = pl.BlockSpec((1, N_HEADS, HEAD_DIM, HEAD_DIM), lambda b, i: (b, 0, 0, 0))
    return pl.pallas_call(
        functools.partial(_ret_kernel, L),
        grid=(B, T // L),
        in_specs=[tspec, tspec, tspec, sspec],
        out_specs=(tspec, sspec),
        out_shape=(jax.ShapeDtypeStruct((B, T, BW), F32),
                   jax.ShapeDtypeStruct((B, N_HEADS, HEAD_DIM, HEAD_DIM), F32)),
        scratch_shapes=[pltpu.VMEM((N_HEADS, HEAD_DIM, HEAD_DIM), F32), pltpu.VMEM((N_HEADS, L, L), F32),
                        pltpu.VMEM((N_HEADS, L, HEAD_DIM), F32), pltpu.VMEM((N_HEADS, L, HEAD_DIM), F32)],
        compiler_params=_cparams(("arbitrary", "arbitrary")),
        name="ret",
    )(q, k, v, s0)


def _kvprep_kernel(lat_ref, kr_ref, wk_ref, wv_ref, place_ref, k_ref, v_ref):
    lat = lat_ref[0].astype(BF16)
    k_ref[0] = (_dot(lat, wk_ref[...]) + _dot(kr_ref[0].astype(BF16), place_ref[...])).astype(BF16)
    lane = lax.broadcasted_iota(jnp.int32, (1, QP), 1) % HEAD_PAD
    v_ref[0] = jnp.where(lane == V_D, 1.0, _dot(lat, wv_ref[...])).astype(BF16)


def _kvprep(lat_all, kr_all, wk, wv, place, tr):
    B, tk, _ = lat_all.shape
    rspec = lambda c: pl.BlockSpec((1, tr, c), lambda b, i: (b, i, 0))
    full = lambda a: pl.BlockSpec(a.shape, lambda b, i: (0,) * a.ndim)
    return pl.pallas_call(
        _kvprep_kernel,
        grid=(B, tk // tr),
        in_specs=[rspec(KV_LORA), rspec(LANES), full(wk), full(wv), full(place)],
        out_specs=(rspec(QP), rspec(QP)),
        out_shape=(jax.ShapeDtypeStruct((B, tk, QP), BF16), jax.ShapeDtypeStruct((B, tk, QP), BF16)),
        compiler_params=_cparams(("arbitrary", "arbitrary")),
        name="kvprep",
    )(lat_all, kr_all, wk, wv, place)


def _attn_kernel(bq, bk, q0, kv_len, last_block, diag_only, q_ref, k_ref, v_ref, o_ref, m_ref, acc_ref):
    i = pl.program_id(1)
    j = pl.program_id(2)

    @pl.when(j == 0)
    def _():
        m_ref[...] = jnp.full(m_ref.shape, -jnp.inf, F32)
        acc_ref[...] = jnp.zeros(acc_ref.shape, F32)

    def block(masked):
        if masked:
            qi = q0 + i * bq + lax.broadcasted_iota(jnp.int32, (bq, bk), 0)
            ki = j * bk + lax.broadcasted_iota(jnp.int32, (bq, bk), 1)
            keep = jnp.logical_and(ki // CHUNK <= qi // CHUNK, ki < kv_len)
        for h in range(N_HEADS):
            hs = slice(h * HEAD_PAD, (h + 1) * HEAD_PAD)
            s = lax.dot_general(q_ref[0, :, hs], k_ref[0, :, hs], (((1,), (1,)), ((), ())),
                                preferred_element_type=F32)
            if masked:
                s = jnp.where(keep, s, -jnp.inf)
            m_old = m_ref[h]
            m_new = jnp.maximum(m_old, jnp.max(s, axis=-1, keepdims=True))
            alpha = jnp.exp2(m_old - m_new)
            p = jnp.exp2(s - m_new[:, 0:1])
            acc_ref[:, hs] = alpha * acc_ref[:, hs] + _dot(p.astype(BF16), v_ref[0, :, hs])
            m_ref[h] = m_new

    last_needed = (q0 + (i + 1) * bq - 1) // CHUNK * CHUNK + CHUNK - 1
    if diag_only:
        @pl.when(j < i)
        def _():
            block(False)

        @pl.when(j == i)
        def _():
            block(True)
    else:
        @pl.when(j * bk <= last_needed)
        def _():
            block(True)

    @pl.when(j == last_block)
    def _():
        for h in range(N_HEADS):
            a = acc_ref[:, h * HEAD_PAD:(h + 1) * HEAD_PAD]
            o_ref[0, :, h * V_D:(h + 1) * V_D] = a[:, :V_D] / a[:, V_D:V_D + 1]


def _attn(qp, kp, vp, q0, kv_len, bk):
    B, T, _ = qp.shape
    tk = kp.shape[1]
    bq = min(512, T)
    nq, nk = T // bq, tk // bk
    diag_only = q0 == 0 and bq == bk and T == tk and bq % CHUNK == 0
    if diag_only:
        kv_map = lambda b, i, j: (b, jnp.minimum(j, i), 0)
    else:
        kv_map = lambda b, i, j: (b, j, 0)
    return pl.pallas_call(
        functools.partial(_attn_kernel, bq, bk, q0, kv_len, nk - 1, diag_only),
        grid=(B, nq, nk),
        in_specs=[pl.BlockSpec((1, bq, QP), lambda b, i, j: (b, i, 0)),
                  pl.BlockSpec((1, bk, QP), kv_map), pl.BlockSpec((1, bk, QP), kv_map)],
        out_specs=pl.BlockSpec((1, bq, BW), lambda b, i, j: (b, i, 0)),
        out_shape=jax.ShapeDtypeStruct((B, T, BW), F32),
        scratch_shapes=[pltpu.VMEM((N_HEADS, bq, LANES), F32), pltpu.VMEM((bq, QP), F32)],
        compiler_params=_cparams(("arbitrary", "arbitrary", "arbitrary")),
        name="attn",
    )(qp, kp, vp)


def _merge_kernel(final, h_ref, ya_ref, posta_ref, ob_ref, gb_ref, oc_ref, gc_ref, od_ref, gd_ref,
                  par_ref, seg_ref, wg_ref, wb_ref, wo_ref, out_ref):
    par = lambda r, n: par_ref[r:r + 1, 0:n]
    seg = seg_ref[...]
    h = h_ref[0]
    u = _rms(h, par(R_NORM, D_MODEL), NORM_EPS).astype(BF16)
    inv = 1.0 / HEAD_DIM

    ya = ya_ref[...]
    xc = ya - _dot_lhs2(ya, seg) * inv
    yn = xc * lax.rsqrt(_dot_lhs2(xc * xc, seg) * inv + GN_EPS) * par(R_LNW, BW) + par(R_LNB, BW)
    oa = (yn + posta_ref[:, :BW]) * posta_ref[:, BW:]

    def head_rms(o, g):
        return o * lax.rsqrt(_dot_lhs2(o * o, seg) * inv + HEAD_NORM_EPS) * g

    ob = head_rms(ob_ref[...], par(R_HG, BW)) * gb_ref[...]
    oc = oc_ref[0] * gc_ref[0]
    od = head_rms(od_ref[0], par(R_RG, BW)) * gd_ref[0]

    merged = None
    for n, o in enumerate((oa, ob, oc, od)):
        gate = _sigmoid(_dot(u, wg_ref[:, n * D_MODEL:(n + 1) * D_MODEL]))
        term = gate * _dot(o.astype(BF16), wb_ref[n * BW:(n + 1) * BW, :])
        merged = term if merged is None else merged + term
    out = h + _dot(merged.astype(BF16), wo_ref[...])
    if final:
        out = _rms(out, par(R_FG, D_MODEL), NORM_EPS)
    out_ref[0] = out


def _merge(final, h, ya, posta, ob, gb, oc, gc, od, gd, par, seg, wg, wb, wo):
    B, T, _ = h.shape
    tt = min(512, T)
    bspec = lambda c: pl.BlockSpec((1, tt, c), lambda t, b: (b, t, 0))
    tspec = lambda c: pl.BlockSpec((tt, c), lambda t, b: (t, b))
    full = lambda a: pl.BlockSpec(a.shape, lambda t, b: (0,) * a.ndim)
    consts = (par, seg, wg, wb, wo)
    return pl.pallas_call(
        functools.partial(_merge_kernel, final),
        grid=(T // tt, B),
        in_specs=[bspec(D_MODEL), tspec(BW), tspec(2 * BW), tspec(BW), tspec(BW), bspec(BW), bspec(BW), bspec(BW),
                  bspec(BW)] + [full(a) for a in consts],
        out_specs=bspec(D_MODEL),
        out_shape=jax.ShapeDtypeStruct((B, T, D_MODEL), F32),
        compiler_params=_cparams(("arbitrary", "arbitrary")),
        name="merge",
    )(h, ya, posta, ob, gb, oc, gc, od, gd, *consts)


def _rope_tables(past, T):
    pos = (past + jnp.arange(T, dtype=jnp.int32)).astype(F32)[:, None]

    def tables(width, group, start, d):
        lane = np.arange(width) % group - start
        on = (lane >= 0) & (lane < d)
        idx = np.where(on, lane % (d // 2), 0)
        inv_freq = jnp.power(ROPE_BASE, -jnp.arange(0, d, 2, dtype=F32) / d)
        ang = pos * inv_freq[None, :]
        cos = jnp.where(on[None, :], jnp.cos(ang)[:, idx], 1.0)
        sign = np.where(lane < d // 2, -1.0, 1.0).astype(np.float32)
        sin = jnp.where(on[None, :], jnp.sin(ang)[:, idx] * sign[None, :], 0.0)
        return cos, sin

    cd, sd = tables(BW, HEAD_DIM, 0, HEAD_DIM)
    cq, sq = tables(QP, HEAD_PAD, NOPE, ROPE_D)
    ck, sk = tables(LANES, LANES, 0, ROPE_D)
    return jnp.concatenate([cd, sd, cq, sq, ck, sk], axis=1)


def _layer_weights(l, p):
    w_in = p["w_in"][l]
    gate_cols = N_BRANCH * D_MODEL
    c_kr = gate_cols + SHIFT_COLS + BW + 4 * BW + Q_LORA + KV_LORA
    c_g = c_kr + ROPE_D
    w_rest = jnp.concatenate([
        w_in[:, gate_cols:c_kr],
        jnp.pad(w_in[:, c_kr:c_g], ((0, 0), (0, LANES - ROPE_D))),
        w_in[:, c_g:],
    ], axis=1).astype(BF16)
    wg = w_in[:, :gate_cols].astype(BF16)

    z = jnp.zeros((64, BW), F32)
    wl = jnp.concatenate([jnp.concatenate([p["rwkv_w_up"][l], z], axis=1),
                          jnp.concatenate([z, p["rwkv_a_up"][l]], axis=1)], axis=0)
    wl_hi = wl.astype(BF16)
    wl_lo = (wl - wl_hi.astype(F32)).astype(BF16)

    wq = p["mla_w_q_up"][l].reshape(Q_LORA, N_HEADS, QK_D)
    wq = jnp.pad(wq, ((0, 0), (0, 0), (0, HEAD_PAD - QK_D))).reshape(Q_LORA, QP).astype(BF16)
    wkv = p["mla_w_kv_up"][l].reshape(KV_LORA, N_HEADS, NOPE + V_D)
    wk = jnp.pad(wkv[:, :, :NOPE], ((0, 0), (0, 0), (0, HEAD_PAD - NOPE))).reshape(KV_LORA, QP).astype(BF16)
    wv = jnp.pad(wkv[:, :, NOPE:], ((0, 0), (0, 0), (0, HEAD_PAD - V_D))).reshape(KV_LORA, QP).astype(BF16)

    wb = p["w_branch"][l].reshape(N_BRANCH * BW, D_MODEL).astype(BF16)
    wo = p["w_out"][l].astype(BF16)

    def row(v):
        v = v.reshape(1, -1).astype(F32)
        return jnp.pad(v, ((0, 0), (0, D_MODEL - v.shape[1])))

    rows = [row(p["norm_g"][l]), row(p["rwkv_mu"][l]), row(p["rwkv_w0"][l]), row(p["rwkv_a0"][l]),
            row(p["rwkv_k_k"][l]), row(p["rwkv_k_a"][l]), row(p["rwkv_r_k"][l])]
    rows += [row(p["hgrn_lb_logits"][j]) for j in range(DEPTH)]
    rows += [row(p["mla_q_norm_g"][l]), row(p["mla_kv_norm_g"][l]), row(p["rwkv_ln_w"][l]), row(p["rwkv_ln_b"][l]),
             row(p["hgrn_norm_g"][l]), row(p["ret_norm_g"][l]), row(p["final_norm_g"])]
    par = jnp.concatenate(rows + [jnp.zeros((N_ROWS - len(rows), D_MODEL), F32)], axis=0)
    return dict(w_rest=w_rest, wg=wg, wl_hi=wl_hi, wl_lo=wl_lo, wq=wq, wk=wk, wv=wv, wb=wb, wo=wo, par=par)


def _constants():
    head = np.arange(BW) // HEAD_DIM
    seg = jnp.asarray((head[:, None] == head[None, :]).astype(np.float32), dtype=BF16)
    place = np.zeros((LANES, QP), np.float32)
    for h in range(N_HEADS):
        for j in range(ROPE_D):
            place[j, h * HEAD_PAD + NOPE + j] = 1.0
    return seg, jnp.asarray(place, dtype=BF16)


def _state_in(s, B, rep, key_last):
    vr = HEAD_DIM // rep
    if key_last:
        y = s.reshape(B, 2, 2, rep, vr, HEAD_DIM).transpose(5, 4, 3, 2, 0, 1)
    else:
        y = s.reshape(B, 2, 2, HEAD_DIM, rep, vr).transpose(3, 5, 4, 2, 0, 1)
    return y.reshape(HEAD_DIM, vr, LANES).astype(F32)


def _state_out(y, B, rep, key_last):
    vr = HEAD_DIM // rep
    y = y.reshape(HEAD_DIM, vr, rep, 2, B, 2)
    if key_last:
        y = y.transpose(4, 5, 3, 2, 1, 0)
    else:
        y = y.transpose(4, 5, 3, 0, 2, 1)
    return y.reshape(B, N_HEADS, HEAD_DIM, HEAD_DIM)


def _run_trunk(x, shift0, rwkv0, hgrn0, ret0, lat0, kr0, p, weights, consts):
    B, T, _ = x.shape
    past = lat0.shape[2]
    bh = B * N_HEADS
    rep = LANES // bh
    assert rep * bh == LANES and HEAD_DIM % rep == 0 and (HEAD_DIM // rep) % SUBLANES == 0 and N_HEADS == 4
    seg, place = consts
    tab = _rope_tables(past, T)
    h = x
    per_layer = []
    for l in range(DEPTH):
        w = weights[l]
        (ar, aw, ak, av, akk, ab, posta, bq, bf, bv, gb, qp, lat, kr, gc, dq, dk, dv, gd, shn) = _proj(
            l, h, shift0[l][:, None, :], tab, w["par"], w["w_rest"], w["wl_hi"], w["wl_lo"], seg, w["wq"])

        ya, ob, s_rwkv, s_hgrn = _seq([ar, aw, ak, akk, ab], av, [bq, bf], bv, _state_in(rwkv0[l], B, rep, True),
                                      _state_in(hgrn0[l], B, rep, False), bh)
        od, s_ret = _ret(dq, dk, dv, ret0[l].astype(F32))

        lat_all = jnp.concatenate([lat0[l], lat], axis=1)
        kr_all = jnp.concatenate([jnp.pad(kr0[l], ((0, 0), (0, 0), (0, LANES - ROPE_D))), kr], axis=1)
        tk = past + T
        blk = min(512, T) if past == 0 else 256
        tk_pad = -(-tk // blk) * blk
        if tk_pad != tk:
            lat_all = jnp.pad(lat_all, ((0, 0), (0, tk_pad - tk), (0, 0)))
            kr_all = jnp.pad(kr_all, ((0, 0), (0, tk_pad - tk), (0, 0)))
        kp, vp = _kvprep(lat_all, kr_all, w["wk"], w["wv"], place, blk)
        oc = _attn(qp, kp, vp, past, tk, blk)

        h = _merge(l == DEPTH - 1, h, ya, posta, ob, gb, oc, gc, od, gd, w["par"], seg, w["wg"], w["wb"], w["wo"])
        per_layer.append((shn[-1, :, 0, :], _state_out(s_rwkv, B, rep, True), _state_out(s_hgrn, B, rep, False),
                          s_ret, lat, kr[:, :, :ROPE_D]))
    stacked = [jnp.stack([st[j] for st in per_layer]) for j in range(6)]
    return h, stacked


def kernel(x_prompt, x_sample, state_rwkv_shift, state_rwkv, state_hgrn, cache_mla_latent, cache_mla_krope, state_ret, norm_g, w_in, rwkv_mu, rwkv_w0, rwkv_w_up, rwkv_a0, rwkv_a_up, rwkv_k_k, rwkv_k_a, rwkv_r_k, rwkv_ln_w, rwkv_ln_b, hgrn_lb_logits, hgrn_norm_g, mla_q_norm_g, mla_w_q_up, mla_kv_norm_g, mla_w_kv_up, ret_norm_g, w_branch, w_out, final_norm_g):
    p = dict(norm_g=norm_g, w_in=w_in, rwkv_mu=rwkv_mu, rwkv_w0=rwkv_w0, rwkv_w_up=rwkv_w_up, rwkv_a0=rwkv_a0,
             rwkv_a_up=rwkv_a_up, rwkv_k_k=rwkv_k_k, rwkv_k_a=rwkv_k_a, rwkv_r_k=rwkv_r_k, rwkv_ln_w=rwkv_ln_w,
             rwkv_ln_b=rwkv_ln_b, hgrn_lb_logits=hgrn_lb_logits, hgrn_norm_g=hgrn_norm_g, mla_q_norm_g=mla_q_norm_g,
             mla_w_q_up=mla_w_q_up, mla_kv_norm_g=mla_kv_norm_g, mla_w_kv_up=mla_w_kv_up, ret_norm_g=ret_norm_g,
             w_branch=w_branch, w_out=w_out, final_norm_g=final_norm_g)
    weights = [_layer_weights(l, p) for l in range(DEPTH)]
    consts = _constants()
    bp, dt = x_prompt.shape[0], x_prompt.dtype
    zs = jnp.zeros((DEPTH, bp, N_HEADS, HEAD_DIM, HEAD_DIM), dt)
    y_p, (p_shift, p_rwkv, p_hgrn, p_ret, p_lat, p_kr) = _run_trunk(
        x_prompt, jnp.zeros((DEPTH, bp, SHIFT_COLS), dt), zs, zs, zs, jnp.zeros((DEPTH, bp, 0, KV_LORA), dt),
        jnp.zeros((DEPTH, bp, 0, ROPE_D), dt), p, weights, consts)
    y_s, (s_shift, s_rwkv, s_hgrn, s_ret, s_lat, s_kr) = _run_trunk(
        x_sample, state_rwkv_shift, state_rwkv, state_hgrn, state_ret, cache_mla_latent, cache_mla_krope,
        p, weights, consts)
    return (y_p, y_s, p_shift, s_shift, p_rwkv, s_rwkv, p_hgrn, s_hgrn, p_lat, s_lat, p_kr, s_kr, p_ret, s_ret)
```

```python
import functools

import jax
import jax.numpy as jnp
import numpy as np
from jax import lax
from jax.experimental import pallas as pl
from jax.experimental.pallas import tpu as pltpu

F32 = jnp.float32
BF16 = jnp.bfloat16

D_MODEL = 1024
DEPTH = 4
N_HEADS = 4
HEAD_DIM = 64
BW = N_HEADS * HEAD_DIM
N_BRANCH = 4
CHUNK = 64
SHIFT_COLS = 3 * BW + 64 + 64
GN_EPS = 64e-5
NORM_EPS = 1e-6
HEAD_NORM_EPS = 1e-5
ROPE_BASE = 10000.0
Q_LORA, KV_LORA, NOPE, ROPE_D, V_D = 256, 128, 64, 32, 64
QK_D = NOPE + ROPE_D
HEAD_PAD = 128
Q_SCALE = QK_D ** -0.5 * 1.4426950408889634
QP = N_HEADS * HEAD_PAD
LANES = 128
SUBLANES = 8

A0, A1 = 0, SHIFT_COLS + BW
B0, B1 = A1, A1 + 4 * BW
C0, C1 = B1, B1 + Q_LORA + KV_LORA + LANES + BW
D0, D1 = C1, C1 + 4 * BW
PROJ_COLS = D1

T_CD, T_SD, T_CQ, T_SQ, T_CK, T_SK, TAB_COLS = 0, 256, 512, 1024, 1536, 1664, 1792

(R_NORM, R_MU, R_W0, R_A0, R_KK, R_KA, R_RK, R_LB, R_QG, R_KVG,
 R_LNW, R_LNB, R_HG, R_RG, R_FG, N_ROWS) = (0, 1, 2, 3, 4, 5, 6, 7, 11, 12, 13, 14, 15, 16, 17, 24)

VMEM_LIMIT = 50 * 1024 * 1024


def _cparams(sem):
    return pltpu.CompilerParams(dimension_semantics=sem, vmem_limit_bytes=VMEM_LIMIT)


def _sigmoid(x):
    return 1.0 / (1.0 + jnp.exp(-x))


def _silu(x):
    return x * _sigmoid(x)


def _softplus(x):
    return jnp.maximum(x, 0.0) + jnp.log(1.0 + jnp.exp(-jnp.abs(x)))


def _dot(a, b):
    return jnp.dot(a, b, preferred_element_type=F32)


def _split(x):
    hi = x.astype(BF16)
    lo = (x - hi.astype(F32)).astype(BF16)
    return hi, lo


def _dot_lhs2(x, m_bf16):
    hi, lo = _split(x)
    return _dot(hi, m_bf16) + _dot(lo, m_bf16)


def _dot3(x, m_hi, m_lo):
    hi, lo = _split(x)
    return _dot(hi, m_hi) + _dot(lo, m_hi) + _dot(hi, m_lo)


def _rms(x, g, eps):
    return x * lax.rsqrt(jnp.mean(x * x, axis=-1, keepdims=True) + eps) * g


def _rope(x, cos, sin_signed, half, group, start=0):
    w = x.shape[-1]
    lane = lax.broadcasted_iota(jnp.int32, (1, w), 1) % group
    up = pltpu.roll(x, w - half, axis=1)
    dn = pltpu.roll(x, half, axis=1)
    return x * cos + jnp.where(lane < start + half, up, dn) * sin_signed


def _proj_kernel(layer, tt, h_ref, sh0_ref, tab_ref, par_ref, w_ref, wl_hi_ref, wl_lo_ref, seg_ref, wq_ref,
                 ar_ref, aw_ref, ak_ref, av_ref, akk_ref, ab_ref, posta_ref,
                 bq_ref, bf_ref, bv_ref, gb_ref,
                 qp_ref, lat_ref, kr_ref, gc_ref,
                 dq_ref, dk_ref, dv_ref, gd_ref, shn_ref, carry_ref):
    t = pl.program_id(0)
    b = pl.program_id(1)
    par = lambda r, n: par_ref[r:r + 1, 0:n]
    seg = seg_ref[...]

    u = _rms(h_ref[0], par(R_NORM, D_MODEL), NORM_EPS).astype(BF16)

    pa_all = _dot(u, w_ref[:, A0:A1])
    pa = pa_all[:, :SHIFT_COLS]
    prev_row = jnp.where(t == 0, sh0_ref[b], carry_ref[b])
    row = lax.broadcasted_iota(jnp.int32, (tt, 1), 0)
    p_prev = jnp.where(row == 0, prev_row, pltpu.roll(pa, 1, axis=0))
    last = pa[tt - 1:tt, :]
    carry_ref[b] = last
    shn_ref[0, 0] = last
    xs = pa + (p_prev - pa) * par(R_MU, SHIFT_COLS)
    r, k, v, wdad = xs[:, :BW], xs[:, BW:2 * BW], xs[:, 2 * BW:3 * BW], xs[:, 3 * BW:]
    lane = lax.broadcasted_iota(jnp.int32, (1, LANES), 1)
    lora = _dot3(jnp.where(lane < 64, jnp.tanh(wdad), wdad), wl_hi_ref[...], wl_lo_ref[...])
    w_raw = -_softplus(-(par(R_W0, BW) + lora[:, :BW])) - 0.5
    a = _sigmoid(par(R_A0, BW) + lora[:, BW:])
    kk = k * par(R_KK, BW)
    kk = kk / jnp.maximum(jnp.sqrt(_dot_lhs2(kk * kk, seg)), 1e-12)
    k2 = k * (1.0 + (a - 1.0) * par(R_KA, BW))
    ar_ref[...] = r
    aw_ref[...] = jnp.exp(-jnp.exp(w_raw))
    ak_ref[...] = k2
    av_ref[...] = v
    akk_ref[...] = kk
    ab_ref[...] = kk * a
    posta_ref[:, :BW] = _dot_lhs2(r * k2 * par(R_RK, BW), seg) * v
    posta_ref[:, BW:] = _silu(pa_all[:, SHIFT_COLS:])

    pb = _dot(u, w_ref[:, B0:B1])
    lg = par_ref[R_LB:R_LB + DEPTH, 0:BW]
    e = jnp.exp(lg - jnp.max(lg, axis=0, keepdims=True))
    lb = jnp.zeros((1, BW), F32)
    for j in range(1, layer + 1):
        lb = lb + e[j:j + 1]
    lb = lb / jnp.sum(e, axis=0, keepdims=True)
    z = pb[:, BW:2 * BW]
    bq_ref[...] = _silu(pb[:, :BW]) * (HEAD_DIM ** -0.5)
    bf_ref[...] = lb + (1.0 - lb) * _sigmoid(z)
    bv_ref[...] = pb[:, 2 * BW:3 * BW]
    gb_ref[...] = _silu(pb[:, 3 * BW:])

    pc = _dot(u, w_ref[:, C0:C1])
    qn = _rms(pc[:, :Q_LORA], par(R_QG, Q_LORA), NORM_EPS).astype(BF16)
    q = _dot(qn, wq_ref[...]) * Q_SCALE
    qp_ref[0] = _rope(q, tab_ref[:, T_CQ:T_CQ + QP], tab_ref[:, T_SQ:T_SQ + QP], ROPE_D // 2, HEAD_PAD,
                      NOPE).astype(BF16)
    lat_ref[0] = _rms(pc[:, Q_LORA:Q_LORA + KV_LORA], par(R_KVG, KV_LORA), NORM_EPS)
    kr_ref[0] = _rope(pc[:, Q_LORA + KV_LORA:Q_LORA + KV_LORA + LANES], tab_ref[:, T_CK:T_CK + LANES],
                      tab_ref[:, T_SK:T_SK + LANES], ROPE_D // 2, LANES)
    gc_ref[0] = _silu(pc[:, Q_LORA + KV_LORA + LANES:])

    pd = _dot(u, w_ref[:, D0:D1])
    cos, sin = tab_ref[:, T_CD:T_CD + BW], tab_ref[:, T_SD:T_SD + BW]
    dq_ref[0] = _rope(pd[:, :BW], cos, sin, HEAD_DIM // 2, HEAD_DIM)
    dk_ref[0] = _rope(pd[:, BW:2 * BW], cos, sin, HEAD_DIM // 2, HEAD_DIM) * (HEAD_DIM ** -0.5)
    dv_ref[0] = pd[:, 2 * BW:3 * BW]
    gd_ref[0] = _silu(pd[:, 3 * BW:])


def _proj(layer, h, shift0, tab, par, w_rest, wl_hi, wl_lo, seg, wq):
    B, T, _ = h.shape
    tt = min(256, T)
    nt = T // tt
    bspec = lambda c: pl.BlockSpec((1, tt, c), lambda t, b: (b, t, 0))
    tspec = lambda c: pl.BlockSpec((tt, c), lambda t, b: (t, b))
    full = lambda a: pl.BlockSpec(a.shape, lambda t, b: (0,) * a.ndim)
    bm = lambda c, dt=F32: jax.ShapeDtypeStruct((B, T, c), dt)
    tm = lambda c: jax.ShapeDtypeStruct((T, B * c), F32)
    outs = ([(tm(BW), tspec(BW))] * 6 + [(tm(2 * BW), tspec(2 * BW))]
            + [(tm(BW), tspec(BW))] * 4
            + [(bm(QP, BF16), bspec(QP)), (bm(KV_LORA), bspec(KV_LORA)), (bm(LANES), bspec(LANES)),
               (bm(BW), bspec(BW))]
            + [(bm(BW), bspec(BW))] * 4
            + [(jax.ShapeDtypeStruct((nt, B, 1, SHIFT_COLS), F32),
                pl.BlockSpec((1, 1, 1, SHIFT_COLS), lambda t, b: (t, b, 0, 0)))])
    return pl.pallas_call(
        functools.partial(_proj_kernel, layer, tt),
        grid=(nt, B),
        in_specs=[bspec(D_MODEL), full(shift0), pl.BlockSpec((tt, TAB_COLS), lambda t, b: (t, 0)), full(par),
                  full(w_rest), full(wl_hi), full(wl_lo), full(seg), full(wq)],
        out_specs=tuple(o[1] for o in outs),
        out_shape=tuple(o[0] for o in outs),
        scratch_shapes=[pltpu.VMEM((B, 1, SHIFT_COLS), F32)],
        compiler_params=_cparams(("arbitrary", "arbitrary")),
        name="proj",
    )(h, shift0, tab, par, w_rest, wl_hi, wl_lo, seg, wq)


N_ACC = 4


N_COL = 8
LOOKAHEAD = 2
N_SLOTS = 4


def _seq_kernel(tc, bh, r_ref, w_ref, k_ref, kk_ref, b_ref, va_ref, q_ref, f_ref, vb_ref, sa0_ref, sb0_ref,
                ya_ref, yb_ref, sa_out_ref, sb_out_ref, st_a, st_b, *ring):
    slots, vslots, yslots = ring[0:N_SLOTS], ring[N_SLOTS:2 * N_SLOTS], ring[2 * N_SLOTS:3 * N_SLOTS]
    g_sz = bh // 2
    rep = LANES // bh
    vr = HEAD_DIM // rep
    n_pairs = tc // 2
    lane = lax.broadcasted_iota(jnp.int32, (1, LANES), 1)
    low = lane < HEAD_DIM
    hi_id = lane // bh
    grid_step = pl.program_id(0)

    @pl.when(grid_step == 0)
    def _():
        st_a[...] = sa0_ref[...]
        st_b[...] = sb0_ref[...]

    def paired(ref, p):
        a, b = ref[2 * p], ref[2 * p + 1]
        s0 = jnp.where(low, a, pltpu.roll(b, HEAD_DIM, axis=1))
        s1 = jnp.where(low, pltpu.roll(a, HEAD_DIM, axis=1), b)
        return jnp.concatenate([s0, s1] * rep, axis=0).T

    def prepare(p, slot, vslot):
        for o, ref in enumerate((r_ref, w_ref, k_ref, kk_ref, b_ref, q_ref, f_ref)):
            xt = paired(ref, p)
            slot[o] = xt
            if ref is f_ref:
                slot[o + 1] = 1.0 - xt
        for ri, ref in enumerate((va_ref, vb_ref)):
            xt = paired(ref, p)
            for s in range(2):
                vv = xt[s * HEAD_DIM:s * HEAD_DIM + vr]
                for g in range(1, rep):
                    vv = jnp.where(hi_id == g, xt[s * HEAD_DIM + g * vr:s * HEAD_DIM + (g + 1) * vr], vv)
                vslot[ri, s] = vv

    def emit(p, yslot):
        for ri, y_ref in enumerate((ya_ref, yb_ref)):
            blocks = [jnp.where(hi_id == g, yslot[ri, s], 0.0) for s in range(2) for g in range(rep)]
            zt = jnp.concatenate(blocks, axis=0).T
            yp = zt[0:bh]
            for g in range(1, rep):
                yp = yp + zt[g * bh:(g + 1) * bh]
            p0, p1 = yp[:g_sz], yp[g_sz:]
            y_ref[2 * p] = jnp.where(low, p0, pltpu.roll(p1, HEAD_DIM, axis=1))
            y_ref[2 * p + 1] = jnp.where(low, pltpu.roll(p0, HEAD_DIM, axis=1), p1)

    def tree(xs):
        while len(xs) > 1:
            xs = [xs[i] + xs[i + 1] for i in range(0, len(xs), 2)]
        return xs[0]

    def step(s, slot, vslot, yslot):
        def bc(o, k):
            return jnp.broadcast_to(slot[o, pl.ds(s * HEAD_DIM + k, 1), :], (vr, LANES))

        def accumulate(acc, k, p):
            acc[k % N_ACC] = p if acc[k % N_ACC] is None else acc[k % N_ACC] + p

        vv = vslot[0, s]
        acc = [None] * N_ACC
        for k in range(HEAD_DIM):
            accumulate(acc, k, st_a[k] * bc(3, k))
        sa = -tree(acc)
        acc = [None] * N_ACC
        for k in range(HEAD_DIM):
            new = st_a[k] * bc(1, k) + sa * bc(4, k) + vv * bc(2, k)
            st_a[k] = new
            accumulate(acc, k, new * bc(0, k))
        yslot[0, s] = tree(acc)
        vv = vslot[1, s]
        acc = [None] * N_ACC
        for k in range(HEAD_DIM):
            new = st_b[k] * bc(6, k) + vv * bc(7, k)
            st_b[k] = new
            accumulate(acc, k, new * bc(5, k))
        yslot[1, s] = tree(acc)

    def pair_of_steps(p, j):
        emit(jnp.maximum(p - 1, 0), yslots[(j - 1) % N_SLOTS])
        nxt = (j + LOOKAHEAD) % N_SLOTS
        prepare(jnp.minimum(p + LOOKAHEAD, n_pairs - 1), slots[nxt], vslots[nxt])
        step(0, slots[j], vslots[j], yslots[j])
        step(1, slots[j], vslots[j], yslots[j])

    for j in range(LOOKAHEAD):
        prepare(j, slots[j], vslots[j])
    yslots[N_SLOTS - 1][...] = jnp.zeros(yslots[N_SLOTS - 1].shape, F32)

    def body(i, carry):
        for j in range(N_SLOTS):
            pair_of_steps(N_SLOTS * i + j, j)
        return carry

    lax.fori_loop(0, n_pairs // N_SLOTS, body, 0)
    emit(n_pairs - 1, yslots[N_SLOTS - 1])

    @pl.when(grid_step == pl.num_programs(0) - 1)
    def _():
        sa_out_ref[...] = st_a[...]
        sb_out_ref[...] = st_b[...]


def _seq(cols_a, va, cols_b, vb, sa0, sb0, bh):
    T = va.shape[0]
    g_sz = bh // 2
    vr = sa0.shape[1]
    tc = min(128, T)
    assert tc % (2 * N_SLOTS) == 0
    view = lambda a: a.reshape(T, g_sz, LANES)
    ospec = pl.BlockSpec((tc, g_sz, LANES), lambda g: (g, 0, 0))
    sspec = pl.BlockSpec((HEAD_DIM, vr, LANES), lambda g: (0, 0, 0))
    args = [view(c) for c in (*cols_a, va, *cols_b, vb)] + [sa0, sb0]
    y_shape = jax.ShapeDtypeStruct((T, g_sz, LANES), F32)
    s_shape = jax.ShapeDtypeStruct((HEAD_DIM, vr, LANES), F32)
    slot = pltpu.VMEM((N_COL, 2 * HEAD_DIM, LANES), F32)
    small = pltpu.VMEM((2, 2, vr, LANES), F32)
    ya, yb, sa, sb = pl.pallas_call(
        functools.partial(_seq_kernel, tc, bh),
        grid=(T // tc,),
        in_specs=[ospec] * (len(args) - 2) + [sspec, sspec],
        out_specs=(ospec, ospec, sspec, sspec),
        out_shape=(y_shape, y_shape, s_shape, s_shape),
        scratch_shapes=[pltpu.VMEM((HEAD_DIM, vr, LANES), F32), pltpu.VMEM((HEAD_DIM, vr, LANES), F32),
                        *([slot] * N_SLOTS + [small] * (2 * N_SLOTS))],
        compiler_params=_cparams(("arbitrary",)),
        name="seq",
    )(*args)
    return ya.reshape(T, bh * HEAD_DIM), yb.reshape(T, bh * HEAD_DIM), sa, sb


def _ret_kernel(L, q_ref, k_ref, v_ref, s0_ref, o_ref, s_out_ref, st, dm, gq, gk):
    i = pl.program_id(1)
    log2_gamma = [float(np.log2(1.0 - 2.0 ** (-5.0 - h))) for h in range(N_HEADS)]

    @pl.when(i == 0)
    def _():
        st[...] = s0_ref[0]
        t_idx = lax.broadcasted_iota(jnp.int32, (L, L), 0)
        s_idx = lax.broadcasted_iota(jnp.int32, (L, L), 1)
        row = lax.broadcasted_iota(jnp.int32, (L, HEAD_DIM), 0).astype(F32)
        for h in range(N_HEADS):
            d = (t_idx - s_idx).astype(F32)
            dm[h] = jnp.where(t_idx >= s_idx, jnp.exp2(d * log2_gamma[h]), 0.0)
            gq[h] = jnp.exp2((row + 1.0) * log2_gamma[h])
            gk[h] = jnp.exp2((L - 1.0 - row) * log2_gamma[h])

    for h in range(N_HEADS):
        hs = slice(h * HEAD_DIM, (h + 1) * HEAD_DIM)
        q, k, v = q_ref[0, :, hs], k_ref[0, :, hs], v_ref[0, :, hs].astype(BF16)
        a = lax.dot_general(q.astype(BF16), k.astype(BF16), (((1,), (1,)), ((), ())),
                            preferred_element_type=F32) * dm[h]
        s_prev = st[h]
        o_ref[0, :, hs] = _dot(a.astype(BF16), v) + _dot((q * gq[h]).astype(BF16), s_prev.astype(BF16))
        kd = (k * gk[h]).astype(BF16)
        st[h] = s_prev * (2.0 ** (L * log2_gamma[h])) + lax.dot_general(
            kd, v, (((0,), (0,)), ((), ())), preferred_element_type=F32)

    @pl.when(i == pl.num_programs(1) - 1)
    def _():
        s_out_ref[0] = st[...]


def _ret(q, k, v, s0):
    B, T, _ = q.shape
    L = min(512, T)
    tspec = pl.BlockSpec((1, L, BW), lambda b, i: (b, i, 0))
    sspec = pl.BlockSpec((1, N_HEADS, HEAD_DIM, HEAD_DIM), lambda b, i: (b, 0, 0, 0))
    return pl.pallas_call(
        functools.partial(_ret_kernel, L),
        grid=(B, T // L),
        in_specs=[tspec, tspec, tspec, sspec],
        out_specs=(tspec, sspec),
        out_shape=(jax.ShapeDtypeStruct((B, T, BW), F32),
                   jax.ShapeDtypeStruct((B, N_HEADS, HEAD_DIM, HEAD_DIM), F32)),
        scratch_shapes=[pltpu.VMEM((N_HEADS, HEAD_DIM, HEAD_DIM), F32), pltpu.VMEM((N_HEADS, L, L), F32),
                        pltpu.VMEM((N_HEADS, L, HEAD_DIM), F32), pltpu.VMEM((N_HEADS, L, HEAD_DIM), F32)],
        compiler_params=_cparams(("arbitrary", "arbitrary")),
        name="ret",
    )(q, k, v, s0)


def _kvprep_kernel(lat_ref, kr_ref, wk_ref, wv_ref, place_ref, k_ref, v_ref):
    lat = lat_ref[0].astype(BF16)
    k_ref[0] = (_dot(lat, wk_ref[...]) + _dot(kr_ref[0].astype(BF16), place_ref[...])).astype(BF16)
    lane = lax.broadcasted_iota(jnp.int32, (1, QP), 1) % HEAD_PAD
    v_ref[0] = jnp.where(lane == V_D, 1.0, _dot(lat, wv_ref[...])).astype(BF16)


def _kvprep(lat_all, kr_all, wk, wv, place, tr):
    B, tk, _ = lat_all.shape
    rspec = lambda c: pl.BlockSpec((1, tr, c), lambda b, i: (b, i, 0))
    full = lambda a: pl.BlockSpec(a.shape, lambda b, i: (0,) * a.ndim)
    return pl.pallas_call(
        _kvprep_kernel,
        grid=(B, tk // tr),
        in_specs=[rspec(KV_LORA), rspec(LANES), full(wk), full(wv), full(place)],
        out_specs=(rspec(QP), rspec(QP)),
        out_shape=(jax.ShapeDtypeStruct((B, tk, QP), BF16), jax.ShapeDtypeStruct((B, tk, QP), BF16)),
        compiler_params=_cparams(("arbitrary", "arbitrary")),
        name="kvprep",
    )(lat_all, kr_all, wk, wv, place)


def _attn_kernel(bq, bk, q0, kv_len, last_block, diag_only, q_ref, k_ref, v_ref, o_ref, m_ref, acc_ref):
    i = pl.program_id(1)
    j = pl.program_id(2)

    @pl.when(j == 0)
    def _():
        m_ref[...] = jnp.full(m_ref.shape, -jnp.inf, F32)
        acc_ref[...] = jnp.zeros(acc_ref.shape, F32)

    def block(masked):
        if masked:
            qi = q0 + i * bq + lax.broadcasted_iota(jnp.int32, (bq, bk), 0)
            ki = j * bk + lax.broadcasted_iota(jnp.int32, (bq, bk), 1)
            keep = jnp.logical_and(ki // CHUNK <= qi // CHUNK, ki < kv_len)
        for h in range(N_HEADS):
            hs = slice(h * HEAD_PAD, (h + 1) * HEAD_PAD)
            s = lax.dot_general(q_ref[0, :, hs], k_ref[0, :, hs], (((1,), (1,)), ((), ())),
                                preferred_element_type=F32)
            if masked:
                s = jnp.where(keep, s, -jnp.inf)
            m_old = m_ref[h]
            m_new = jnp.maximum(m_old, jnp.max(s, axis=-1, keepdims=True))
            alpha = jnp.exp2(m_old - m_new)
            p = jnp.exp2(s - m_new[:, 0:1])
            acc_ref[:, hs] = alpha * acc_ref[:, hs] + _dot(p.astype(BF16), v_ref[0, :, hs])
            m_ref[h] = m_new

    last_needed = (q0 + (i + 1) * bq - 1) // CHUNK * CHUNK + CHUNK - 1
    if diag_only:
        @pl.when(j < i)
        def _():
            block(False)

        @pl.when(j == i)
        def _():
            block(True)
    else:
        @pl.when(j * bk <= last_needed)
        def _():
            block(True)

    @pl.when(j == last_block)
    def _():
        for h in range(N_HEADS):
            a = acc_ref[:, h * HEAD_PAD:(h + 1) * HEAD_PAD]
            o_ref[0, :, h * V_D:(h + 1) * V_D] = a[:, :V_D] / a[:, V_D:V_D + 1]


def _attn(qp, kp, vp, q0, kv_len, bk):
    B, T, _ = qp.shape
    tk = kp.shape[1]
    bq = min(512, T)
    nq, nk = T // bq, tk // bk
    diag_only = q0 == 0 and bq == bk and T == tk and bq % CHUNK == 0
    if diag_only:
        kv_map = lambda b, i, j: (b, jnp.minimum(j, i), 0)
    else:
        kv_map = lambda b, i, j: (b, j, 0)
    return pl.pallas_call(
        functools.partial(_attn_kernel, bq, bk, q0, kv_len, nk - 1, diag_only),
        grid=(B, nq, nk),
        in_specs=[pl.BlockSpec((1, bq, QP), lambda b, i, j: (b, i, 0)),
                  pl.BlockSpec((1, bk, QP), kv_map), pl.BlockSpec((1, bk, QP), kv_map)],
        out_specs=pl.BlockSpec((1, bq, BW), lambda b, i, j: (b, i, 0)),
        out_shape=jax.ShapeDtypeStruct((B, T, BW), F32),
        scratch_shapes=[pltpu.VMEM((N_HEADS, bq, LANES), F32), pltpu.VMEM((bq, QP), F32)],
        compiler_params=_cparams(("arbitrary", "arbitrary", "arbitrary")),
        name="attn",
    )(qp, kp, vp)


def _merge_kernel(final, h_ref, ya_ref, posta_ref, ob_ref, gb_ref, oc_ref, gc_ref, od_ref, gd_ref,
                  par_ref, seg_ref, wg_ref, wb_ref, wo_ref, out_ref):
    par = lambda r, n: par_ref[r:r + 1, 0:n]
    seg = seg_ref[...]
    h = h_ref[0]
    u = _rms(h, par(R_NORM, D_MODEL), NORM_EPS).astype(BF16)
    inv = 1.0 / HEAD_DIM

    ya = ya_ref[...]
    xc = ya - _dot_lhs2(ya, seg) * inv
    yn = xc * lax.rsqrt(_dot_lhs2(xc * xc, seg) * inv + GN_EPS) * par(R_LNW, BW) + par(R_LNB, BW)
    oa = (yn + posta_ref[:, :BW]) * posta_ref[:, BW:]

    def head_rms(o, g):
        return o * lax.rsqrt(_dot_lhs2(o * o, seg) * inv + HEAD_NORM_EPS) * g

    ob = head_rms(ob_ref[...], par(R_HG, BW)) * gb_ref[...]
    oc = oc_ref[0] * gc_ref[0]
    od = head_rms(od_ref[0], par(R_RG, BW)) * gd_ref[0]

    merged = None
    for n, o in enumerate((oa, ob, oc, od)):
        gate = _sigmoid(_dot(u, wg_ref[:, n * D_MODEL:(n + 1) * D_MODEL]))
        term = gate * _dot(o.astype(BF16), wb_ref[n * BW:(n + 1) * BW, :])
        merged = term if merged is None else merged + term
    out = h + _dot(merged.astype(BF16), wo_ref[...])
    if final:
        out = _rms(out, par(R_FG, D_MODEL), NORM_EPS)
    out_ref[0] = out


def _merge(final, h, ya, posta, ob, gb, oc, gc, od, gd, par, seg, wg, wb, wo):
    B, T, _ = h.shape
    tt = min(512, T)
    bspec = lambda c: pl.BlockSpec((1, tt, c), lambda t, b: (b, t, 0))
    tspec = lambda c: pl.BlockSpec((tt, c), lambda t, b: (t, b))
    full = lambda a: pl.BlockSpec(a.shape, lambda t, b: (0,) * a.ndim)
    consts = (par, seg, wg, wb, wo)
    return pl.pallas_call(
        functools.partial(_merge_kernel, final),
        grid=(T // tt, B),
        in_specs=[bspec(D_MODEL), tspec(BW), tspec(2 * BW), tspec(BW), tspec(BW), bspec(BW), bspec(BW), bspec(BW),
                  bspec(BW)] + [full(a) for a in consts],
        out_specs=bspec(D_MODEL),
        out_shape=jax.ShapeDtypeStruct((B, T, D_MODEL), F32),
        compiler_params=_cparams(("arbitrary", "arbitrary")),
        name="merge",
    )(h, ya, posta, ob, gb, oc, gc, od, gd, *consts)


def _rope_tables(past, T):
    pos = (past + jnp.arange(T, dtype=jnp.int32)).astype(F32)[:, None]

    def tables(width, group, start, d):
        lane = np.arange(width) % group - start
        on = (lane >= 0) & (lane < d)
        idx = np.where(on, lane % (d // 2), 0)
        inv_freq = jnp.power(ROPE_BASE, -jnp.arange(0, d, 2, dtype=F32) / d)
        ang = pos * inv_freq[None, :]
        cos = jnp.where(on[None, :], jnp.cos(ang)[:, idx], 1.0)
        sign = np.where(lane < d // 2, -1.0, 1.0).astype(np.float32)
        sin = jnp.where(on[None, :], jnp.sin(ang)[:, idx] * sign[None, :], 0.0)
        return cos, sin

    cd, sd = tables(BW, HEAD_DIM, 0, HEAD_DIM)
    cq, sq = tables(QP, HEAD_PAD, NOPE, ROPE_D)
    ck, sk = tables(LANES, LANES, 0, ROPE_D)
    return jnp.concatenate([cd, sd, cq, sq, ck, sk], axis=1)


def _layer_weights(l, p):
    w_in = p["w_in"][l]
    gate_cols = N_BRANCH * D_MODEL
    c_kr = gate_cols + SHIFT_COLS + BW + 4 * BW + Q_LORA + KV_LORA
    c_g = c_kr + ROPE_D
    w_rest = jnp.concatenate([
        w_in[:, gate_cols:c_kr],
        jnp.pad(w_in[:, c_kr:c_g], ((0, 0), (0, LANES - ROPE_D))),
        w_in[:, c_g:],
    ], axis=1).astype(BF16)
    wg = w_in[:, :gate_cols].astype(BF16)

    z = jnp.zeros((64, BW), F32)
    wl = jnp.concatenate([jnp.concatenate([p["rwkv_w_up"][l], z], axis=1),
                          jnp.concatenate([z, p["rwkv_a_up"][l]], axis=1)], axis=0)
    wl_hi = wl.astype(BF16)
    wl_lo = (wl - wl_hi.astype(F32)).astype(BF16)

    wq = p["mla_w_q_up"][l].reshape(Q_LORA, N_HEADS, QK_D)
    wq = jnp.pad(wq, ((0, 0), (0, 0), (0, HEAD_PAD - QK_D))).reshape(Q_LORA, QP).astype(BF16)
    wkv = p["mla_w_kv_up"][l].reshape(KV_LORA, N_HEADS, NOPE + V_D)
    wk = jnp.pad(wkv[:, :, :NOPE], ((0, 0), (0, 0), (0, HEAD_PAD - NOPE))).reshape(KV_LORA, QP).astype(BF16)
    wv = jnp.pad(wkv[:, :, NOPE:], ((0, 0), (0, 0), (0, HEAD_PAD - V_D))).reshape(KV_LORA, QP).astype(BF16)

    wb = p["w_branch"][l].reshape(N_BRANCH * BW, D_MODEL).astype(BF16)
    wo = p["w_out"][l].astype(BF16)

    def row(v):
        v = v.reshape(1, -1).astype(F32)
        return jnp.pad(v, ((0, 0), (0, D_MODEL - v.shape[1])))

    rows = [row(p["norm_g"][l]), row(p["rwkv_mu"][l]), row(p["rwkv_w0"][l]), row(p["rwkv_a0"][l]),
            row(p["rwkv_k_k"][l]), row(p["rwkv_k_a"][l]), row(p["rwkv_r_k"][l])]
    rows += [row(p["hgrn_lb_logits"][j]) for j in range(DEPTH)]
    rows += [row(p["mla_q_norm_g"][l]), row(p["mla_kv_norm_g"][l]), row(p["rwkv_ln_w"][l]), row(p["rwkv_ln_b"][l]),
             row(p["hgrn_norm_g"][l]), row(p["ret_norm_g"][l]), row(p["final_norm_g"])]
    par = jnp.concatenate(rows + [jnp.zeros((N_ROWS - len(rows), D_MODEL), F32)], axis=0)
    return dict(w_rest=w_rest, wg=wg, wl_hi=wl_hi, wl_lo=wl_lo, wq=wq, wk=wk, wv=wv, wb=wb, wo=wo, par=par)


def _constants():
    head = np.arange(BW) // HEAD_DIM
    seg = jnp.asarray((head[:, None] == head[None, :]).astype(np.float32), dtype=BF16)
    place = np.zeros((LANES, QP), np.float32)
    for h in range(N_HEADS):
        for j in range(ROPE_D):
            place[j, h * HEAD_PAD + NOPE + j] = 1.0
    return seg, jnp.asarray(place, dtype=BF16)


def _state_in(s, B, rep, key_last):
    vr = HEAD_DIM // rep
    if key_last:
        y = s.reshape(B, 2, 2, rep, vr, HEAD_DIM).transpose(5, 4, 3, 2, 0, 1)
    else:
        y = s.reshape(B, 2, 2, HEAD_DIM, rep, vr).transpose(3, 5, 4, 2, 0, 1)
    return y.reshape(HEAD_DIM, vr, LANES).astype(F32)


def _state_out(y, B, rep, key_last):
    vr = HEAD_DIM // rep
    y = y.reshape(HEAD_DIM, vr, rep, 2, B, 2)
    if key_last:
        y = y.transpose(4, 5, 3, 2, 1, 0)
    else:
        y = y.transpose(4, 5, 3, 0, 2, 1)
    return y.reshape(B, N_HEADS, HEAD_DIM, HEAD_DIM)


def _run_trunk(x, shift0, rwkv0, hgrn0, ret0, lat0, kr0, p, weights, consts):
    B, T, _ = x.shape
    past = lat0.shape[2]
    bh = B * N_HEADS
    rep = LANES // bh
    assert rep * bh == LANES and HEAD_DIM % rep == 0 and (HEAD_DIM // rep) % SUBLANES == 0 and N_HEADS == 4
    seg, place = consts
    tab = _rope_tables(past, T)
    h = x
    per_layer = []
    for l in range(DEPTH):
        w = weights[l]
        (ar, aw, ak, av, akk, ab, posta, bq, bf, bv, gb, qp, lat, kr, gc, dq, dk, dv, gd, shn) = _proj(
            l, h, shift0[l][:, None, :], tab, w["par"], w["w_rest"], w["wl_hi"], w["wl_lo"], seg, w["wq"])

        ya, ob, s_rwkv, s_hgrn = _seq([ar, aw, ak, akk, ab], av, [bq, bf], bv, _state_in(rwkv0[l], B, rep, True),
                                      _state_in(hgrn0[l], B, rep, False), bh)
        od, s_ret = _ret(dq, dk, dv, ret0[l].astype(F32))

        lat_all = jnp.concatenate([lat0[l], lat], axis=1)
        kr_all = jnp.concatenate([jnp.pad(kr0[l], ((0, 0), (0, 0), (0, LANES - ROPE_D))), kr], axis=1)
        tk = past + T
        if past == 0:
            blk = min(512, T)
        else:
            blk = tk if tk % SUBLANES == 0 and tk * QP * 2 * 4 <= VMEM_LIMIT // 4 else 256
        tk_pad = -(-tk // blk) * blk
        if tk_pad != tk:
            lat_all = jnp.pad(lat_all, ((0, 0), (0, tk_pad - tk), (0, 0)))
            kr_all = jnp.pad(kr_all, ((0, 0), (0, tk_pad - tk), (0, 0)))
        kp, vp = _kvprep(lat_all, kr_all, w["wk"], w["wv"], place, blk)
        oc = _attn(qp, kp, vp, past, tk, blk)

        h = _merge(l == DEPTH - 1, h, ya, posta, ob, gb, oc, gc, od, gd, w["par"], seg, w["wg"], w["wb"], w["wo"])
        per_layer.append((shn[-1, :, 0, :], _state_out(s_rwkv, B, rep, True), _state_out(s_hgrn, B, rep, False),
                          s_ret, lat, kr[:, :, :ROPE_D]))
    stacked = [jnp.stack([st[j] for st in per_layer]) for j in range(6)]
    return h, stacked


def kernel(x_prompt, x_sample, state_rwkv_shift, state_rwkv, state_hgrn, cache_mla_latent, cache_mla_krope, state_ret, norm_g, w_in, rwkv_mu, rwkv_w0, rwkv_w_up, rwkv_a0, rwkv_a_up, rwkv_k_k, rwkv_k_a, rwkv_r_k, rwkv_ln_w, rwkv_ln_b, hgrn_lb_logits, hgrn_norm_g, mla_q_norm_g, mla_w_q_up, mla_kv_norm_g, mla_w_kv_up, ret_norm_g, w_branch, w_out, final_norm_g):
    p = dict(norm_g=norm_g, w_in=w_in, rwkv_mu=rwkv_mu, rwkv_w0=rwkv_w0, rwkv_w_up=rwkv_w_up, rwkv_a0=rwkv_a0,
             rwkv_a_up=rwkv_a_up, rwkv_k_k=rwkv_k_k, rwkv_k_a=rwkv_k_a, rwkv_r_k=rwkv_r_k, rwkv_ln_w=rwkv_ln_w,
             rwkv_ln_b=rwkv_ln_b, hgrn_lb_logits=hgrn_lb_logits, hgrn_norm_g=hgrn_norm_g, mla_q_norm_g=mla_q_norm_g,
             mla_w_q_up=mla_w_q_up, mla_kv_norm_g=mla_kv_norm_g, mla_w_kv_up=mla_w_kv_up, ret_norm_g=ret_norm_g,
             w_branch=w_branch, w_out=w_out, final_norm_g=final_norm_g)
    weights = [_layer_weights(l, p) for l in range(DEPTH)]
    consts = _constants()
    bp, dt = x_prompt.shape[0], x_prompt.dtype
    zs = jnp.zeros((DEPTH, bp, N_HEADS, HEAD_DIM, HEAD_DIM), dt)
    y_p, (p_shift, p_rwkv, p_hgrn, p_ret, p_lat, p_kr) = _run_trunk(
        x_prompt, jnp.zeros((DEPTH, bp, SHIFT_COLS), dt), zs, zs, zs, jnp.zeros((DEPTH, bp, 0, KV_LORA), dt),
        jnp.zeros((DEPTH, bp, 0, ROPE_D), dt), p, weights, consts)
    y_s, (s_shift, s_rwkv, s_hgrn, s_ret, s_lat, s_kr) = _run_trunk(
        x_sample, state_rwkv_shift, state_rwkv, state_hgrn, state_ret, cache_mla_latent, cache_mla_krope,
        p, weights, consts)
    return (y_p, y_s, p_shift, s_shift, p_rwkv, s_rwkv, p_hgrn, s_hgrn, p_lat, s_lat, p_kr, s_kr, p_ret, s_ret)
```

```python
import functools

import jax
import jax.numpy as jnp
import numpy as np
from jax import lax
from jax.experimental import pallas as pl
from jax.experimental.pallas import tpu as pltpu

F32 = jnp.float32
BF16 = jnp.bfloat16

D_MODEL = 1024
DEPTH = 4
N_HEADS = 4
HEAD_DIM = 64
BW = N_HEADS * HEAD_DIM
N_BRANCH = 4
CHUNK = 64
SHIFT_COLS = 3 * BW + 64 + 64
GN_EPS = 64e-5
NORM_EPS = 1e-6
HEAD_NORM_EPS = 1e-5
ROPE_BASE = 10000.0
Q_LORA, KV_LORA, NOPE, ROPE_D, V_D = 256, 128, 64, 32, 64
QK_D = NOPE + ROPE_D
HEAD_PAD = 128
Q_SCALE = QK_D ** -0.5 * 1.4426950408889634
QP = N_HEADS * HEAD_PAD
LANES = 128
SUBLANES = 8

A0, A1 = 0, SHIFT_COLS + BW
B0, B1 = A1, A1 + 4 * BW
C0, C1 = B1, B1 + Q_LORA + KV_LORA + LANES + BW
D0, D1 = C1, C1 + 4 * BW
PROJ_COLS = D1

T_CD, T_SD, T_CQ, T_SQ, T_CK, T_SK, TAB_COLS = 0, 256, 512, 1024, 1536, 1664, 1792

(R_NORM, R_MU, R_W0, R_A0, R_KK, R_KA, R_RK, R_LB, R_QG, R_KVG,
 R_LNW, R_LNB, R_HG, R_RG, R_FG, N_ROWS) = (0, 1, 2, 3, 4, 5, 6, 7, 11, 12, 13, 14, 15, 16, 17, 24)

VMEM_LIMIT = 50 * 1024 * 1024


def _cparams(sem):
    return pltpu.CompilerParams(dimension_semantics=sem, vmem_limit_bytes=VMEM_LIMIT)


def _sigmoid(x):
    return 1.0 / (1.0 + jnp.exp(-x))


def _silu(x):
    return x * _sigmoid(x)


def _softplus(x):
    return jnp.maximum(x, 0.0) + jnp.log(1.0 + jnp.exp(-jnp.abs(x)))


def _dot(a, b):
    return jnp.dot(a, b, preferred_element_type=F32)


def _split(x):
    hi = x.astype(BF16)
    lo = (x - hi.astype(F32)).astype(BF16)
    return hi, lo


def _dot_lhs2(x, m_bf16):
    hi, lo = _split(x)
    return _dot(hi, m_bf16) + _dot(lo, m_bf16)


def _dot3(x, m_hi, m_lo):
    hi, lo = _split(x)
    return _dot(hi, m_hi) + _dot(lo, m_hi) + _dot(hi, m_lo)


def _rms(x, g, eps):
    return x * lax.rsqrt(jnp.mean(x * x, axis=-1, keepdims=True) + eps) * g


def _rope(x, cos, sin_signed, half, group, start=0):
    w = x.shape[-1]
    lane = lax.broadcasted_iota(jnp.int32, (1, w), 1) % group
    up = pltpu.roll(x, w - half, axis=1)
    dn = pltpu.roll(x, half, axis=1)
    return x * cos + jnp.where(lane < start + half, up, dn) * sin_signed


def _proj_kernel(layer, tt, h_ref, sh0_ref, tab_ref, par_ref, w_ref, wl_hi_ref, wl_lo_ref, seg_ref, wq_ref,
                 ar_ref, aw_ref, ak_ref, av_ref, akk_ref, ab_ref, posta_ref,
                 bq_ref, bf_ref, bv_ref, gb_ref,
                 qp_ref, lat_ref, kr_ref, gc_ref,
                 dq_ref, dk_ref, dv_ref, gd_ref, shn_ref, carry_ref):
    t = pl.program_id(0)
    b = pl.program_id(1)
    par = lambda r, n: par_ref[r:r + 1, 0:n]
    seg = seg_ref[...]

    u = _rms(h_ref[0], par(R_NORM, D_MODEL), NORM_EPS).astype(BF16)

    pa_all = _dot(u, w_ref[:, A0:A1])
    pa = pa_all[:, :SHIFT_COLS]
    prev_row = jnp.where(t == 0, sh0_ref[b], carry_ref[b])
    row = lax.broadcasted_iota(jnp.int32, (tt, 1), 0)
    p_prev = jnp.where(row == 0, prev_row, pltpu.roll(pa, 1, axis=0))
    last = pa[tt - 1:tt, :]
    carry_ref[b] = last
    shn_ref[0, 0] = last
    xs = pa + (p_prev - pa) * par(R_MU, SHIFT_COLS)
    r, k, v, wdad = xs[:, :BW], xs[:, BW:2 * BW], xs[:, 2 * BW:3 * BW], xs[:, 3 * BW:]
    lane = lax.broadcasted_iota(jnp.int32, (1, LANES), 1)
    lora = _dot3(jnp.where(lane < 64, jnp.tanh(wdad), wdad), wl_hi_ref[...], wl_lo_ref[...])
    w_raw = -_softplus(-(par(R_W0, BW) + lora[:, :BW])) - 0.5
    a = _sigmoid(par(R_A0, BW) + lora[:, BW:])
    kk = k * par(R_KK, BW)
    kk = kk / jnp.maximum(jnp.sqrt(_dot_lhs2(kk * kk, seg)), 1e-12)
    k2 = k * (1.0 + (a - 1.0) * par(R_KA, BW))
    ar_ref[...] = r
    aw_ref[...] = jnp.exp(-jnp.exp(w_raw))
    ak_ref[...] = k2
    av_ref[...] = v
    akk_ref[...] = kk
    ab_ref[...] = kk * a
    posta_ref[:, :BW] = _dot_lhs2(r * k2 * par(R_RK, BW), seg) * v
    posta_ref[:, BW:] = _silu(pa_all[:, SHIFT_COLS:])

    pb = _dot(u, w_ref[:, B0:B1])
    lg = par_ref[R_LB:R_LB + DEPTH, 0:BW]
    e = jnp.exp(lg - jnp.max(lg, axis=0, keepdims=True))
    lb = jnp.zeros((1, BW), F32)
    for j in range(1, layer + 1):
        lb = lb + e[j:j + 1]
    lb = lb / jnp.sum(e, axis=0, keepdims=True)
    z = pb[:, BW:2 * BW]
    bq_ref[...] = _silu(pb[:, :BW]) * (HEAD_DIM ** -0.5)
    bf_ref[...] = lb + (1.0 - lb) * _sigmoid(z)
    bv_ref[...] = pb[:, 2 * BW:3 * BW]
    gb_ref[...] = _silu(pb[:, 3 * BW:])

    pc = _dot(u, w_ref[:, C0:C1])
    qn = _rms(pc[:, :Q_LORA], par(R_QG, Q_LORA), NORM_EPS).astype(BF16)
    q = _dot(qn, wq_ref[...]) * Q_SCALE
    qp_ref[0] = _rope(q, tab_ref[:, T_CQ:T_CQ + QP], tab_ref[:, T_SQ:T_SQ + QP], ROPE_D // 2, HEAD_PAD,
                      NOPE).astype(BF16)
    lat_ref[0] = _rms(pc[:, Q_LORA:Q_LORA + KV_LORA], par(R_KVG, KV_LORA), NORM_EPS)
    kr_ref[0] = _rope(pc[:, Q_LORA + KV_LORA:Q_LORA + KV_LORA + LANES], tab_ref[:, T_CK:T_CK + LANES],
                      tab_ref[:, T_SK:T_SK + LANES], ROPE_D // 2, LANES)
    gc_ref[0] = _silu(pc[:, Q_LORA + KV_LORA + LANES:])

    pd = _dot(u, w_ref[:, D0:D1])
    cos, sin = tab_ref[:, T_CD:T_CD + BW], tab_ref[:, T_SD:T_SD + BW]
    dq_ref[0] = _rope(pd[:, :BW], cos, sin, HEAD_DIM // 2, HEAD_DIM)
    dk_ref[0] = _rope(pd[:, BW:2 * BW], cos, sin, HEAD_DIM // 2, HEAD_DIM) * (HEAD_DIM ** -0.5)
    dv_ref[0] = pd[:, 2 * BW:3 * BW]
    gd_ref[0] = _silu(pd[:, 3 * BW:])


def _proj(layer, h, shift0, tab, par, w_rest, wl_hi, wl_lo, seg, wq):
    B, T, _ = h.shape
    tt = min(256, T)
    nt = T // tt
    bspec = lambda c: pl.BlockSpec((1, tt, c), lambda t, b: (b, t, 0))
    tspec = lambda c: pl.BlockSpec((tt, c), lambda t, b: (t, b))
    full = lambda a: pl.BlockSpec(a.shape, lambda t, b: (0,) * a.ndim)
    bm = lambda c, dt=F32: jax.ShapeDtypeStruct((B, T, c), dt)
    tm = lambda c: jax.ShapeDtypeStruct((T, B * c), F32)
    outs = ([(tm(BW), tspec(BW))] * 6 + [(tm(2 * BW), tspec(2 * BW))]
            + [(tm(BW), tspec(BW))] * 4
            + [(bm(QP, BF16), bspec(QP)), (bm(KV_LORA), bspec(KV_LORA)), (bm(LANES), bspec(LANES)),
               (bm(BW), bspec(BW))]
            + [(bm(BW), bspec(BW))] * 4
            + [(jax.ShapeDtypeStruct((nt, B, 1, SHIFT_COLS), F32),
                pl.BlockSpec((1, 1, 1, SHIFT_COLS), lambda t, b: (t, b, 0, 0)))])
    return pl.pallas_call(
        functools.partial(_proj_kernel, layer, tt),
        grid=(nt, B),
        in_specs=[bspec(D_MODEL), full(shift0), pl.BlockSpec((tt, TAB_COLS), lambda t, b: (t, 0)), full(par),
                  full(w_rest), full(wl_hi), full(wl_lo), full(seg), full(wq)],
        out_specs=tuple(o[1] for o in outs),
        out_shape=tuple(o[0] for o in outs),
        scratch_shapes=[pltpu.VMEM((B, 1, SHIFT_COLS), F32)],
        compiler_params=_cparams(("arbitrary", "arbitrary")),
        name="proj",
    )(h, shift0, tab, par, w_rest, wl_hi, wl_lo, seg, wq)


N_ACC = 2


N_COL = 8
LOOKAHEAD = 2
N_SLOTS = 4


def _seq_kernel(tc, bh, r_ref, w_ref, k_ref, kk_ref, b_ref, va_ref, q_ref, f_ref, vb_ref, sa0_ref, sb0_ref,
                ya_ref, yb_ref, sa_out_ref, sb_out_ref, st_a, st_b, *ring):
    slots, vslots, yslots = ring[0:N_SLOTS], ring[N_SLOTS:2 * N_SLOTS], ring[2 * N_SLOTS:3 * N_SLOTS]
    g_sz = bh // 2
    rep = LANES // bh
    vr = HEAD_DIM // rep
    n_pairs = tc // 2
    lane = lax.broadcasted_iota(jnp.int32, (1, LANES), 1)
    low = lane < HEAD_DIM
    hi_id = lane // bh
    grid_step = pl.program_id(0)

    @pl.when(grid_step == 0)
    def _():
        st_a[...] = sa0_ref[...]
        st_b[...] = sb0_ref[...]

    def paired(ref, p):
        a, b = ref[2 * p], ref[2 * p + 1]
        s0 = jnp.where(low, a, pltpu.roll(b, HEAD_DIM, axis=1))
        s1 = jnp.where(low, pltpu.roll(a, HEAD_DIM, axis=1), b)
        return jnp.concatenate([s0, s1] * rep, axis=0).T

    def prepare(p, slot, vslot):
        for o, ref in enumerate((r_ref, w_ref, k_ref, kk_ref, b_ref, q_ref, f_ref)):
            xt = paired(ref, p)
            slot[o] = xt
            if ref is f_ref:
                slot[o + 1] = 1.0 - xt
        for ri, ref in enumerate((va_ref, vb_ref)):
            xt = paired(ref, p)
            for s in range(2):
                vv = xt[s * HEAD_DIM:s * HEAD_DIM + vr]
                for g in range(1, rep):
                    vv = jnp.where(hi_id == g, xt[s * HEAD_DIM + g * vr:s * HEAD_DIM + (g + 1) * vr], vv)
                vslot[ri, s] = vv

    def emit(p, yslot):
        for ri, y_ref in enumerate((ya_ref, yb_ref)):
            blocks = [jnp.where(hi_id == g, yslot[ri, s], 0.0) for s in range(2) for g in range(rep)]
            zt = jnp.concatenate(blocks, axis=0).T
            yp = zt[0:bh]
            for g in range(1, rep):
                yp = yp + zt[g * bh:(g + 1) * bh]
            p0, p1 = yp[:g_sz], yp[g_sz:]
            y_ref[2 * p] = jnp.where(low, p0, pltpu.roll(p1, HEAD_DIM, axis=1))
            y_ref[2 * p + 1] = jnp.where(low, pltpu.roll(p0, HEAD_DIM, axis=1), p1)

    def tree(xs):
        while len(xs) > 1:
            xs = [xs[i] + xs[i + 1] for i in range(0, len(xs), 2)]
        return xs[0]

    def bc(slot, o, row):
        return jnp.broadcast_to(slot[o, pl.ds(row, 1), :], (vr, LANES))

    def accumulate(acc, k, p):
        acc[k % N_ACC] = p if acc[k % N_ACC] is None else acc[k % N_ACC] + p

    def first_sa(slot):
        acc = [None] * N_ACC
        for k in range(HEAD_DIM):
            accumulate(acc, k, st_a[k] * bc(slot, 3, k))
        return -tree(acc)

    def rwkv_step(s, slot, kk_slot, kk_row, vslot, yslot, sa):
        vv = vslot[0, s]
        acc_y, acc_s = [None] * N_ACC, [None] * N_ACC
        for k in range(HEAD_DIM):
            row = s * HEAD_DIM + k
            new = st_a[k] * bc(slot, 1, row) + sa * bc(slot, 4, row) + vv * bc(slot, 2, row)
            st_a[k] = new
            accumulate(acc_y, k, new * bc(slot, 0, row))
            accumulate(acc_s, k, new * bc(kk_slot, 3, kk_row + k))
        yslot[0, s] = tree(acc_y)
        return -tree(acc_s)

    def hgrn_pair(slot, vslot, yslot):
        v0, v1 = vslot[1, 0], vslot[1, 1]
        acc0, acc1 = [None] * N_ACC, [None] * N_ACC
        for k in range(HEAD_DIM):
            mid = st_b[k] * bc(slot, 6, k) + v0 * bc(slot, 7, k)
            accumulate(acc0, k, mid * bc(slot, 5, k))
            new = mid * bc(slot, 6, HEAD_DIM + k) + v1 * bc(slot, 7, HEAD_DIM + k)
            st_b[k] = new
            accumulate(acc1, k, new * bc(slot, 5, HEAD_DIM + k))
        yslot[1, 0] = tree(acc0)
        yslot[1, 1] = tree(acc1)

    def pair_of_steps(p, j, sa):
        emit(jnp.maximum(p - 1, 0), yslots[(j - 1) % N_SLOTS])
        nxt = (j + LOOKAHEAD) % N_SLOTS
        prepare(jnp.minimum(p + LOOKAHEAD, n_pairs - 1), slots[nxt], vslots[nxt])
        sa = rwkv_step(0, slots[j], slots[j], HEAD_DIM, vslots[j], yslots[j], sa)
        sa = rwkv_step(1, slots[j], slots[(j + 1) % N_SLOTS], 0, vslots[j], yslots[j], sa)
        hgrn_pair(slots[j], vslots[j], yslots[j])
        return sa

    for j in range(LOOKAHEAD):
        prepare(j, slots[j], vslots[j])
    yslots[N_SLOTS - 1][...] = jnp.zeros(yslots[N_SLOTS - 1].shape, F32)

    def body(i, sa):
        for j in range(N_SLOTS):
            sa = pair_of_steps(N_SLOTS * i + j, j, sa)
        return sa

    lax.fori_loop(0, n_pairs // N_SLOTS, body, first_sa(slots[0]))
    emit(n_pairs - 1, yslots[N_SLOTS - 1])

    @pl.when(grid_step == pl.num_programs(0) - 1)
    def _():
        sa_out_ref[...] = st_a[...]
        sb_out_ref[...] = st_b[...]


def _seq(cols_a, va, cols_b, vb, sa0, sb0, bh):
    T = va.shape[0]
    g_sz = bh // 2
    vr = sa0.shape[1]
    tc = min(128, T)
    assert tc % (2 * N_SLOTS) == 0
    view = lambda a: a.reshape(T, g_sz, LANES)
    ospec = pl.BlockSpec((tc, g_sz, LANES), lambda g: (g, 0, 0))
    sspec = pl.BlockSpec((HEAD_DIM, vr, LANES), lambda g: (0, 0, 0))
    args = [view(c) for c in (*cols_a, va, *cols_b, vb)] + [sa0, sb0]
    y_shape = jax.ShapeDtypeStruct((T, g_sz, LANES), F32)
    s_shape = jax.ShapeDtypeStruct((HEAD_DIM, vr, LANES), F32)
    slot = pltpu.VMEM((N_COL, 2 * HEAD_DIM, LANES), F32)
    small = pltpu.VMEM((2, 2, vr, LANES), F32)
    ya, yb, sa, sb = pl.pallas_call(
        functools.partial(_seq_kernel, tc, bh),
        grid=(T // tc,),
        in_specs=[ospec] * (len(args) - 2) + [sspec, sspec],
        out_specs=(ospec, ospec, sspec, sspec),
        out_shape=(y_shape, y_shape, s_shape, s_shape),
        scratch_shapes=[pltpu.VMEM((HEAD_DIM, vr, LANES), F32), pltpu.VMEM((HEAD_DIM, vr, LANES), F32),
                        *([slot] * N_SLOTS + [small] * (2 * N_SLOTS))],
        compiler_params=_cparams(("arbitrary",)),
        name="seq",
    )(*args)
    return ya.reshape(T, bh * HEAD_DIM), yb.reshape(T, bh * HEAD_DIM), sa, sb


def _ret_kernel(L, q_ref, k_ref, v_ref, s0_ref, o_ref, s_out_ref, st, dm, gq, gk):
    i = pl.program_id(1)
    log2_gamma = [float(np.log2(1.0 - 2.0 ** (-5.0 - h))) for h in range(N_HEADS)]

    @pl.when(i == 0)
    def _():
        st[...] = s0_ref[0]
        t_idx = lax.broadcasted_iota(jnp.int32, (L, L), 0)
        s_idx = lax.broadcasted_iota(jnp.int32, (L, L), 1)
        row = lax.broadcasted_iota(jnp.int32, (L, HEAD_DIM), 0).astype(F32)
        for h in range(N_HEADS):
            d = (t_idx - s_idx).astype(F32)
            dm[h] = jnp.where(t_idx >= s_idx, jnp.exp2(d * log2_gamma[h]), 0.0)
            gq[h] = jnp.exp2((row + 1.0) * log2_gamma[h])
            gk[h] = jnp.exp2((L - 1.0 - row) * log2_gamma[h])

    for h in range(N_HEADS):
        hs = slice(h * HEAD_DIM, (h + 1) * HEAD_DIM)
        q, k, v = q_ref[0, :, hs], k_ref[0, :, hs], v_ref[0, :, hs].astype(BF16)
        a = lax.dot_general(q.astype(BF16), k.astype(BF16), (((1,), (1,)), ((), ())),
                            preferred_element_type=F32) * dm[h]
        s_prev = st[h]
        o_ref[0, :, hs] = _dot(a.astype(BF16), v) + _dot((q * gq[h]).astype(BF16), s_prev.astype(BF16))
        kd = (k * gk[h]).astype(BF16)
        st[h] = s_prev * (2.0 ** (L * log2_gamma[h])) + lax.dot_general(
            kd, v, (((0,), (0,)), ((), ())), preferred_element_type=F32)

    @pl.when(i == pl.num_programs(1) - 1)
    def _():
        s_out_ref[0] = st[...]


def _ret(q, k, v, s0):
    B, T, _ = q.shape
    L = min(512, T)
    tspec = pl.BlockSpec((1, L, BW), lambda b, i: (b, i, 0))
    sspec = pl.BlockSpec((1, N_HEADS, HEAD_DIM, HEAD_DIM), lambda b, i: (b, 0, 0, 0))
    return pl.pallas_call(
        functools.partial(_ret_kernel, L),
        grid=(B, T // L),
        in_specs=[tspec, tspec, tspec, sspec],
        out_specs=(tspec, sspec),
        out_shape=(jax.ShapeDtypeStruct((B, T, BW), F32),
                   jax.ShapeDtypeStruct((B, N_HEADS, HEAD_DIM, HEAD_DIM), F32)),
        scratch_shapes=[pltpu.VMEM((N_HEADS, HEAD_DIM, HEAD_DIM), F32), pltpu.VMEM((N_HEADS, L, L), F32),
                        pltpu.VMEM((N_HEADS, L, HEAD_DIM), F32), pltpu.VMEM((N_HEADS, L, HEAD_DIM), F32)],
        compiler_params=_cparams(("arbitrary", "arbitrary")),
        name="ret",
    )(q, k, v, s0)


def _kvprep_kernel(lat_ref, kr_ref, wkt_ref, wv_ref, placet_ref, kt_ref, v_ref):
    lat = lat_ref[0].astype(BF16)
    nt = (((1,), (1,)), ((), ()))
    kt_ref[0] = (lax.dot_general(wkt_ref[...], lat, nt, preferred_element_type=F32)
                 + lax.dot_general(placet_ref[...], kr_ref[0].astype(BF16), nt, preferred_element_type=F32)).astype(BF16)
    lane = lax.broadcasted_iota(jnp.int32, (1, QP), 1) % HEAD_PAD
    v_ref[0] = jnp.where(lane == V_D, 1.0, _dot(lat, wv_ref[...])).astype(BF16)


def _kvprep(lat_all, kr_all, wkt, wv, placet, tr):
    B, tk, _ = lat_all.shape
    rspec = lambda c: pl.BlockSpec((1, tr, c), lambda b, i: (b, i, 0))
    full = lambda a: pl.BlockSpec(a.shape, lambda b, i: (0,) * a.ndim)
    return pl.pallas_call(
        _kvprep_kernel,
        grid=(B, tk // tr),
        in_specs=[rspec(KV_LORA), rspec(LANES), full(wkt), full(wv), full(placet)],
        out_specs=(pl.BlockSpec((1, QP, tr), lambda b, i: (b, 0, i)), rspec(QP)),
        out_shape=(jax.ShapeDtypeStruct((B, QP, tk), BF16), jax.ShapeDtypeStruct((B, tk, QP), BF16)),
        compiler_params=_cparams(("arbitrary", "arbitrary")),
        name="kvprep",
    )(lat_all, kr_all, wkt, wv, placet)


def _attn_kernel(bq, bk, q0, kv_len, last_block, diag_only, q_ref, kt_ref, v_ref, o_ref, m_ref, acc_ref):
    i = pl.program_id(1)
    j = pl.program_id(2)

    @pl.when(j == 0)
    def _():
        m_ref[...] = jnp.full(m_ref.shape, -jnp.inf, F32)
        acc_ref[...] = jnp.zeros(acc_ref.shape, F32)

    def block(masked):
        if masked:
            qi = q0 + i * bq + lax.broadcasted_iota(jnp.int32, (bq, bk), 0)
            ki = j * bk + lax.broadcasted_iota(jnp.int32, (bq, bk), 1)
            keep = jnp.logical_and(ki // CHUNK <= qi // CHUNK, ki < kv_len)
        for h in range(N_HEADS):
            hs = slice(h * HEAD_PAD, (h + 1) * HEAD_PAD)
            s = _dot(q_ref[0, :, hs], kt_ref[0, hs, :])
            if masked:
                s = jnp.where(keep, s, -jnp.inf)
            m_old = m_ref[h]
            m_new = jnp.maximum(m_old, jnp.max(s, axis=-1, keepdims=True))
            alpha = jnp.exp2(m_old - m_new)
            m_full = jnp.concatenate([m_new] * (bk // LANES), axis=1) if bk % LANES == 0 else m_new[:, 0:1]
            p = jnp.exp2(s - m_full)
            acc_ref[:, hs] = alpha * acc_ref[:, hs] + _dot(p.astype(BF16), v_ref[0, :, hs])
            m_ref[h] = m_new

    last_needed = (q0 + (i + 1) * bq - 1) // CHUNK * CHUNK + CHUNK - 1
    if diag_only:
        @pl.when(j < i)
        def _():
            block(False)

        @pl.when(j == i)
        def _():
            block(True)
    else:
        @pl.when(j * bk <= last_needed)
        def _():
            block(True)

    @pl.when(j == last_block)
    def _():
        for h in range(N_HEADS):
            a = acc_ref[:, h * HEAD_PAD:(h + 1) * HEAD_PAD]
            o_ref[0, :, h * V_D:(h + 1) * V_D] = a[:, :V_D] / a[:, V_D:V_D + 1]


def _attn(qp, ktp, vp, q0, kv_len, bk):
    B, T, _ = qp.shape
    tk = vp.shape[1]
    bq = min(512, T)
    nq, nk = T // bq, tk // bk
    diag_only = q0 == 0 and bq == bk and T == tk and bq % CHUNK == 0
    if diag_only:
        kv_blk = lambda i, j: jnp.minimum(j, i)
    else:
        kv_blk = lambda i, j: j
    return pl.pallas_call(
        functools.partial(_attn_kernel, bq, bk, q0, kv_len, nk - 1, diag_only),
        grid=(B, nq, nk),
        in_specs=[pl.BlockSpec((1, bq, QP), lambda b, i, j: (b, i, 0)),
                  pl.BlockSpec((1, QP, bk), lambda b, i, j: (b, 0, kv_blk(i, j))),
                  pl.BlockSpec((1, bk, QP), lambda b, i, j: (b, kv_blk(i, j), 0))],
        out_specs=pl.BlockSpec((1, bq, BW), lambda b, i, j: (b, i, 0)),
        out_shape=jax.ShapeDtypeStruct((B, T, BW), F32),
        scratch_shapes=[pltpu.VMEM((N_HEADS, bq, LANES), F32), pltpu.VMEM((bq, QP), F32)],
        compiler_params=_cparams(("arbitrary", "arbitrary", "arbitrary")),
        name="attn",
    )(qp, ktp, vp)


def _merge_kernel(final, h_ref, ya_ref, posta_ref, ob_ref, gb_ref, oc_ref, gc_ref, od_ref, gd_ref,
                  par_ref, seg_ref, wg_ref, wb_ref, wo_ref, out_ref):
    par = lambda r, n: par_ref[r:r + 1, 0:n]
    seg = seg_ref[...]
    h = h_ref[0]
    u = _rms(h, par(R_NORM, D_MODEL), NORM_EPS).astype(BF16)
    inv = 1.0 / HEAD_DIM

    ya = ya_ref[...]
    xc = ya - _dot_lhs2(ya, seg) * inv
    yn = xc * lax.rsqrt(_dot_lhs2(xc * xc, seg) * inv + GN_EPS) * par(R_LNW, BW) + par(R_LNB, BW)
    oa = (yn + posta_ref[:, :BW]) * posta_ref[:, BW:]

    def head_rms(o, g):
        return o * lax.rsqrt(_dot_lhs2(o * o, seg) * inv + HEAD_NORM_EPS) * g

    ob = head_rms(ob_ref[...], par(R_HG, BW)) * gb_ref[...]
    oc = oc_ref[0] * gc_ref[0]
    od = head_rms(od_ref[0], par(R_RG, BW)) * gd_ref[0]

    merged = None
    for n, o in enumerate((oa, ob, oc, od)):
        gate = _sigmoid(_dot(u, wg_ref[:, n * D_MODEL:(n + 1) * D_MODEL]))
        term = gate * _dot(o.astype(BF16), wb_ref[n * BW:(n + 1) * BW, :])
        merged = term if merged is None else merged + term
    out = h + _dot(merged.astype(BF16), wo_ref[...])
    if final:
        out = _rms(out, par(R_FG, D_MODEL), NORM_EPS)
    out_ref[0] = out


def _merge(final, h, ya, posta, ob, gb, oc, gc, od, gd, par, seg, wg, wb, wo):
    B, T, _ = h.shape
    tt = min(512, T)
    bspec = lambda c: pl.BlockSpec((1, tt, c), lambda t, b: (b, t, 0))
    tspec = lambda c: pl.BlockSpec((tt, c), lambda t, b: (t, b))
    full = lambda a: pl.BlockSpec(a.shape, lambda t, b: (0,) * a.ndim)
    consts = (par, seg, wg, wb, wo)
    return pl.pallas_call(
        functools.partial(_merge_kernel, final),
        grid=(T // tt, B),
        in_specs=[bspec(D_MODEL), tspec(BW), tspec(2 * BW), tspec(BW), tspec(BW), bspec(BW), bspec(BW), bspec(BW),
                  bspec(BW)] + [full(a) for a in consts],
        out_specs=bspec(D_MODEL),
        out_shape=jax.ShapeDtypeStruct((B, T, D_MODEL), F32),
        compiler_params=_cparams(("arbitrary", "arbitrary")),
        name="merge",
    )(h, ya, posta, ob, gb, oc, gc, od, gd, *consts)


def _rope_tables(past, T):
    pos = (past + jnp.arange(T, dtype=jnp.int32)).astype(F32)[:, None]

    def tables(width, group, start, d):
        lane = np.arange(width) % group - start
        on = (lane >= 0) & (lane < d)
        idx = np.where(on, lane % (d // 2), 0)
        inv_freq = jnp.power(ROPE_BASE, -jnp.arange(0, d, 2, dtype=F32) / d)
        ang = pos * inv_freq[None, :]
        cos = jnp.where(on[None, :], jnp.cos(ang)[:, idx], 1.0)
        sign = np.where(lane < d // 2, -1.0, 1.0).astype(np.float32)
        sin = jnp.where(on[None, :], jnp.sin(ang)[:, idx] * sign[None, :], 0.0)
        return cos, sin

    cd, sd = tables(BW, HEAD_DIM, 0, HEAD_DIM)
    cq, sq = tables(QP, HEAD_PAD, NOPE, ROPE_D)
    ck, sk = tables(LANES, LANES, 0, ROPE_D)
    return jnp.concatenate([cd, sd, cq, sq, ck, sk], axis=1)


def _layer_weights(l, p):
    w_in = p["w_in"][l]
    gate_cols = N_BRANCH * D_MODEL
    c_kr = gate_cols + SHIFT_COLS + BW + 4 * BW + Q_LORA + KV_LORA
    c_g = c_kr + ROPE_D
    w_rest = jnp.concatenate([
        w_in[:, gate_cols:c_kr],
        jnp.pad(w_in[:, c_kr:c_g], ((0, 0), (0, LANES - ROPE_D))),
        w_in[:, c_g:],
    ], axis=1).astype(BF16)
    wg = w_in[:, :gate_cols].astype(BF16)

    z = jnp.zeros((64, BW), F32)
    wl = jnp.concatenate([jnp.concatenate([p["rwkv_w_up"][l], z], axis=1),
                          jnp.concatenate([z, p["rwkv_a_up"][l]], axis=1)], axis=0)
    wl_hi = wl.astype(BF16)
    wl_lo = (wl - wl_hi.astype(F32)).astype(BF16)

    wq = p["mla_w_q_up"][l].reshape(Q_LORA, N_HEADS, QK_D)
    wq = jnp.pad(wq, ((0, 0), (0, 0), (0, HEAD_PAD - QK_D))).reshape(Q_LORA, QP).astype(BF16)
    wkv = p["mla_w_kv_up"][l].reshape(KV_LORA, N_HEADS, NOPE + V_D)
    wkt = jnp.pad(wkv[:, :, :NOPE], ((0, 0), (0, 0), (0, HEAD_PAD - NOPE))).reshape(KV_LORA, QP).T.astype(BF16)
    wv = jnp.pad(wkv[:, :, NOPE:], ((0, 0), (0, 0), (0, HEAD_PAD - V_D))).reshape(KV_LORA, QP).astype(BF16)

    wb = p["w_branch"][l].reshape(N_BRANCH * BW, D_MODEL).astype(BF16)
    wo = p["w_out"][l].astype(BF16)

    def row(v):
        v = v.reshape(1, -1).astype(F32)
        return jnp.pad(v, ((0, 0), (0, D_MODEL - v.shape[1])))

    rows = [row(p["norm_g"][l]), row(p["rwkv_mu"][l]), row(p["rwkv_w0"][l]), row(p["rwkv_a0"][l]),
            row(p["rwkv_k_k"][l]), row(p["rwkv_k_a"][l]), row(p["rwkv_r_k"][l])]
    rows += [row(p["hgrn_lb_logits"][j]) for j in range(DEPTH)]
    rows += [row(p["mla_q_norm_g"][l]), row(p["mla_kv_norm_g"][l]), row(p["rwkv_ln_w"][l]), row(p["rwkv_ln_b"][l]),
             row(p["hgrn_norm_g"][l]), row(p["ret_norm_g"][l]), row(p["final_norm_g"])]
    par = jnp.concatenate(rows + [jnp.zeros((N_ROWS - len(rows), D_MODEL), F32)], axis=0)
    return dict(w_rest=w_rest, wg=wg, wl_hi=wl_hi, wl_lo=wl_lo, wq=wq, wkt=wkt, wv=wv, wb=wb, wo=wo, par=par)


def _constants():
    head = np.arange(BW) // HEAD_DIM
    seg = jnp.asarray((head[:, None] == head[None, :]).astype(np.float32), dtype=BF16)
    place = np.zeros((LANES, QP), np.float32)
    for h in range(N_HEADS):
        for j in range(ROPE_D):
            place[j, h * HEAD_PAD + NOPE + j] = 1.0
    return seg, jnp.asarray(place.T, dtype=BF16)


def _state_in(s, B, rep, key_last):
    vr = HEAD_DIM // rep
    if key_last:
        y = s.reshape(B, 2, 2, rep, vr, HEAD_DIM).transpose(5, 4, 3, 2, 0, 1)
    else:
        y = s.reshape(B, 2, 2, HEAD_DIM, rep, vr).transpose(3, 5, 4, 2, 0, 1)
    return y.reshape(HEAD_DIM, vr, LANES).astype(F32)


def _state_out(y, B, rep, key_last):
    vr = HEAD_DIM // rep
    y = y.reshape(HEAD_DIM, vr, rep, 2, B, 2)
    if key_last:
        y = y.transpose(4, 5, 3, 2, 1, 0)
    else:
        y = y.transpose(4, 5, 3, 0, 2, 1)
    return y.reshape(B, N_HEADS, HEAD_DIM, HEAD_DIM)


def _run_trunk(x, shift0, rwkv0, hgrn0, ret0, lat0, kr0, p, weights, consts):
    B, T, _ = x.shape
    past = lat0.shape[2]
    bh = B * N_HEADS
    rep = LANES // bh
    assert rep * bh == LANES and HEAD_DIM % rep == 0 and (HEAD_DIM // rep) % SUBLANES == 0 and N_HEADS == 4
    seg, placet = consts
    tab = _rope_tables(past, T)
    h = x
    per_layer = []
    for l in range(DEPTH):
        w = weights[l]
        (ar, aw, ak, av, akk, ab, posta, bq, bf, bv, gb, qp, lat, kr, gc, dq, dk, dv, gd, shn) = _proj(
            l, h, shift0[l][:, None, :], tab, w["par"], w["w_rest"], w["wl_hi"], w["wl_lo"], seg, w["wq"])

        ya, ob, s_rwkv, s_hgrn = _seq([ar, aw, ak, akk, ab], av, [bq, bf], bv, _state_in(rwkv0[l], B, rep, True),
                                      _state_in(hgrn0[l], B, rep, False), bh)
        od, s_ret = _ret(dq, dk, dv, ret0[l].astype(F32))

        lat_all = jnp.concatenate([lat0[l], lat], axis=1)
        kr_all = jnp.concatenate([jnp.pad(kr0[l], ((0, 0), (0, 0), (0, LANES - ROPE_D))), kr], axis=1)
        tk = past + T
        if past == 0:
            blk = min(512, T)
        else:
            whole = -(-tk // LANES) * LANES
            blk = whole if whole * QP * 2 * 4 <= VMEM_LIMIT // 4 else 256
        tk_pad = -(-tk // blk) * blk
        if tk_pad != tk:
            lat_all = jnp.pad(lat_all, ((0, 0), (0, tk_pad - tk), (0, 0)))
            kr_all = jnp.pad(kr_all, ((0, 0), (0, tk_pad - tk), (0, 0)))
        ktp, vp = _kvprep(lat_all, kr_all, w["wkt"], w["wv"], placet, blk)
        oc = _attn(qp, ktp, vp, past, tk, blk)

        h = _merge(l == DEPTH - 1, h, ya, posta, ob, gb, oc, gc, od, gd, w["par"], seg, w["wg"], w["wb"], w["wo"])
        per_layer.append((shn[-1, :, 0, :], _state_out(s_rwkv, B, rep, True), _state_out(s_hgrn, B, rep, False),
                          s_ret, lat, kr[:, :, :ROPE_D]))
    stacked = [jnp.stack([st[j] for st in per_layer]) for j in range(6)]
    return h, stacked


def kernel(x_prompt, x_sample, state_rwkv_shift, state_rwkv, state_hgrn, cache_mla_latent, cache_mla_krope, state_ret, norm_g, w_in, rwkv_mu, rwkv_w0, rwkv_w_up, rwkv_a0, rwkv_a_up, rwkv_k_k, rwkv_k_a, rwkv_r_k, rwkv_ln_w, rwkv_ln_b, hgrn_lb_logits, hgrn_norm_g, mla_q_norm_g, mla_w_q_up, mla_kv_norm_g, mla_w_kv_up, ret_norm_g, w_branch, w_out, final_norm_g):
    p = dict(norm_g=norm_g, w_in=w_in, rwkv_mu=rwkv_mu, rwkv_w0=rwkv_w0, rwkv_w_up=rwkv_w_up, rwkv_a0=rwkv_a0,
             rwkv_a_up=rwkv_a_up, rwkv_k_k=rwkv_k_k, rwkv_k_a=rwkv_k_a, rwkv_r_k=rwkv_r_k, rwkv_ln_w=rwkv_ln_w,
             rwkv_ln_b=rwkv_ln_b, hgrn_lb_logits=hgrn_lb_logits, hgrn_norm_g=hgrn_norm_g, mla_q_norm_g=mla_q_norm_g,
             mla_w_q_up=mla_w_q_up, mla_kv_norm_g=mla_kv_norm_g, mla_w_kv_up=mla_w_kv_up, ret_norm_g=ret_norm_g,
             w_branch=w_branch, w_out=w_out, final_norm_g=final_norm_g)
    weights = [_layer_weights(l, p) for l in range(DEPTH)]
    consts = _constants()
    bp, dt = x_prompt.shape[0], x_prompt.dtype
    zs = jnp.zeros((DEPTH, bp, N_HEADS, HEAD_DIM, HEAD_DIM), dt)
    y_p, (p_shift, p_rwkv, p_hgrn, p_ret, p_lat, p_kr) = _run_trunk(
        x_prompt, jnp.zeros((DEPTH, bp, SHIFT_COLS), dt), zs, zs, zs, jnp.zeros((DEPTH, bp, 0, KV_LORA), dt),
        jnp.zeros((DEPTH, bp, 0, ROPE_D), dt), p, weights, consts)
    y_s, (s_shift, s_rwkv, s_hgrn, s_ret, s_lat, s_kr) = _run_trunk(
        x_sample, state_rwkv_shift, state_rwkv, state_hgrn, state_ret, cache_mla_latent, cache_mla_krope,
        p, weights, consts)
    return (y_p, y_s, p_shift, s_shift, p_rwkv, s_rwkv, p_hgrn, s_hgrn, p_lat, s_lat, p_kr, s_kr, p_ret, s_ret)
```

```python
import functools

import jax
import jax.numpy as jnp
import numpy as np
from jax import lax
from jax.experimental import pallas as pl
from jax.experimental.pallas import tpu as pltpu

F32 = jnp.float32
BF16 = jnp.bfloat16

D_MODEL = 1024
DEPTH = 4
N_HEADS = 4
HEAD_DIM = 64
BW = N_HEADS * HEAD_DIM
N_BRANCH = 4
CHUNK = 64
SHIFT_COLS = 3 * BW + 64 + 64
GN_EPS = 64e-5
NORM_EPS = 1e-6
HEAD_NORM_EPS = 1e-5
ROPE_BASE = 10000.0
Q_LORA, KV_LORA, NOPE, ROPE_D, V_D = 256, 128, 64, 32, 64
QK_D = NOPE + ROPE_D
HEAD_PAD = 128
Q_SCALE = QK_D ** -0.5 * 1.4426950408889634
QP = N_HEADS * HEAD_PAD
LANES = 128
SUBLANES = 8

A0, A1 = 0, SHIFT_COLS + BW
B0, B1 = A1, A1 + 4 * BW
C0, C1 = B1, B1 + Q_LORA + KV_LORA + LANES + BW
D0, D1 = C1, C1 + 4 * BW
PROJ_COLS = D1

T_CD, T_SD, T_CQ, T_SQ, T_CK, T_SK, TAB_COLS = 0, 256, 512, 1024, 1536, 1664, 1792

(R_NORM, R_MU, R_W0, R_A0, R_KK, R_KA, R_RK, R_LB, R_QG, R_KVG,
 R_LNW, R_LNB, R_HG, R_RG, R_FG, N_ROWS) = (0, 1, 2, 3, 4, 5, 6, 7, 11, 12, 13, 14, 15, 16, 17, 24)

VMEM_LIMIT = 56 * 1024 * 1024


def _cparams(sem):
    return pltpu.CompilerParams(dimension_semantics=sem, vmem_limit_bytes=VMEM_LIMIT)


def _sigmoid(x):
    return 1.0 / (1.0 + jnp.exp(-x))


def _silu(x):
    return x * _sigmoid(x)


def _softplus(x):
    return jnp.maximum(x, 0.0) + jnp.log(1.0 + jnp.exp(-jnp.abs(x)))


def _dot(a, b):
    return jnp.dot(a, b, preferred_element_type=F32)


def _split(x):
    hi = x.astype(BF16)
    lo = (x - hi.astype(F32)).astype(BF16)
    return hi, lo


def _dot_lhs2(x, m_bf16):
    hi, lo = _split(x)
    return _dot(hi, m_bf16) + _dot(lo, m_bf16)


def _dot3(x, m_hi, m_lo):
    hi, lo = _split(x)
    return _dot(hi, m_hi) + _dot(lo, m_hi) + _dot(hi, m_lo)


def _rms(x, g, eps):
    return x * lax.rsqrt(jnp.mean(x * x, axis=-1, keepdims=True) + eps) * g


def _rope(x, cos, sin_signed, half, group, start=0):
    w = x.shape[-1]
    lane = lax.broadcasted_iota(jnp.int32, (1, w), 1) % group
    up = pltpu.roll(x, w - half, axis=1)
    dn = pltpu.roll(x, half, axis=1)
    return x * cos + jnp.where(lane < start + half, up, dn) * sin_signed


def _proj_kernel(layer, tt, h_ref, sh0_ref, tab_ref, par_ref, w_ref, wl_hi_ref, wl_lo_ref, seg_ref, wq_ref,
                 ar_ref, aw_ref, ak_ref, av_ref, akk_ref, ab_ref, posta_ref,
                 bq_ref, bf_ref, bv_ref, gb_ref,
                 qp_ref, lat_ref, kr_ref, gc_ref,
                 dq_ref, dk_ref, dv_ref, gd_ref, shn_ref, carry_ref):
    t = pl.program_id(0)
    b = pl.program_id(1)
    par = lambda r, n: par_ref[r:r + 1, 0:n]
    seg = seg_ref[...]

    u = _rms(h_ref[0], par(R_NORM, D_MODEL), NORM_EPS).astype(BF16)

    pa_all = _dot(u, w_ref[:, A0:A1])
    pa = pa_all[:, :SHIFT_COLS]
    prev_row = jnp.where(t == 0, sh0_ref[b], carry_ref[b])
    row = lax.broadcasted_iota(jnp.int32, (tt, 1), 0)
    p_prev = jnp.where(row == 0, prev_row, pltpu.roll(pa, 1, axis=0))
    last = pa[tt - 1:tt, :]
    carry_ref[b] = last
    shn_ref[0, 0] = last
    xs = pa + (p_prev - pa) * par(R_MU, SHIFT_COLS)
    r, k, v, wdad = xs[:, :BW], xs[:, BW:2 * BW], xs[:, 2 * BW:3 * BW], xs[:, 3 * BW:]
    lane = lax.broadcasted_iota(jnp.int32, (1, LANES), 1)
    lora = _dot3(jnp.where(lane < 64, jnp.tanh(wdad), wdad), wl_hi_ref[...], wl_lo_ref[...])
    w_raw = -_softplus(-(par(R_W0, BW) + lora[:, :BW])) - 0.5
    a = _sigmoid(par(R_A0, BW) + lora[:, BW:])
    kk = k * par(R_KK, BW)
    kk = kk / jnp.maximum(jnp.sqrt(_dot_lhs2(kk * kk, seg)), 1e-12)
    k2 = k * (1.0 + (a - 1.0) * par(R_KA, BW))
    ar_ref[...] = r
    aw_ref[...] = jnp.exp(-jnp.exp(w_raw))
    ak_ref[...] = k2
    av_ref[...] = v
    akk_ref[...] = kk
    ab_ref[...] = kk * a
    posta_ref[:, :BW] = _dot_lhs2(r * k2 * par(R_RK, BW), seg) * v
    posta_ref[:, BW:] = _silu(pa_all[:, SHIFT_COLS:])

    pb = _dot(u, w_ref[:, B0:B1])
    lg = par_ref[R_LB:R_LB + DEPTH, 0:BW]
    e = jnp.exp(lg - jnp.max(lg, axis=0, keepdims=True))
    lb = jnp.zeros((1, BW), F32)
    for j in range(1, layer + 1):
        lb = lb + e[j:j + 1]
    lb = lb / jnp.sum(e, axis=0, keepdims=True)
    z = pb[:, BW:2 * BW]
    bq_ref[...] = _silu(pb[:, :BW]) * (HEAD_DIM ** -0.5)
    bf_ref[...] = lb + (1.0 - lb) * _sigmoid(z)
    bv_ref[...] = pb[:, 2 * BW:3 * BW]
    gb_ref[...] = _silu(pb[:, 3 * BW:])

    pc = _dot(u, w_ref[:, C0:C1])
    qn = _rms(pc[:, :Q_LORA], par(R_QG, Q_LORA), NORM_EPS).astype(BF16)
    q = _dot(qn, wq_ref[...]) * Q_SCALE
    qp_ref[0] = _rope(q, tab_ref[:, T_CQ:T_CQ + QP], tab_ref[:, T_SQ:T_SQ + QP], ROPE_D // 2, HEAD_PAD,
                      NOPE).astype(BF16)
    lat_ref[0] = _rms(pc[:, Q_LORA:Q_LORA + KV_LORA], par(R_KVG, KV_LORA), NORM_EPS)
    kr_ref[0] = _rope(pc[:, Q_LORA + KV_LORA:Q_LORA + KV_LORA + LANES], tab_ref[:, T_CK:T_CK + LANES],
                      tab_ref[:, T_SK:T_SK + LANES], ROPE_D // 2, LANES)
    gc_ref[0] = _silu(pc[:, Q_LORA + KV_LORA + LANES:])

    pd = _dot(u, w_ref[:, D0:D1])
    cos, sin = tab_ref[:, T_CD:T_CD + BW], tab_ref[:, T_SD:T_SD + BW]
    dq_ref[0] = _rope(pd[:, :BW], cos, sin, HEAD_DIM // 2, HEAD_DIM)
    dk_ref[0] = _rope(pd[:, BW:2 * BW], cos, sin, HEAD_DIM // 2, HEAD_DIM) * (HEAD_DIM ** -0.5)
    dv_ref[0] = pd[:, 2 * BW:3 * BW]
    gd_ref[0] = _silu(pd[:, 3 * BW:])


def _proj(layer, h, shift0, tab, par, w_rest, wl_hi, wl_lo, seg, wq):
    B, T, _ = h.shape
    tt = min(512, T)
    nt = T // tt
    bspec = lambda c: pl.BlockSpec((1, tt, c), lambda t, b: (b, t, 0))
    tspec = lambda c: pl.BlockSpec((tt, c), lambda t, b: (t, b))
    full = lambda a: pl.BlockSpec(a.shape, lambda t, b: (0,) * a.ndim)
    bm = lambda c, dt=F32: jax.ShapeDtypeStruct((B, T, c), dt)
    tm = lambda c: jax.ShapeDtypeStruct((T, B * c), F32)
    outs = ([(tm(BW), tspec(BW))] * 6 + [(tm(2 * BW), tspec(2 * BW))]
            + [(tm(BW), tspec(BW))] * 4
            + [(bm(QP, BF16), bspec(QP)), (bm(KV_LORA), bspec(KV_LORA)), (bm(LANES), bspec(LANES)),
               (bm(BW), bspec(BW))]
            + [(bm(BW), bspec(BW))] * 4
            + [(jax.ShapeDtypeStruct((nt, B, 1, SHIFT_COLS), F32),
                pl.BlockSpec((1, 1, 1, SHIFT_COLS), lambda t, b: (t, b, 0, 0)))])
    return pl.pallas_call(
        functools.partial(_proj_kernel, layer, tt),
        grid=(nt, B),
        in_specs=[bspec(D_MODEL), full(shift0), pl.BlockSpec((tt, TAB_COLS), lambda t, b: (t, 0)), full(par),
                  full(w_rest), full(wl_hi), full(wl_lo), full(seg), full(wq)],
        out_specs=tuple(o[1] for o in outs),
        out_shape=tuple(o[0] for o in outs),
        scratch_shapes=[pltpu.VMEM((B, 1, SHIFT_COLS), F32)],
        compiler_params=_cparams(("arbitrary", "arbitrary")),
        name="proj",
    )(h, shift0, tab, par, w_rest, wl_hi, wl_lo, seg, wq)


N_ACC = 1


N_COL = 8
LOOKAHEAD = 2
N_SLOTS = 4


def _seq_kernel(tc, bh, r_ref, w_ref, k_ref, kk_ref, b_ref, va_ref, q_ref, f_ref, vb_ref, sa0_ref, sb0_ref,
                ya_ref, yb_ref, sa_out_ref, sb_out_ref, st_a, st_b, *ring):
    slots, vslots, yslots = ring[0:N_SLOTS], ring[N_SLOTS:2 * N_SLOTS], ring[2 * N_SLOTS:3 * N_SLOTS]
    g_sz = bh // 2
    rep = LANES // bh
    vr = HEAD_DIM // rep
    n_pairs = tc // 2
    lane = lax.broadcasted_iota(jnp.int32, (1, LANES), 1)
    low = lane < HEAD_DIM
    hi_id = lane // bh
    grid_step = pl.program_id(0)

    @pl.when(grid_step == 0)
    def _():
        st_a[...] = sa0_ref[...]
        st_b[...] = sb0_ref[...]

    def paired(ref, p):
        a, b = ref[2 * p], ref[2 * p + 1]
        s0 = jnp.where(low, a, pltpu.roll(b, HEAD_DIM, axis=1))
        s1 = jnp.where(low, pltpu.roll(a, HEAD_DIM, axis=1), b)
        return jnp.concatenate([s0, s1] * rep, axis=0).T

    def prepare(p, slot, vslot):
        for o, ref in enumerate((r_ref, w_ref, k_ref, kk_ref, b_ref, q_ref, f_ref)):
            xt = paired(ref, p)
            slot[o] = xt
            if ref is f_ref:
                slot[o + 1] = 1.0 - xt
        for ri, ref in enumerate((va_ref, vb_ref)):
            xt = paired(ref, p)
            for s in range(2):
                vv = xt[s * HEAD_DIM:s * HEAD_DIM + vr]
                for g in range(1, rep):
                    vv = jnp.where(hi_id == g, xt[s * HEAD_DIM + g * vr:s * HEAD_DIM + (g + 1) * vr], vv)
                vslot[ri, s] = vv

    def emit(p, yslot):
        for ri, y_ref in enumerate((ya_ref, yb_ref)):
            blocks = [jnp.where(hi_id == g, yslot[ri, s], 0.0) for s in range(2) for g in range(rep)]
            zt = jnp.concatenate(blocks, axis=0).T
            yp = zt[0:bh]
            for g in range(1, rep):
                yp = yp + zt[g * bh:(g + 1) * bh]
            p0, p1 = yp[:g_sz], yp[g_sz:]
            y_ref[2 * p] = jnp.where(low, p0, pltpu.roll(p1, HEAD_DIM, axis=1))
            y_ref[2 * p + 1] = jnp.where(low, pltpu.roll(p0, HEAD_DIM, axis=1), p1)

    def tree(xs):
        while len(xs) > 1:
            xs = [xs[i] + xs[i + 1] for i in range(0, len(xs), 2)]
        return xs[0]

    def bc(slot, o, row):
        return jnp.broadcast_to(slot[o, pl.ds(row, 1), :], (vr, LANES))

    def accumulate(acc, k, p):
        acc[k % N_ACC] = p if acc[k % N_ACC] is None else acc[k % N_ACC] + p

    def first_sa(slot):
        acc = [None] * N_ACC
        for k in range(HEAD_DIM):
            accumulate(acc, k, st_a[k] * bc(slot, 3, k))
        return -tree(acc)

    def rwkv_step(s, slot, kk_slot, kk_row, vslot, yslot, sa):
        vv = vslot[0, s]
        acc_y, acc_s = [None] * N_ACC, [None] * N_ACC
        for k in range(HEAD_DIM):
            row = s * HEAD_DIM + k
            new = st_a[k] * bc(slot, 1, row) + sa * bc(slot, 4, row) + vv * bc(slot, 2, row)
            st_a[k] = new
            accumulate(acc_y, k, new * bc(slot, 0, row))
            accumulate(acc_s, k, new * bc(kk_slot, 3, kk_row + k))
        yslot[0, s] = tree(acc_y)
        return -tree(acc_s)

    def hgrn_pair(slot, vslot, yslot):
        v0, v1 = vslot[1, 0], vslot[1, 1]
        acc0, acc1 = [None] * N_ACC, [None] * N_ACC
        for k in range(HEAD_DIM):
            mid = st_b[k] * bc(slot, 6, k) + v0 * bc(slot, 7, k)
            accumulate(acc0, k, mid * bc(slot, 5, k))
            new = mid * bc(slot, 6, HEAD_DIM + k) + v1 * bc(slot, 7, HEAD_DIM + k)
            st_b[k] = new
            accumulate(acc1, k, new * bc(slot, 5, HEAD_DIM + k))
        yslot[1, 0] = tree(acc0)
        yslot[1, 1] = tree(acc1)

    def pair_of_steps(p, j, sa):
        emit(jnp.maximum(p - 1, 0), yslots[(j - 1) % N_SLOTS])
        nxt = (j + LOOKAHEAD) % N_SLOTS
        prepare(jnp.minimum(p + LOOKAHEAD, n_pairs - 1), slots[nxt], vslots[nxt])
        sa = rwkv_step(0, slots[j], slots[j], HEAD_DIM, vslots[j], yslots[j], sa)
        sa = rwkv_step(1, slots[j], slots[(j + 1) % N_SLOTS], 0, vslots[j], yslots[j], sa)
        hgrn_pair(slots[j], vslots[j], yslots[j])
        return sa

    for j in range(LOOKAHEAD):
        prepare(j, slots[j], vslots[j])
    yslots[N_SLOTS - 1][...] = jnp.zeros(yslots[N_SLOTS - 1].shape, F32)

    def body(i, sa):
        for j in range(N_SLOTS):
            sa = pair_of_steps(N_SLOTS * i + j, j, sa)
        return sa

    lax.fori_loop(0, n_pairs // N_SLOTS, body, first_sa(slots[0]))
    emit(n_pairs - 1, yslots[N_SLOTS - 1])

    @pl.when(grid_step == pl.num_programs(0) - 1)
    def _():
        sa_out_ref[...] = st_a[...]
        sb_out_ref[...] = st_b[...]


def _seq(cols_a, va, cols_b, vb, sa0, sb0, bh):
    T = va.shape[0]
    g_sz = bh // 2
    vr = sa0.shape[1]
    tc = min(128, T)
    assert tc % (2 * N_SLOTS) == 0
    view = lambda a: a.reshape(T, g_sz, LANES)
    ospec = pl.BlockSpec((tc, g_sz, LANES), lambda g: (g, 0, 0))
    sspec = pl.BlockSpec((HEAD_DIM, vr, LANES), lambda g: (0, 0, 0))
    args = [view(c) for c in (*cols_a, va, *cols_b, vb)] + [sa0, sb0]
    y_shape = jax.ShapeDtypeStruct((T, g_sz, LANES), F32)
    s_shape = jax.ShapeDtypeStruct((HEAD_DIM, vr, LANES), F32)
    slot = pltpu.VMEM((N_COL, 2 * HEAD_DIM, LANES), F32)
    small = pltpu.VMEM((2, 2, vr, LANES), F32)
    ya, yb, sa, sb = pl.pallas_call(
        functools.partial(_seq_kernel, tc, bh),
        grid=(T // tc,),
        in_specs=[ospec] * (len(args) - 2) + [sspec, sspec],
        out_specs=(ospec, ospec, sspec, sspec),
        out_shape=(y_shape, y_shape, s_shape, s_shape),
        scratch_shapes=[pltpu.VMEM((HEAD_DIM, vr, LANES), F32), pltpu.VMEM((HEAD_DIM, vr, LANES), F32),
                        *([slot] * N_SLOTS + [small] * (2 * N_SLOTS))],
        compiler_params=_cparams(("arbitrary",)),
        name="seq",
    )(*args)
    return ya.reshape(T, bh * HEAD_DIM), yb.reshape(T, bh * HEAD_DIM), sa, sb


def _ret_kernel(L, q_ref, k_ref, v_ref, s0_ref, o_ref, s_out_ref, st, dm, gq, gk):
    i = pl.program_id(1)
    log2_gamma = [float(np.log2(1.0 - 2.0 ** (-5.0 - h))) for h in range(N_HEADS)]

    @pl.when(i == 0)
    def _():
        st[...] = s0_ref[0]
        t_idx = lax.broadcasted_iota(jnp.int32, (L, L), 0)
        s_idx = lax.broadcasted_iota(jnp.int32, (L, L), 1)
        row = lax.broadcasted_iota(jnp.int32, (L, HEAD_DIM), 0).astype(F32)
        for h in range(N_HEADS):
            d = (t_idx - s_idx).astype(F32)
            dm[h] = jnp.where(t_idx >= s_idx, jnp.exp2(d * log2_gamma[h]), 0.0)
            gq[h] = jnp.exp2((row + 1.0) * log2_gamma[h])
            gk[h] = jnp.exp2((L - 1.0 - row) * log2_gamma[h])

    for h in range(N_HEADS):
        hs = slice(h * HEAD_DIM, (h + 1) * HEAD_DIM)
        q, k, v = q_ref[0, :, hs], k_ref[0, :, hs], v_ref[0, :, hs].astype(BF16)
        a = lax.dot_general(q.astype(BF16), k.astype(BF16), (((1,), (1,)), ((), ())),
                            preferred_element_type=F32) * dm[h]
        s_prev = st[h]
        o_ref[0, :, hs] = _dot(a.astype(BF16), v) + _dot((q * gq[h]).astype(BF16), s_prev.astype(BF16))
        kd = (k * gk[h]).astype(BF16)
        st[h] = s_prev * (2.0 ** (L * log2_gamma[h])) + lax.dot_general(
            kd, v, (((0,), (0,)), ((), ())), preferred_element_type=F32)

    @pl.when(i == pl.num_programs(1) - 1)
    def _():
        s_out_ref[0] = st[...]


def _ret(q, k, v, s0):
    B, T, _ = q.shape
    L = min(512, T)
    tspec = pl.BlockSpec((1, L, BW), lambda b, i: (b, i, 0))
    sspec = pl.BlockSpec((1, N_HEADS, HEAD_DIM, HEAD_DIM), lambda b, i: (b, 0, 0, 0))
    return pl.pallas_call(
        functools.partial(_ret_kernel, L),
        grid=(B, T // L),
        in_specs=[tspec, tspec, tspec, sspec],
        out_specs=(tspec, sspec),
        out_shape=(jax.ShapeDtypeStruct((B, T, BW), F32),
                   jax.ShapeDtypeStruct((B, N_HEADS, HEAD_DIM, HEAD_DIM), F32)),
        scratch_shapes=[pltpu.VMEM((N_HEADS, HEAD_DIM, HEAD_DIM), F32), pltpu.VMEM((N_HEADS, L, L), F32),
                        pltpu.VMEM((N_HEADS, L, HEAD_DIM), F32), pltpu.VMEM((N_HEADS, L, HEAD_DIM), F32)],
        compiler_params=_cparams(("arbitrary", "arbitrary")),
        name="ret",
    )(q, k, v, s0)


def _kvprep_kernel(lat_ref, kr_ref, wkt_ref, wv_ref, placet_ref, kt_ref, v_ref):
    lat = lat_ref[0].astype(BF16)
    nt = (((1,), (1,)), ((), ()))
    kt_ref[0] = (lax.dot_general(wkt_ref[...], lat, nt, preferred_element_type=F32)
                 + lax.dot_general(placet_ref[...], kr_ref[0].astype(BF16), nt, preferred_element_type=F32)).astype(BF16)
    lane = lax.broadcasted_iota(jnp.int32, (1, QP), 1) % HEAD_PAD
    v_ref[0] = jnp.where(lane == V_D, 1.0, _dot(lat, wv_ref[...])).astype(BF16)


def _kvprep(lat_all, kr_all, wkt, wv, placet, tr):
    B, tk, _ = lat_all.shape
    rspec = lambda c: pl.BlockSpec((1, tr, c), lambda b, i: (b, i, 0))
    full = lambda a: pl.BlockSpec(a.shape, lambda b, i: (0,) * a.ndim)
    return pl.pallas_call(
        _kvprep_kernel,
        grid=(B, tk // tr),
        in_specs=[rspec(KV_LORA), rspec(LANES), full(wkt), full(wv), full(placet)],
        out_specs=(pl.BlockSpec((1, QP, tr), lambda b, i: (b, 0, i)), rspec(QP)),
        out_shape=(jax.ShapeDtypeStruct((B, QP, tk), BF16), jax.ShapeDtypeStruct((B, tk, QP), BF16)),
        compiler_params=_cparams(("arbitrary", "arbitrary")),
        name="kvprep",
    )(lat_all, kr_all, wkt, wv, placet)


A_FIRST, A_LAST, A_MASKED = 1, 2, 4


def _attn_kernel(bq, bk, q0, kv_len, qi_ref, kj_ref, flag_ref, q_ref, kt_ref, v_ref, o_ref, m_ref, acc_ref):
    step = pl.program_id(1)
    i = qi_ref[step]
    j = kj_ref[step]
    flags = flag_ref[step]

    @pl.when((flags & A_FIRST) != 0)
    def _():
        m_ref[...] = jnp.full(m_ref.shape, -jnp.inf, F32)
        acc_ref[...] = jnp.zeros(acc_ref.shape, F32)

    def block(masked):
        if masked:
            qi = q0 + i * bq + lax.broadcasted_iota(jnp.int32, (bq, bk), 0)
            ki = j * bk + lax.broadcasted_iota(jnp.int32, (bq, bk), 1)
            keep = jnp.logical_and(ki // CHUNK <= qi // CHUNK, ki < kv_len)
        for h in range(N_HEADS):
            hs = slice(h * HEAD_PAD, (h + 1) * HEAD_PAD)
            s = _dot(q_ref[0, :, hs], kt_ref[0, hs, :])
            if masked:
                s = jnp.where(keep, s, -jnp.inf)
            m_old = m_ref[h]
            m_new = jnp.maximum(m_old, jnp.max(s, axis=-1, keepdims=True))
            alpha = jnp.exp2(m_old - m_new)
            m_full = jnp.concatenate([m_new] * (bk // LANES), axis=1) if bk % LANES == 0 else m_new[:, 0:1]
            p = jnp.exp2(s - m_full)
            acc_ref[:, hs] = alpha * acc_ref[:, hs] + _dot(p.astype(BF16), v_ref[0, :, hs])
            m_ref[h] = m_new

    @pl.when((flags & A_MASKED) == 0)
    def _():
        block(False)

    @pl.when((flags & A_MASKED) != 0)
    def _():
        block(True)

    @pl.when((flags & A_LAST) != 0)
    def _():
        for h in range(N_HEADS):
            a = acc_ref[:, h * HEAD_PAD:(h + 1) * HEAD_PAD]
            o_ref[0, :, h * V_D:(h + 1) * V_D] = a[:, :V_D] / a[:, V_D:V_D + 1]


def _attn(qp, ktp, vp, q0, kv_len, bk):
    B, T, _ = qp.shape
    tk = vp.shape[1]
    bq = min(512, T)
    nq, nk = T // bq, tk // bk
    qi, kj, flags = [], [], []
    for i in range(nq):
        first_q, last_q = q0 + i * bq, q0 + (i + 1) * bq - 1
        n_blocks = min(nk, (min(kv_len, (last_q // CHUNK + 1) * CHUNK) - 1) // bk + 1)
        for j in range(n_blocks):
            fully_visible = (j + 1) * bk <= min(kv_len, (first_q // CHUNK + 1) * CHUNK)
            qi.append(i)
            kj.append(j)
            flags.append((A_FIRST if j == 0 else 0) | (A_LAST if j == n_blocks - 1 else 0)
                         | (0 if fully_visible else A_MASKED))
    sched = [jnp.asarray(np.asarray(a, np.int32)) for a in (qi, kj, flags)]
    grid_spec = pltpu.PrefetchScalarGridSpec(
        num_scalar_prefetch=3,
        grid=(B, len(qi)),
        in_specs=[pl.BlockSpec((1, bq, QP), lambda b, s, qi, kj, fl: (b, qi[s], 0)),
                  pl.BlockSpec((1, QP, bk), lambda b, s, qi, kj, fl: (b, 0, kj[s])),
                  pl.BlockSpec((1, bk, QP), lambda b, s, qi, kj, fl: (b, kj[s], 0))],
        out_specs=pl.BlockSpec((1, bq, BW), lambda b, s, qi, kj, fl: (b, qi[s], 0)),
        scratch_shapes=[pltpu.VMEM((N_HEADS, bq, LANES), F32), pltpu.VMEM((bq, QP), F32)])
    return pl.pallas_call(
        functools.partial(_attn_kernel, bq, bk, q0, kv_len),
        grid_spec=grid_spec,
        out_shape=jax.ShapeDtypeStruct((B, T, BW), F32),
        compiler_params=_cparams(("arbitrary", "arbitrary")),
        name="attn",
    )(*sched, qp, ktp, vp)


def _merge_kernel(final, h_ref, ya_ref, posta_ref, ob_ref, gb_ref, oc_ref, gc_ref, od_ref, gd_ref,
                  par_ref, seg_ref, wg_ref, wb_ref, wo_ref, out_ref):
    par = lambda r, n: par_ref[r:r + 1, 0:n]
    seg = seg_ref[...]
    h = h_ref[0]
    u = _rms(h, par(R_NORM, D_MODEL), NORM_EPS).astype(BF16)
    inv = 1.0 / HEAD_DIM

    ya = ya_ref[...]
    xc = ya - _dot_lhs2(ya, seg) * inv
    yn = xc * lax.rsqrt(_dot_lhs2(xc * xc, seg) * inv + GN_EPS) * par(R_LNW, BW) + par(R_LNB, BW)
    oa = (yn + posta_ref[:, :BW]) * posta_ref[:, BW:]

    def head_rms(o, g):
        return o * lax.rsqrt(_dot_lhs2(o * o, seg) * inv + HEAD_NORM_EPS) * g

    ob = head_rms(ob_ref[...], par(R_HG, BW)) * gb_ref[...]
    oc = oc_ref[0] * gc_ref[0]
    od = head_rms(od_ref[0], par(R_RG, BW)) * gd_ref[0]

    merged = None
    for n, o in enumerate((oa, ob, oc, od)):
        gate = _sigmoid(_dot(u, wg_ref[:, n * D_MODEL:(n + 1) * D_MODEL]))
        term = gate * _dot(o.astype(BF16), wb_ref[n * BW:(n + 1) * BW, :])
        merged = term if merged is None else merged + term
    out = h + _dot(merged.astype(BF16), wo_ref[...])
    if final:
        out = _rms(out, par(R_FG, D_MODEL), NORM_EPS)
    out_ref[0] = out


def _merge(final, h, ya, posta, ob, gb, oc, gc, od, gd, par, seg, wg, wb, wo):
    B, T, _ = h.shape
    tt = min(512, T)
    bspec = lambda c: pl.BlockSpec((1, tt, c), lambda t, b: (b, t, 0))
    tspec = lambda c: pl.BlockSpec((tt, c), lambda t, b: (t, b))
    full = lambda a: pl.BlockSpec(a.shape, lambda t, b: (0,) * a.ndim)
    consts = (par, seg, wg, wb, wo)
    return pl.pallas_call(
        functools.partial(_merge_kernel, final),
        grid=(T // tt, B),
        in_specs=[bspec(D_MODEL), tspec(BW), tspec(2 * BW), tspec(BW), tspec(BW), bspec(BW), bspec(BW), bspec(BW),
                  bspec(BW)] + [full(a) for a in consts],
        out_specs=bspec(D_MODEL),
        out_shape=jax.ShapeDtypeStruct((B, T, D_MODEL), F32),
        compiler_params=_cparams(("arbitrary", "arbitrary")),
        name="merge",
    )(h, ya, posta, ob, gb, oc, gc, od, gd, *consts)


def _rope_tables(past, T):
    pos = (past + jnp.arange(T, dtype=jnp.int32)).astype(F32)[:, None]

    def tables(width, group, start, d):
        lane = np.arange(width) % group - start
        on = (lane >= 0) & (lane < d)
        idx = np.where(on, lane % (d // 2), 0)
        inv_freq = jnp.power(ROPE_BASE, -jnp.arange(0, d, 2, dtype=F32) / d)
        ang = pos * inv_freq[None, :]
        cos = jnp.where(on[None, :], jnp.cos(ang)[:, idx], 1.0)
        sign = np.where(lane < d // 2, -1.0, 1.0).astype(np.float32)
        sin = jnp.where(on[None, :], jnp.sin(ang)[:, idx] * sign[None, :], 0.0)
        return cos, sin

    cd, sd = tables(BW, HEAD_DIM, 0, HEAD_DIM)
    cq, sq = tables(QP, HEAD_PAD, NOPE, ROPE_D)
    ck, sk = tables(LANES, LANES, 0, ROPE_D)
    return jnp.concatenate([cd, sd, cq, sq, ck, sk], axis=1)


def _layer_weights(l, p):
    w_in = p["w_in"][l]
    gate_cols = N_BRANCH * D_MODEL
    c_kr = gate_cols + SHIFT_COLS + BW + 4 * BW + Q_LORA + KV_LORA
    c_g = c_kr + ROPE_D
    w_rest = jnp.concatenate([
        w_in[:, gate_cols:c_kr],
        jnp.pad(w_in[:, c_kr:c_g], ((0, 0), (0, LANES - ROPE_D))),
        w_in[:, c_g:],
    ], axis=1).astype(BF16)
    wg = w_in[:, :gate_cols].astype(BF16)

    z = jnp.zeros((64, BW), F32)
    wl = jnp.concatenate([jnp.concatenate([p["rwkv_w_up"][l], z], axis=1),
                          jnp.concatenate([z, p["rwkv_a_up"][l]], axis=1)], axis=0)
    wl_hi = wl.astype(BF16)
    wl_lo = (wl - wl_hi.astype(F32)).astype(BF16)

    wq = p["mla_w_q_up"][l].reshape(Q_LORA, N_HEADS, QK_D)
    wq = jnp.pad(wq, ((0, 0), (0, 0), (0, HEAD_PAD - QK_D))).reshape(Q_LORA, QP).astype(BF16)
    wkv = p["mla_w_kv_up"][l].reshape(KV_LORA, N_HEADS, NOPE + V_D)
    wkt = jnp.pad(wkv[:, :, :NOPE], ((0, 0), (0, 0), (0, HEAD_PAD - NOPE))).reshape(KV_LORA, QP).T.astype(BF16)
    wv = jnp.pad(wkv[:, :, NOPE:], ((0, 0), (0, 0), (0, HEAD_PAD - V_D))).reshape(KV_LORA, QP).astype(BF16)

    wb = p["w_branch"][l].reshape(N_BRANCH * BW, D_MODEL).astype(BF16)
    wo = p["w_out"][l].astype(BF16)

    return dict(w_rest=w_rest, wg=wg, wl_hi=wl_hi, wl_lo=wl_lo, wq=wq, wkt=wkt, wv=wv, wb=wb, wo=wo)


def _param_tables(p):
    def rows(v, n=1):
        v = v.reshape(DEPTH, n, -1).astype(F32)
        return jnp.pad(v, ((0, 0), (0, 0), (0, D_MODEL - v.shape[-1])))

    every = lambda v, n: jnp.broadcast_to(v.reshape(1, n, -1), (DEPTH, n, v.shape[-1]))
    parts = [rows(p["norm_g"]), rows(p["rwkv_mu"]), rows(p["rwkv_w0"]), rows(p["rwkv_a0"]), rows(p["rwkv_k_k"]),
             rows(p["rwkv_k_a"]), rows(p["rwkv_r_k"]), rows(every(p["hgrn_lb_logits"], DEPTH), DEPTH),
             rows(p["mla_q_norm_g"]), rows(p["mla_kv_norm_g"]), rows(p["rwkv_ln_w"]), rows(p["rwkv_ln_b"]),
             rows(p["hgrn_norm_g"]), rows(p["ret_norm_g"]), rows(every(p["final_norm_g"], 1))]
    used = sum(a.shape[1] for a in parts)
    return jnp.concatenate(parts + [jnp.zeros((DEPTH, N_ROWS - used, D_MODEL), F32)], axis=1)


def _all_weights(p):
    tables = _param_tables(p)
    return [dict(_layer_weights(l, p), par=tables[l]) for l in range(DEPTH)]


def _constants():
    head = np.arange(BW) // HEAD_DIM
    seg = jnp.asarray((head[:, None] == head[None, :]).astype(np.float32), dtype=BF16)
    place = np.zeros((LANES, QP), np.float32)
    for h in range(N_HEADS):
        for j in range(ROPE_D):
            place[j, h * HEAD_PAD + NOPE + j] = 1.0
    return seg, jnp.asarray(place.T, dtype=BF16)


def _state_in(s, B, rep, key_last):
    vr = HEAD_DIM // rep
    if key_last:
        y = s.reshape(B, 2, 2, rep, vr, HEAD_DIM).transpose(5, 4, 3, 2, 0, 1)
    else:
        y = s.reshape(B, 2, 2, HEAD_DIM, rep, vr).transpose(3, 5, 4, 2, 0, 1)
    return y.reshape(HEAD_DIM, vr, LANES).astype(F32)


def _state_out(y, B, rep, key_last):
    vr = HEAD_DIM // rep
    y = y.reshape(HEAD_DIM, vr, rep, 2, B, 2)
    if key_last:
        y = y.transpose(4, 5, 3, 2, 1, 0)
    else:
        y = y.transpose(4, 5, 3, 0, 2, 1)
    return y.reshape(B, N_HEADS, HEAD_DIM, HEAD_DIM)


def _run_trunk(x, shift0, rwkv0, hgrn0, ret0, lat0, kr0, p, weights, consts):
    B, T, _ = x.shape
    past = lat0.shape[2]
    bh = B * N_HEADS
    rep = LANES // bh
    assert rep * bh == LANES and HEAD_DIM % rep == 0 and (HEAD_DIM // rep) % SUBLANES == 0 and N_HEADS == 4
    seg, placet = consts
    tab = _rope_tables(past, T)
    h = x
    per_layer = []
    for l in range(DEPTH):
        w = weights[l]
        (ar, aw, ak, av, akk, ab, posta, bq, bf, bv, gb, qp, lat, kr, gc, dq, dk, dv, gd, shn) = _proj(
            l, h, shift0[l][:, None, :], tab, w["par"], w["w_rest"], w["wl_hi"], w["wl_lo"], seg, w["wq"])

        ya, ob, s_rwkv, s_hgrn = _seq([ar, aw, ak, akk, ab], av, [bq, bf], bv, _state_in(rwkv0[l], B, rep, True),
                                      _state_in(hgrn0[l], B, rep, False), bh)
        od, s_ret = _ret(dq, dk, dv, ret0[l].astype(F32))

        lat_all = jnp.concatenate([lat0[l], lat], axis=1)
        kr_all = jnp.concatenate([jnp.pad(kr0[l], ((0, 0), (0, 0), (0, LANES - ROPE_D))), kr], axis=1)
        tk = past + T
        if past == 0:
            blk = min(512, T)
        else:
            whole = -(-tk // LANES) * LANES
            blk = whole if whole * QP * 2 * 4 <= VMEM_LIMIT // 4 else 256
        tk_pad = -(-tk // blk) * blk
        if tk_pad != tk:
            lat_all = jnp.pad(lat_all, ((0, 0), (0, tk_pad - tk), (0, 0)))
            kr_all = jnp.pad(kr_all, ((0, 0), (0, tk_pad - tk), (0, 0)))
        ktp, vp = _kvprep(lat_all, kr_all, w["wkt"], w["wv"], placet, blk)
        oc = _attn(qp, ktp, vp, past, tk, blk)

        h = _merge(l == DEPTH - 1, h, ya, posta, ob, gb, oc, gc, od, gd, w["par"], seg, w["wg"], w["wb"], w["wo"])
        per_layer.append((shn[-1, :, 0, :], _state_out(s_rwkv, B, rep, True), _state_out(s_hgrn, B, rep, False),
                          s_ret, lat, kr[:, :, :ROPE_D]))
    stacked = [jnp.stack([st[j] for st in per_layer]) for j in range(6)]
    return h, stacked


def kernel(x_prompt, x_sample, state_rwkv_shift, state_rwkv, state_hgrn, cache_mla_latent, cache_mla_krope, state_ret, norm_g, w_in, rwkv_mu, rwkv_w0, rwkv_w_up, rwkv_a0, rwkv_a_up, rwkv_k_k, rwkv_k_a, rwkv_r_k, rwkv_ln_w, rwkv_ln_b, hgrn_lb_logits, hgrn_norm_g, mla_q_norm_g, mla_w_q_up, mla_kv_norm_g, mla_w_kv_up, ret_norm_g, w_branch, w_out, final_norm_g):
    p = dict(norm_g=norm_g, w_in=w_in, rwkv_mu=rwkv_mu, rwkv_w0=rwkv_w0, rwkv_w_up=rwkv_w_up, rwkv_a0=rwkv_a0,
             rwkv_a_up=rwkv_a_up, rwkv_k_k=rwkv_k_k, rwkv_k_a=rwkv_k_a, rwkv_r_k=rwkv_r_k, rwkv_ln_w=rwkv_ln_w,
             rwkv_ln_b=rwkv_ln_b, hgrn_lb_logits=hgrn_lb_logits, hgrn_norm_g=hgrn_norm_g, mla_q_norm_g=mla_q_norm_g,
             mla_w_q_up=mla_w_q_up, mla_kv_norm_g=mla_kv_norm_g, mla_w_kv_up=mla_w_kv_up, ret_norm_g=ret_norm_g,
             w_branch=w_branch, w_out=w_out, final_norm_g=final_norm_g)
    weights = _all_weights(p)
    consts = _constants()
    bp, dt = x_prompt.shape[0], x_prompt.dtype
    zs = jnp.zeros((DEPTH, bp, N_HEADS, HEAD_DIM, HEAD_DIM), dt)
    y_p, (p_shift, p_rwkv, p_hgrn, p_ret, p_lat, p_kr) = _run_trunk(
        x_prompt, jnp.zeros((DEPTH, bp, SHIFT_COLS), dt), zs, zs, zs, jnp.zeros((DEPTH, bp, 0, KV_LORA), dt),
        jnp.zeros((DEPTH, bp, 0, ROPE_D), dt), p, weights, consts)
    y_s, (s_shift, s_rwkv, s_hgrn, s_ret, s_lat, s_kr) = _run_trunk(
        x_sample, state_rwkv_shift, state_rwkv, state_hgrn, state_ret, cache_mla_latent, cache_mla_krope,
        p, weights, consts)
    return (y_p, y_s, p_shift, s_shift, p_rwkv, s_rwkv, p_hgrn, s_hgrn, p_lat, s_lat, p_kr, s_kr, p_ret, s_ret)
```

```python
import functools

import jax
import jax.numpy as jnp
import numpy as np
from jax import lax
from jax.experimental import pallas as pl
from jax.experimental.pallas import tpu as pltpu

F32 = jnp.float32
BF16 = jnp.bfloat16

D_MODEL = 1024
DEPTH = 4
N_HEADS = 4
HEAD_DIM = 64
BW = N_HEADS * HEAD_DIM
N_BRANCH = 4
CHUNK = 64
SHIFT_COLS = 3 * BW + 64 + 64
GN_EPS = 64e-5
NORM_EPS = 1e-6
HEAD_NORM_EPS = 1e-5
ROPE_BASE = 10000.0
Q_LORA, KV_LORA, NOPE, ROPE_D, V_D = 256, 128, 64, 32, 64
QK_D = NOPE + ROPE_D
HEAD_PAD = 128
Q_SCALE = QK_D ** -0.5 * 1.4426950408889634
QP = N_HEADS * HEAD_PAD
LANES = 128
SUBLANES = 8

A0, A1 = 0, SHIFT_COLS + BW
B0, B1 = A1, A1 + 4 * BW
C0, C1 = B1, B1 + Q_LORA + KV_LORA + LANES + BW
D0, D1 = C1, C1 + 4 * BW
PROJ_COLS = D1

T_CD, T_SD, T_CQ, T_SQ, T_CK, T_SK, TAB_COLS = 0, 128, 256, 384, 512, 640, 768

(R_NORM, R_MU, R_W0, R_A0, R_KK, R_KA, R_RK, R_LB, R_QG, R_KVG,
 R_LNW, R_LNB, R_HG, R_RG, R_FG, N_ROWS) = (0, 1, 2, 3, 4, 5, 6, 7, 11, 12, 13, 14, 15, 16, 17, 24)

VMEM_LIMIT = 56 * 1024 * 1024


def _cparams(sem):
    return pltpu.CompilerParams(dimension_semantics=sem, vmem_limit_bytes=VMEM_LIMIT)


def _sigmoid(x):
    return 1.0 / (1.0 + jnp.exp(-x))


def _silu(x):
    return x * _sigmoid(x)


def _softplus(x):
    return jnp.maximum(x, 0.0) + jnp.log(1.0 + jnp.exp(-jnp.abs(x)))


def _dot(a, b):
    return jnp.dot(a, b, preferred_element_type=F32)


def _split(x):
    hi = x.astype(BF16)
    lo = (x - hi.astype(F32)).astype(BF16)
    return hi, lo


def _dot_lhs2(x, m_bf16):
    hi, lo = _split(x)
    return _dot(hi, m_bf16) + _dot(lo, m_bf16)


def _dot3(x, m_hi, m_lo):
    hi, lo = _split(x)
    return _dot(hi, m_hi) + _dot(lo, m_hi) + _dot(hi, m_lo)


def _rms(x, g, eps):
    return x * lax.rsqrt(jnp.mean(x * x, axis=-1, keepdims=True) + eps) * g


def _rope(x, cos, sin_signed, half, group, start=0):
    w = x.shape[-1]
    lane = lax.broadcasted_iota(jnp.int32, (1, w), 1) % group
    up = pltpu.roll(x, w - half, axis=1)
    dn = pltpu.roll(x, half, axis=1)
    return x * cos + jnp.where(lane < start + half, up, dn) * sin_signed


def _proj_kernel(layer, tt, h_ref, sh0_ref, tab_ref, par_ref, w_ref, wl_hi_ref, wl_lo_ref, seg_ref, wq_ref,
                 ar_ref, aw_ref, ak_ref, av_ref, akk_ref, ab_ref, posta_ref,
                 bq_ref, bf_ref, bv_ref, gb_ref,
                 qp_ref, lat_ref, kr_ref, gc_ref,
                 dq_ref, dk_ref, dv_ref, gd_ref, shn_ref, carry_ref):
    t = pl.program_id(0)
    b = pl.program_id(1)
    par = lambda r, n: par_ref[r:r + 1, 0:n]
    seg = seg_ref[...]

    u = _rms(h_ref[0], par(R_NORM, D_MODEL), NORM_EPS).astype(BF16)

    pa_all = _dot(u, w_ref[:, A0:A1])
    pa = pa_all[:, :SHIFT_COLS]
    prev_row = jnp.where(t == 0, sh0_ref[b], carry_ref[b])
    row = lax.broadcasted_iota(jnp.int32, (tt, 1), 0)
    p_prev = jnp.where(row == 0, prev_row, pltpu.roll(pa, 1, axis=0))
    last = pa[tt - 1:tt, :]
    carry_ref[b] = last
    shn_ref[0, 0] = last
    xs = pa + (p_prev - pa) * par(R_MU, SHIFT_COLS)
    r, k, v, wdad = xs[:, :BW], xs[:, BW:2 * BW], xs[:, 2 * BW:3 * BW], xs[:, 3 * BW:]
    lane = lax.broadcasted_iota(jnp.int32, (1, LANES), 1)
    lora = _dot3(jnp.where(lane < 64, jnp.tanh(wdad), wdad), wl_hi_ref[...], wl_lo_ref[...])
    w_raw = -_softplus(-(par(R_W0, BW) + lora[:, :BW])) - 0.5
    a = _sigmoid(par(R_A0, BW) + lora[:, BW:])
    kk = k * par(R_KK, BW)
    kk = kk / jnp.maximum(jnp.sqrt(_dot_lhs2(kk * kk, seg)), 1e-12)
    k2 = k * (1.0 + (a - 1.0) * par(R_KA, BW))
    ar_ref[...] = r
    aw_ref[...] = jnp.exp(-jnp.exp(w_raw))
    ak_ref[...] = k2
    av_ref[...] = v
    akk_ref[...] = kk
    ab_ref[...] = kk * a
    posta_ref[:, :BW] = _dot_lhs2(r * k2 * par(R_RK, BW), seg) * v
    posta_ref[:, BW:] = _silu(pa_all[:, SHIFT_COLS:])

    pb = _dot(u, w_ref[:, B0:B1])
    lg = par_ref[R_LB:R_LB + DEPTH, 0:BW]
    e = jnp.exp(lg - jnp.max(lg, axis=0, keepdims=True))
    lb = jnp.zeros((1, BW), F32)
    for j in range(1, layer + 1):
        lb = lb + e[j:j + 1]
    lb = lb / jnp.sum(e, axis=0, keepdims=True)
    z = pb[:, BW:2 * BW]
    bq_ref[...] = _silu(pb[:, :BW]) * (HEAD_DIM ** -0.5)
    bf_ref[...] = lb + (1.0 - lb) * _sigmoid(z)
    bv_ref[...] = pb[:, 2 * BW:3 * BW]
    gb_ref[...] = _silu(pb[:, 3 * BW:])

    pc = _dot(u, w_ref[:, C0:C1])
    qn = _rms(pc[:, :Q_LORA], par(R_QG, Q_LORA), NORM_EPS).astype(BF16)
    q = _dot(qn, wq_ref[...]) * Q_SCALE
    tab = lambda c, n: jnp.concatenate([tab_ref[:, c:c + LANES]] * n, axis=1)
    qp_ref[0] = _rope(q, tab(T_CQ, N_HEADS), tab(T_SQ, N_HEADS), ROPE_D // 2, HEAD_PAD, NOPE).astype(BF16)
    lat_ref[0] = _rms(pc[:, Q_LORA:Q_LORA + KV_LORA], par(R_KVG, KV_LORA), NORM_EPS)
    kr_ref[0] = _rope(pc[:, Q_LORA + KV_LORA:Q_LORA + KV_LORA + LANES], tab_ref[:, T_CK:T_CK + LANES],
                      tab_ref[:, T_SK:T_SK + LANES], ROPE_D // 2, LANES)
    gc_ref[0] = _silu(pc[:, Q_LORA + KV_LORA + LANES:])

    pd = _dot(u, w_ref[:, D0:D1])
    cos, sin = tab(T_CD, BW // LANES), tab(T_SD, BW // LANES)
    dq_ref[0] = _rope(pd[:, :BW], cos, sin, HEAD_DIM // 2, HEAD_DIM)
    dk_ref[0] = _rope(pd[:, BW:2 * BW], cos, sin, HEAD_DIM // 2, HEAD_DIM) * (HEAD_DIM ** -0.5)
    dv_ref[0] = pd[:, 2 * BW:3 * BW]
    gd_ref[0] = _silu(pd[:, 3 * BW:])


def _proj(layer, h, shift0, tab, par, w_rest, wl_hi, wl_lo, seg, wq):
    B, T, _ = h.shape
    tt = min(512, T)
    nt = T // tt
    bspec = lambda c: pl.BlockSpec((1, tt, c), lambda t, b: (b, t, 0))
    tspec = lambda c: pl.BlockSpec((tt, c), lambda t, b: (t, b))
    full = lambda a: pl.BlockSpec(a.shape, lambda t, b: (0,) * a.ndim)
    bm = lambda c, dt=F32: jax.ShapeDtypeStruct((B, T, c), dt)
    tm = lambda c: jax.ShapeDtypeStruct((T, B * c), F32)
    outs = ([(tm(BW), tspec(BW))] * 6 + [(tm(2 * BW), tspec(2 * BW))]
            + [(tm(BW), tspec(BW))] * 4
            + [(bm(QP, BF16), bspec(QP)), (bm(KV_LORA), bspec(KV_LORA)), (bm(LANES), bspec(LANES)),
               (bm(BW), bspec(BW))]
            + [(bm(BW), bspec(BW))] * 4
            + [(jax.ShapeDtypeStruct((nt, B, 1, SHIFT_COLS), F32),
                pl.BlockSpec((1, 1, 1, SHIFT_COLS), lambda t, b: (t, b, 0, 0)))])
    return pl.pallas_call(
        functools.partial(_proj_kernel, layer, tt),
        grid=(nt, B),
        in_specs=[bspec(D_MODEL), full(shift0), pl.BlockSpec((tt, TAB_COLS), lambda t, b: (t, 0)), full(par),
                  full(w_rest), full(wl_hi), full(wl_lo), full(seg), full(wq)],
        out_specs=tuple(o[1] for o in outs),
        out_shape=tuple(o[0] for o in outs),
        scratch_shapes=[pltpu.VMEM((B, 1, SHIFT_COLS), F32)],
        compiler_params=_cparams(("arbitrary", "arbitrary")),
        name="proj",
    )(h, shift0, tab, par, w_rest, wl_hi, wl_lo, seg, wq)


N_ACC = 1


N_COL = 8
LOOKAHEAD = 2
N_SLOTS = 4


def _seq_kernel(tc, bh, r_ref, w_ref, k_ref, kk_ref, b_ref, va_ref, q_ref, f_ref, vb_ref, sa0_ref, sb0_ref,
                ya_ref, yb_ref, sa_out_ref, sb_out_ref, st_a, st_b, *ring):
    slots, vslots, yslots = ring[0:N_SLOTS], ring[N_SLOTS:2 * N_SLOTS], ring[2 * N_SLOTS:3 * N_SLOTS]
    g_sz = bh // 2
    rep = LANES // bh
    vr = HEAD_DIM // rep
    n_pairs = tc // 2
    lane = lax.broadcasted_iota(jnp.int32, (1, LANES), 1)
    low = lane < HEAD_DIM
    hi_id = lane // bh
    grid_step = pl.program_id(0)

    @pl.when(grid_step == 0)
    def _():
        st_a[...] = sa0_ref[...]
        st_b[...] = sb0_ref[...]

    def paired(ref, p):
        a, b = ref[2 * p], ref[2 * p + 1]
        s0 = jnp.where(low, a, pltpu.roll(b, HEAD_DIM, axis=1))
        s1 = jnp.where(low, pltpu.roll(a, HEAD_DIM, axis=1), b)
        return jnp.concatenate([s0, s1] * rep, axis=0).T

    def prepare(p, slot, vslot):
        for o, ref in enumerate((r_ref, w_ref, k_ref, kk_ref, b_ref, q_ref, f_ref)):
            xt = paired(ref, p)
            slot[o] = xt
            if ref is f_ref:
                slot[o + 1] = 1.0 - xt
        for ri, ref in enumerate((va_ref, vb_ref)):
            xt = paired(ref, p)
            for s in range(2):
                vv = xt[s * HEAD_DIM:s * HEAD_DIM + vr]
                for g in range(1, rep):
                    vv = jnp.where(hi_id == g, xt[s * HEAD_DIM + g * vr:s * HEAD_DIM + (g + 1) * vr], vv)
                vslot[ri, s] = vv

    def emit(p, yslot):
        for ri, y_ref in enumerate((ya_ref, yb_ref)):
            blocks = [jnp.where(hi_id == g, yslot[ri, s], 0.0) for s in range(2) for g in range(rep)]
            zt = jnp.concatenate(blocks, axis=0).T
            yp = zt[0:bh]
            for g in range(1, rep):
                yp = yp + zt[g * bh:(g + 1) * bh]
            p0, p1 = yp[:g_sz], yp[g_sz:]
            y_ref[2 * p] = jnp.where(low, p0, pltpu.roll(p1, HEAD_DIM, axis=1))
            y_ref[2 * p + 1] = jnp.where(low, pltpu.roll(p0, HEAD_DIM, axis=1), p1)

    def tree(xs):
        while len(xs) > 1:
            xs = [xs[i] + xs[i + 1] for i in range(0, len(xs), 2)]
        return xs[0]

    def bc(slot, o, row):
        return jnp.broadcast_to(slot[o, pl.ds(row, 1), :], (vr, LANES))

    def accumulate(acc, k, p):
        acc[k % N_ACC] = p if acc[k % N_ACC] is None else acc[k % N_ACC] + p

    def first_sa(slot):
        acc = [None] * N_ACC
        for k in range(HEAD_DIM):
            accumulate(acc, k, st_a[k] * bc(slot, 3, k))
        return -tree(acc)

    def rwkv_step(s, slot, kk_slot, kk_row, vslot, yslot, sa):
        vv = vslot[0, s]
        acc_y, acc_s = [None] * N_ACC, [None] * N_ACC
        for k in range(HEAD_DIM):
            row = s * HEAD_DIM + k
            new = st_a[k] * bc(slot, 1, row) + sa * bc(slot, 4, row) + vv * bc(slot, 2, row)
            st_a[k] = new
            accumulate(acc_y, k, new * bc(slot, 0, row))
            accumulate(acc_s, k, new * bc(kk_slot, 3, kk_row + k))
        yslot[0, s] = tree(acc_y)
        return -tree(acc_s)

    def hgrn_pair(slot, vslot, yslot):
        v0, v1 = vslot[1, 0], vslot[1, 1]
        acc0, acc1 = [None] * N_ACC, [None] * N_ACC
        for k in range(HEAD_DIM):
            mid = st_b[k] * bc(slot, 6, k) + v0 * bc(slot, 7, k)
            accumulate(acc0, k, mid * bc(slot, 5, k))
            new = mid * bc(slot, 6, HEAD_DIM + k) + v1 * bc(slot, 7, HEAD_DIM + k)
            st_b[k] = new
            accumulate(acc1, k, new * bc(slot, 5, HEAD_DIM + k))
        yslot[1, 0] = tree(acc0)
        yslot[1, 1] = tree(acc1)

    def pair_of_steps(p, j, sa):
        emit(jnp.maximum(p - 1, 0), yslots[(j - 1) % N_SLOTS])
        nxt = (j + LOOKAHEAD) % N_SLOTS
        prepare(jnp.minimum(p + LOOKAHEAD, n_pairs - 1), slots[nxt], vslots[nxt])
        sa = rwkv_step(0, slots[j], slots[j], HEAD_DIM, vslots[j], yslots[j], sa)
        sa = rwkv_step(1, slots[j], slots[(j + 1) % N_SLOTS], 0, vslots[j], yslots[j], sa)
        hgrn_pair(slots[j], vslots[j], yslots[j])
        return sa

    for j in range(LOOKAHEAD):
        prepare(j, slots[j], vslots[j])
    yslots[N_SLOTS - 1][...] = jnp.zeros(yslots[N_SLOTS - 1].shape, F32)

    def body(i, sa):
        for j in range(N_SLOTS):
            sa = pair_of_steps(N_SLOTS * i + j, j, sa)
        return sa

    lax.fori_loop(0, n_pairs // N_SLOTS, body, first_sa(slots[0]))
    emit(n_pairs - 1, yslots[N_SLOTS - 1])

    @pl.when(grid_step == pl.num_programs(0) - 1)
    def _():
        sa_out_ref[...] = st_a[...]
        sb_out_ref[...] = st_b[...]


def _seq(cols_a, va, cols_b, vb, sa0, sb0, bh):
    T = va.shape[0]
    g_sz = bh // 2
    vr = sa0.shape[1]
    tc = min(128, T)
    assert tc % (2 * N_SLOTS) == 0
    view = lambda a: a.reshape(T, g_sz, LANES)
    ospec = pl.BlockSpec((tc, g_sz, LANES), lambda g: (g, 0, 0))
    sspec = pl.BlockSpec((HEAD_DIM, vr, LANES), lambda g: (0, 0, 0))
    args = [view(c) for c in (*cols_a, va, *cols_b, vb)] + [sa0, sb0]
    y_shape = jax.ShapeDtypeStruct((T, g_sz, LANES), F32)
    s_shape = jax.ShapeDtypeStruct((HEAD_DIM, vr, LANES), F32)
    slot = pltpu.VMEM((N_COL, 2 * HEAD_DIM, LANES), F32)
    small = pltpu.VMEM((2, 2, vr, LANES), F32)
    ya, yb, sa, sb = pl.pallas_call(
        functools.partial(_seq_kernel, tc, bh),
        grid=(T // tc,),
        in_specs=[ospec] * (len(args) - 2) + [sspec, sspec],
        out_specs=(ospec, ospec, sspec, sspec),
        out_shape=(y_shape, y_shape, s_shape, s_shape),
        scratch_shapes=[pltpu.VMEM((HEAD_DIM, vr, LANES), F32), pltpu.VMEM((HEAD_DIM, vr, LANES), F32),
                        *([slot] * N_SLOTS + [small] * (2 * N_SLOTS))],
        compiler_params=_cparams(("arbitrary",)),
        name="seq",
    )(*args)
    return ya.reshape(T, bh * HEAD_DIM), yb.reshape(T, bh * HEAD_DIM), sa, sb


def _ret_kernel(L, q_ref, k_ref, v_ref, s0_ref, o_ref, s_out_ref, st, dm, gq, gk):
    i = pl.program_id(1)
    log2_gamma = [float(np.log2(1.0 - 2.0 ** (-5.0 - h))) for h in range(N_HEADS)]

    @pl.when(i == 0)
    def _():
        st[...] = s0_ref[0]
        t_idx = lax.broadcasted_iota(jnp.int32, (L, L), 0)
        s_idx = lax.broadcasted_iota(jnp.int32, (L, L), 1)
        row = lax.broadcasted_iota(jnp.int32, (L, HEAD_DIM), 0).astype(F32)
        for h in range(N_HEADS):
            d = (t_idx - s_idx).astype(F32)
            dm[h] = jnp.where(t_idx >= s_idx, jnp.exp2(d * log2_gamma[h]), 0.0)
            gq[h] = jnp.exp2((row + 1.0) * log2_gamma[h])
            gk[h] = jnp.exp2((L - 1.0 - row) * log2_gamma[h])

    for h in range(N_HEADS):
        hs = slice(h * HEAD_DIM, (h + 1) * HEAD_DIM)
        q, k, v = q_ref[0, :, hs], k_ref[0, :, hs], v_ref[0, :, hs].astype(BF16)
        a = lax.dot_general(q.astype(BF16), k.astype(BF16), (((1,), (1,)), ((), ())),
                            preferred_element_type=F32) * dm[h]
        s_prev = st[h]
        o_ref[0, :, hs] = _dot(a.astype(BF16), v) + _dot((q * gq[h]).astype(BF16), s_prev.astype(BF16))
        kd = (k * gk[h]).astype(BF16)
        st[h] = s_prev * (2.0 ** (L * log2_gamma[h])) + lax.dot_general(
            kd, v, (((0,), (0,)), ((), ())), preferred_element_type=F32)

    @pl.when(i == pl.num_programs(1) - 1)
    def _():
        s_out_ref[0] = st[...]


def _ret(q, k, v, s0):
    B, T, _ = q.shape
    L = min(512, T)
    tspec = pl.BlockSpec((1, L, BW), lambda b, i: (b, i, 0))
    sspec = pl.BlockSpec((1, N_HEADS, HEAD_DIM, HEAD_DIM), lambda b, i: (b, 0, 0, 0))
    return pl.pallas_call(
        functools.partial(_ret_kernel, L),
        grid=(B, T // L),
        in_specs=[tspec, tspec, tspec, sspec],
        out_specs=(tspec, sspec),
        out_shape=(jax.ShapeDtypeStruct((B, T, BW), F32),
                   jax.ShapeDtypeStruct((B, N_HEADS, HEAD_DIM, HEAD_DIM), F32)),
        scratch_shapes=[pltpu.VMEM((N_HEADS, HEAD_DIM, HEAD_DIM), F32), pltpu.VMEM((N_HEADS, L, L), F32),
                        pltpu.VMEM((N_HEADS, L, HEAD_DIM), F32), pltpu.VMEM((N_HEADS, L, HEAD_DIM), F32)],
        compiler_params=_cparams(("arbitrary", "arbitrary")),
        name="ret",
    )(q, k, v, s0)


def _keys_values(lat, kr, wkt_ref, wv_ref, placet):
    lat = lat.astype(BF16)
    nt = (((1,), (1,)), ((), ()))
    kt = (lax.dot_general(wkt_ref[...], lat, nt, preferred_element_type=F32)
          + lax.dot_general(placet, kr.astype(BF16), nt, preferred_element_type=F32)).astype(BF16)
    lane = lax.broadcasted_iota(jnp.int32, (1, QP), 1) % HEAD_PAD
    return kt, jnp.where(lane == V_D, 1.0, _dot(lat, wv_ref[...])).astype(BF16)


def _kvprep_kernel(lat_ref, kr_ref, wkt_ref, wv_ref, placet_ref, kt_ref, v_ref):
    kt_ref[0], v_ref[0] = _keys_values(lat_ref[0], kr_ref[0], wkt_ref, wv_ref, placet_ref[...])


def _kvprep(lat, kr, wkt, wv, placet, tr):
    B, tk, _ = lat.shape
    rspec = lambda c: pl.BlockSpec((1, tr, c), lambda b, i: (b, i, 0))
    full = lambda a: pl.BlockSpec(a.shape, lambda b, i: (0,) * a.ndim)
    return pl.pallas_call(
        _kvprep_kernel,
        grid=(B, tk // tr),
        in_specs=[rspec(KV_LORA), rspec(LANES), full(wkt), full(wv), full(placet)],
        out_specs=(pl.BlockSpec((1, QP, tr), lambda b, i: (b, 0, i)), rspec(QP)),
        out_shape=(jax.ShapeDtypeStruct((B, QP, tk), BF16), jax.ShapeDtypeStruct((B, tk, QP), BF16)),
        compiler_params=_cparams(("arbitrary", "arbitrary")),
        name="kvprep",
    )(lat, kr, wkt, wv, placet)


def _kvprep_cached_kernel(past, T, plat_ref, pkr_ref, lat_ref, kr_ref, wkt_ref, wv_ref, placet_ref, kt_ref, v_ref):
    tk = past + T
    kt_ref[0, :, 0:past], v_ref[0, 0:past, :] = _keys_values(plat_ref[0], pkr_ref[0], wkt_ref, wv_ref,
                                                             placet_ref[:, 0:ROPE_D])
    kt_ref[0, :, past:tk], v_ref[0, past:tk, :] = _keys_values(lat_ref[0], kr_ref[0], wkt_ref, wv_ref, placet_ref[...])
    pad = kt_ref.shape[2] - tk
    if pad:
        kt_ref[0, :, tk:] = jnp.zeros((QP, pad), BF16)
        v_ref[0, tk:, :] = jnp.zeros((pad, QP), BF16)


def _kvprep_cached(past_lat, past_kr, lat, kr, wkt, wv, placet, tk_pad):
    B, past, _ = past_lat.shape
    T = lat.shape[1]
    whole = lambda a: pl.BlockSpec((1,) + a.shape[1:], lambda b: (b, 0, 0))
    full = lambda a: pl.BlockSpec(a.shape, lambda b: (0,) * a.ndim)
    return pl.pallas_call(
        functools.partial(_kvprep_cached_kernel, past, T),
        grid=(B,),
        in_specs=[whole(past_lat), whole(past_kr), whole(lat), whole(kr), full(wkt), full(wv), full(placet)],
        out_specs=(pl.BlockSpec((1, QP, tk_pad), lambda b: (b, 0, 0)), pl.BlockSpec((1, tk_pad, QP), lambda b: (b, 0, 0))),
        out_shape=(jax.ShapeDtypeStruct((B, QP, tk_pad), BF16), jax.ShapeDtypeStruct((B, tk_pad, QP), BF16)),
        compiler_params=_cparams(("arbitrary",)),
        name="kvprep_cached",
    )(past_lat, past_kr, lat, kr, wkt, wv, placet)


A_FIRST, A_LAST, A_MASKED = 1, 2, 4


def _attn_kernel(bq, bk, q0, kv_len, qi_ref, kj_ref, flag_ref, q_ref, kt_ref, v_ref, o_ref, m_ref, acc_ref):
    step = pl.program_id(1)
    i = qi_ref[step]
    j = kj_ref[step]
    flags = flag_ref[step]

    @pl.when((flags & A_FIRST) != 0)
    def _():
        m_ref[...] = jnp.full(m_ref.shape, -jnp.inf, F32)
        acc_ref[...] = jnp.zeros(acc_ref.shape, F32)

    def block(masked):
        if masked:
            qi = q0 + i * bq + lax.broadcasted_iota(jnp.int32, (bq, bk), 0)
            ki = j * bk + lax.broadcasted_iota(jnp.int32, (bq, bk), 1)
            keep = jnp.logical_and(ki // CHUNK <= qi // CHUNK, ki < kv_len)
        for h in range(N_HEADS):
            hs = slice(h * HEAD_PAD, (h + 1) * HEAD_PAD)
            s = _dot(q_ref[0, :, hs], kt_ref[0, hs, :])
            if masked:
                s = jnp.where(keep, s, -jnp.inf)
            m_old = m_ref[h]
            m_new = jnp.maximum(m_old, jnp.max(s, axis=-1, keepdims=True))
            alpha = jnp.exp2(m_old - m_new)
            m_full = jnp.concatenate([m_new] * (bk // LANES), axis=1) if bk % LANES == 0 else m_new[:, 0:1]
            p = jnp.exp2(s - m_full)
            acc_ref[:, hs] = alpha * acc_ref[:, hs] + _dot(p.astype(BF16), v_ref[0, :, hs])
            m_ref[h] = m_new

    @pl.when((flags & A_MASKED) == 0)
    def _():
        block(False)

    @pl.when((flags & A_MASKED) != 0)
    def _():
        block(True)

    @pl.when((flags & A_LAST) != 0)
    def _():
        for h in range(N_HEADS):
            a = acc_ref[:, h * HEAD_PAD:(h + 1) * HEAD_PAD]
            o_ref[0, :, h * V_D:(h + 1) * V_D] = a[:, :V_D] / a[:, V_D:V_D + 1]


def _attn(qp, ktp, vp, q0, kv_len, bk):
    B, T, _ = qp.shape
    tk = vp.shape[1]
    bq = min(512, T)
    nq, nk = T // bq, tk // bk
    qi, kj, flags = [], [], []
    for i in range(nq):
        first_q, last_q = q0 + i * bq, q0 + (i + 1) * bq - 1
        n_blocks = min(nk, (min(kv_len, (last_q // CHUNK + 1) * CHUNK) - 1) // bk + 1)
        for j in range(n_blocks):
            fully_visible = (j + 1) * bk <= min(kv_len, (first_q // CHUNK + 1) * CHUNK)
            qi.append(i)
            kj.append(j)
            flags.append((A_FIRST if j == 0 else 0) | (A_LAST if j == n_blocks - 1 else 0)
                         | (0 if fully_visible else A_MASKED))
    sched = [jnp.asarray(np.asarray(a, np.int32)) for a in (qi, kj, flags)]
    grid_spec = pltpu.PrefetchScalarGridSpec(
        num_scalar_prefetch=3,
        grid=(B, len(qi)),
        in_specs=[pl.BlockSpec((1, bq, QP), lambda b, s, qi, kj, fl: (b, qi[s], 0)),
                  pl.BlockSpec((1, QP, bk), lambda b, s, qi, kj, fl: (b, 0, kj[s])),
                  pl.BlockSpec((1, bk, QP), lambda b, s, qi, kj, fl: (b, kj[s], 0))],
        out_specs=pl.BlockSpec((1, bq, BW), lambda b, s, qi, kj, fl: (b, qi[s], 0)),
        scratch_shapes=[pltpu.VMEM((N_HEADS, bq, LANES), F32), pltpu.VMEM((bq, QP), F32)])
    return pl.pallas_call(
        functools.partial(_attn_kernel, bq, bk, q0, kv_len),
        grid_spec=grid_spec,
        out_shape=jax.ShapeDtypeStruct((B, T, BW), F32),
        compiler_params=_cparams(("arbitrary", "arbitrary")),
        name="attn",
    )(*sched, qp, ktp, vp)


def _merge_kernel(final, h_ref, ya_ref, posta_ref, ob_ref, gb_ref, oc_ref, gc_ref, od_ref, gd_ref,
                  par_ref, seg_ref, wg_ref, wb_ref, wo_ref, out_ref):
    par = lambda r, n: par_ref[r:r + 1, 0:n]
    seg = seg_ref[...]
    h = h_ref[0]
    u = _rms(h, par(R_NORM, D_MODEL), NORM_EPS).astype(BF16)
    inv = 1.0 / HEAD_DIM

    ya = ya_ref[...]
    xc = ya - _dot_lhs2(ya, seg) * inv
    yn = xc * lax.rsqrt(_dot_lhs2(xc * xc, seg) * inv + GN_EPS) * par(R_LNW, BW) + par(R_LNB, BW)
    oa = (yn + posta_ref[:, :BW]) * posta_ref[:, BW:]

    def head_rms(o, g):
        return o * lax.rsqrt(_dot_lhs2(o * o, seg) * inv + HEAD_NORM_EPS) * g

    ob = head_rms(ob_ref[...], par(R_HG, BW)) * gb_ref[...]
    oc = oc_ref[0] * gc_ref[0]
    od = head_rms(od_ref[0], par(R_RG, BW)) * gd_ref[0]

    merged = None
    for n, o in enumerate((oa, ob, oc, od)):
        gate = _sigmoid(_dot(u, wg_ref[:, n * D_MODEL:(n + 1) * D_MODEL]))
        term = gate * _dot(o.astype(BF16), wb_ref[n * BW:(n + 1) * BW, :])
        merged = term if merged is None else merged + term
    out = h + _dot(merged.astype(BF16), wo_ref[...])
    if final:
        out = _rms(out, par(R_FG, D_MODEL), NORM_EPS)
    out_ref[0] = out


def _merge(final, h, ya, posta, ob, gb, oc, gc, od, gd, par, seg, wg, wb, wo):
    B, T, _ = h.shape
    tt = min(512, T)
    bspec = lambda c: pl.BlockSpec((1, tt, c), lambda t, b: (b, t, 0))
    tspec = lambda c: pl.BlockSpec((tt, c), lambda t, b: (t, b))
    full = lambda a: pl.BlockSpec(a.shape, lambda t, b: (0,) * a.ndim)
    consts = (par, seg, wg, wb, wo)
    return pl.pallas_call(
        functools.partial(_merge_kernel, final),
        grid=(T // tt, B),
        in_specs=[bspec(D_MODEL), tspec(BW), tspec(2 * BW), tspec(BW), tspec(BW), bspec(BW), bspec(BW), bspec(BW),
                  bspec(BW)] + [full(a) for a in consts],
        out_specs=bspec(D_MODEL),
        out_shape=jax.ShapeDtypeStruct((B, T, D_MODEL), F32),
        compiler_params=_cparams(("arbitrary", "arbitrary")),
        name="merge",
    )(h, ya, posta, ob, gb, oc, gc, od, gd, *consts)


def _rope_tables(past, T):
    pos = (past + jnp.arange(T, dtype=jnp.int32)).astype(F32)[:, None]

    def tables(width, group, start, d):
        lane = np.arange(width) % group - start
        on = (lane >= 0) & (lane < d)
        idx = np.where(on, lane % (d // 2), 0)
        inv_freq = jnp.power(ROPE_BASE, -jnp.arange(0, d, 2, dtype=F32) / d)
        ang = pos * inv_freq[None, :]
        cos = jnp.where(on[None, :], jnp.cos(ang)[:, idx], 1.0)
        sign = np.where(lane < d // 2, -1.0, 1.0).astype(np.float32)
        sin = jnp.where(on[None, :], jnp.sin(ang)[:, idx] * sign[None, :], 0.0)
        return cos, sin

    cd, sd = tables(LANES, HEAD_DIM, 0, HEAD_DIM)
    cq, sq = tables(LANES, HEAD_PAD, NOPE, ROPE_D)
    ck, sk = tables(LANES, LANES, 0, ROPE_D)
    return jnp.concatenate([cd, sd, cq, sq, ck, sk], axis=1)


def _layer_weights(l, p):
    w_in = p["w_in"][l]
    gate_cols = N_BRANCH * D_MODEL
    c_kr = gate_cols + SHIFT_COLS + BW + 4 * BW + Q_LORA + KV_LORA
    c_g = c_kr + ROPE_D
    w_rest = jnp.concatenate([
        w_in[:, gate_cols:c_kr],
        jnp.pad(w_in[:, c_kr:c_g], ((0, 0), (0, LANES - ROPE_D))),
        w_in[:, c_g:],
    ], axis=1).astype(BF16)
    wg = w_in[:, :gate_cols].astype(BF16)

    z = jnp.zeros((64, BW), F32)
    wl = jnp.concatenate([jnp.concatenate([p["rwkv_w_up"][l], z], axis=1),
                          jnp.concatenate([z, p["rwkv_a_up"][l]], axis=1)], axis=0)
    wl_hi = wl.astype(BF16)
    wl_lo = (wl - wl_hi.astype(F32)).astype(BF16)

    wq = p["mla_w_q_up"][l].reshape(Q_LORA, N_HEADS, QK_D)
    wq = jnp.pad(wq, ((0, 0), (0, 0), (0, HEAD_PAD - QK_D))).reshape(Q_LORA, QP).astype(BF16)
    wkv = p["mla_w_kv_up"][l].reshape(KV_LORA, N_HEADS, NOPE + V_D)
    wkt = jnp.pad(wkv[:, :, :NOPE], ((0, 0), (0, 0), (0, HEAD_PAD - NOPE))).reshape(KV_LORA, QP).T.astype(BF16)
    wv = jnp.pad(wkv[:, :, NOPE:], ((0, 0), (0, 0), (0, HEAD_PAD - V_D))).reshape(KV_LORA, QP).astype(BF16)

    wb = p["w_branch"][l].reshape(N_BRANCH * BW, D_MODEL).astype(BF16)
    wo = p["w_out"][l].astype(BF16)

    return dict(w_rest=w_rest, wg=wg, wl_hi=wl_hi, wl_lo=wl_lo, wq=wq, wkt=wkt, wv=wv, wb=wb, wo=wo)


def _param_tables(p):
    def rows(v, n=1):
        v = v.reshape(DEPTH, n, -1).astype(F32)
        return jnp.pad(v, ((0, 0), (0, 0), (0, D_MODEL - v.shape[-1])))

    every = lambda v, n: jnp.broadcast_to(v.reshape(1, n, -1), (DEPTH, n, v.shape[-1]))
    parts = [rows(p["norm_g"]), rows(p["rwkv_mu"]), rows(p["rwkv_w0"]), rows(p["rwkv_a0"]), rows(p["rwkv_k_k"]),
             rows(p["rwkv_k_a"]), rows(p["rwkv_r_k"]), rows(every(p["hgrn_lb_logits"], DEPTH), DEPTH),
             rows(p["mla_q_norm_g"]), rows(p["mla_kv_norm_g"]), rows(p["rwkv_ln_w"]), rows(p["rwkv_ln_b"]),
             rows(p["hgrn_norm_g"]), rows(p["ret_norm_g"]), rows(every(p["final_norm_g"], 1))]
    used = sum(a.shape[1] for a in parts)
    return jnp.concatenate(parts + [jnp.zeros((DEPTH, N_ROWS - used, D_MODEL), F32)], axis=1)


def _all_weights(p):
    tables = _param_tables(p)
    return [dict(_layer_weights(l, p), par=tables[l]) for l in range(DEPTH)]


def _constants():
    head = np.arange(BW) // HEAD_DIM
    seg = jnp.asarray((head[:, None] == head[None, :]).astype(np.float32), dtype=BF16)
    place = np.zeros((LANES, QP), np.float32)
    for h in range(N_HEADS):
        for j in range(ROPE_D):
            place[j, h * HEAD_PAD + NOPE + j] = 1.0
    return seg, jnp.asarray(place.T, dtype=BF16)


def _state_in(s, B, rep, key_last):
    vr = HEAD_DIM // rep
    if key_last:
        y = s.reshape(B, 2, 2, rep, vr, HEAD_DIM).transpose(5, 4, 3, 2, 0, 1)
    else:
        y = s.reshape(B, 2, 2, HEAD_DIM, rep, vr).transpose(3, 5, 4, 2, 0, 1)
    return y.reshape(HEAD_DIM, vr, LANES).astype(F32)


def _state_out(y, B, rep, key_last):
    vr = HEAD_DIM // rep
    y = y.reshape(HEAD_DIM, vr, rep, 2, B, 2)
    if key_last:
        y = y.transpose(4, 5, 3, 2, 1, 0)
    else:
        y = y.transpose(4, 5, 3, 0, 2, 1)
    return y.reshape(B, N_HEADS, HEAD_DIM, HEAD_DIM)


def _run_trunk(x, shift0, rwkv0, hgrn0, ret0, lat0, kr0, p, weights, consts):
    B, T, _ = x.shape
    past = lat0.shape[2]
    bh = B * N_HEADS
    rep = LANES // bh
    assert rep * bh == LANES and HEAD_DIM % rep == 0 and (HEAD_DIM // rep) % SUBLANES == 0 and N_HEADS == 4
    seg, placet = consts
    tab = _rope_tables(past, T)
    h = x
    per_layer = []
    for l in range(DEPTH):
        w = weights[l]
        (ar, aw, ak, av, akk, ab, posta, bq, bf, bv, gb, qp, lat, kr, gc, dq, dk, dv, gd, shn) = _proj(
            l, h, shift0[l][:, None, :], tab, w["par"], w["w_rest"], w["wl_hi"], w["wl_lo"], seg, w["wq"])

        ya, ob, s_rwkv, s_hgrn = _seq([ar, aw, ak, akk, ab], av, [bq, bf], bv, _state_in(rwkv0[l], B, rep, True),
                                      _state_in(hgrn0[l], B, rep, False), bh)
        od, s_ret = _ret(dq, dk, dv, ret0[l].astype(F32))

        tk = past + T
        if past == 0:
            blk = min(512, T)
            ktp, vp = _kvprep(lat, kr, w["wkt"], w["wv"], placet, blk)
        else:
            blk = -(-tk // LANES) * LANES
            assert blk * QP * 2 * 4 <= VMEM_LIMIT // 4 and past % LANES == 0
            ktp, vp = _kvprep_cached(lat0[l], kr0[l], lat, kr, w["wkt"], w["wv"], placet, blk)
        oc = _attn(qp, ktp, vp, past, tk, blk)

        h = _merge(l == DEPTH - 1, h, ya, posta, ob, gb, oc, gc, od, gd, w["par"], seg, w["wg"], w["wb"], w["wo"])
        per_layer.append((shn[-1, :, 0, :], _state_out(s_rwkv, B, rep, True), _state_out(s_hgrn, B, rep, False),
                          s_ret, lat, kr[:, :, :ROPE_D]))
    stacked = [jnp.stack([st[j] for st in per_layer]) for j in range(6)]
    return h, stacked


def kernel(x_prompt, x_sample, state_rwkv_shift, state_rwkv, state_hgrn, cache_mla_latent, cache_mla_krope, state_ret, norm_g, w_in, rwkv_mu, rwkv_w0, rwkv_w_up, rwkv_a0, rwkv_a_up, rwkv_k_k, rwkv_k_a, rwkv_r_k, rwkv_ln_w, rwkv_ln_b, hgrn_lb_logits, hgrn_norm_g, mla_q_norm_g, mla_w_q_up, mla_kv_norm_g, mla_w_kv_up, ret_norm_g, w_branch, w_out, final_norm_g):
    p = dict(norm_g=norm_g, w_in=w_in, rwkv_mu=rwkv_mu, rwkv_w0=rwkv_w0, rwkv_w_up=rwkv_w_up, rwkv_a0=rwkv_a0,
             rwkv_a_up=rwkv_a_up, rwkv_k_k=rwkv_k_k, rwkv_k_a=rwkv_k_a, rwkv_r_k=rwkv_r_k, rwkv_ln_w=rwkv_ln_w,
             rwkv_ln_b=rwkv_ln_b, hgrn_lb_logits=hgrn_lb_logits, hgrn_norm_g=hgrn_norm_g, mla_q_norm_g=mla_q_norm_g,
             mla_w_q_up=mla_w_q_up, mla_kv_norm_g=mla_kv_norm_g, mla_w_kv_up=mla_w_kv_up, ret_norm_g=ret_norm_g,
             w_branch=w_branch, w_out=w_out, final_norm_g=final_norm_g)
    weights = _all_weights(p)
    consts = _constants()
    bp, dt = x_prompt.shape[0], x_prompt.dtype
    zs = jnp.zeros((DEPTH, bp, N_HEADS, HEAD_DIM, HEAD_DIM), dt)
    y_p, (p_shift, p_rwkv, p_hgrn, p_ret, p_lat, p_kr) = _run_trunk(
        x_prompt, jnp.zeros((DEPTH, bp, SHIFT_COLS), dt), zs, zs, zs, jnp.zeros((DEPTH, bp, 0, KV_LORA), dt),
        jnp.zeros((DEPTH, bp, 0, ROPE_D), dt), p, weights, consts)
    y_s, (s_shift, s_rwkv, s_hgrn, s_ret, s_lat, s_kr) = _run_trunk(
        x_sample, state_rwkv_shift, state_rwkv, state_hgrn, state_ret, cache_mla_latent, cache_mla_krope,
        p, weights, consts)
    return (y_p, y_s, p_shift, s_shift, p_rwkv, s_rwkv, p_hgrn, s_hgrn, p_lat, s_lat, p_kr, s_kr, p_ret, s_ret)
```

```python
import functools

import jax
import jax.numpy as jnp
import numpy as np
from jax import lax
from jax.experimental import pallas as pl
from jax.experimental.pallas import tpu as pltpu

F32 = jnp.float32
BF16 = jnp.bfloat16

D_MODEL = 1024
DEPTH = 4
N_HEADS = 4
HEAD_DIM = 64
BW = N_HEADS * HEAD_DIM
N_BRANCH = 4
CHUNK = 64
SHIFT_COLS = 3 * BW + 64 + 64
GN_EPS = 64e-5
NORM_EPS = 1e-6
HEAD_NORM_EPS = 1e-5
ROPE_BASE = 10000.0
Q_LORA, KV_LORA, NOPE, ROPE_D, V_D = 256, 128, 64, 32, 64
QK_D = NOPE + ROPE_D
HEAD_PAD = 128
Q_SCALE = QK_D ** -0.5 * 1.4426950408889634
QP = N_HEADS * HEAD_PAD
LANES = 128
SUBLANES = 8

A0, A1 = 0, SHIFT_COLS + BW
B0, B1 = A1, A1 + 4 * BW
C0, C1 = B1, B1 + Q_LORA + KV_LORA + LANES + BW
D0, D1 = C1, C1 + 4 * BW
PROJ_COLS = D1

T_CD, T_SD, T_CQ, T_SQ, T_CK, T_SK, TAB_COLS = 0, 128, 256, 384, 512, 640, 768

(R_NORM, R_MU, R_W0, R_A0, R_KK, R_KA, R_RK, R_LB, R_QG, R_KVG,
 R_LNW, R_LNB, R_HG, R_RG, R_FG, N_ROWS) = (0, 1, 2, 3, 4, 5, 6, 7, 11, 12, 13, 14, 15, 16, 17, 24)

VMEM_LIMIT = 56 * 1024 * 1024


def _cparams(sem):
    return pltpu.CompilerParams(dimension_semantics=sem, vmem_limit_bytes=VMEM_LIMIT)


def _sigmoid(x):
    return 1.0 / (1.0 + jnp.exp(-x))


def _silu(x):
    return x * _sigmoid(x)


def _softplus(x):
    return jnp.maximum(x, 0.0) + jnp.log(1.0 + jnp.exp(-jnp.abs(x)))


def _dot(a, b):
    return jnp.dot(a, b, preferred_element_type=F32)


def _split(x):
    hi = x.astype(BF16)
    lo = (x - hi.astype(F32)).astype(BF16)
    return hi, lo


def _dot_lhs2(x, m_bf16):
    hi, lo = _split(x)
    return _dot(hi, m_bf16) + _dot(lo, m_bf16)


def _dot3(x, m_hi, m_lo):
    hi, lo = _split(x)
    return _dot(hi, m_hi) + _dot(lo, m_hi) + _dot(hi, m_lo)


def _rms(x, g, eps):
    return x * lax.rsqrt(jnp.mean(x * x, axis=-1, keepdims=True) + eps) * g


def _rope(x, cos, sin_signed, half, group, start=0):
    w = x.shape[-1]
    lane = lax.broadcasted_iota(jnp.int32, (1, w), 1) % group
    up = pltpu.roll(x, w - half, axis=1)
    dn = pltpu.roll(x, half, axis=1)
    return x * cos + jnp.where(lane < start + half, up, dn) * sin_signed


def _proj_kernel(layer, tt, h_ref, sh0_ref, tab_ref, par_ref, w_ref, wl_hi_ref, wl_lo_ref, seg_ref, wq_ref,
                 ar_ref, aw_ref, ak_ref, av_ref, akk_ref, ab_ref, posta_ref,
                 bq_ref, bf_ref, bv_ref, gb_ref,
                 qp_ref, lat_ref, kr_ref, gc_ref,
                 dq_ref, dk_ref, dv_ref, gd_ref, shn_ref, carry_ref):
    t = pl.program_id(0)
    b = pl.program_id(1)
    par = lambda r, n: par_ref[r:r + 1, 0:n]
    seg = seg_ref[...]

    u = _rms(h_ref[0], par(R_NORM, D_MODEL), NORM_EPS).astype(BF16)

    pa_all = _dot(u, w_ref[:, A0:A1])
    pa = pa_all[:, :SHIFT_COLS]
    prev_row = jnp.where(t == 0, sh0_ref[b], carry_ref[b])
    row = lax.broadcasted_iota(jnp.int32, (tt, 1), 0)
    p_prev = jnp.where(row == 0, prev_row, pltpu.roll(pa, 1, axis=0))
    last = pa[tt - 1:tt, :]
    carry_ref[b] = last
    shn_ref[0, 0] = last
    xs = pa + (p_prev - pa) * par(R_MU, SHIFT_COLS)
    r, k, v, wdad = xs[:, :BW], xs[:, BW:2 * BW], xs[:, 2 * BW:3 * BW], xs[:, 3 * BW:]
    lane = lax.broadcasted_iota(jnp.int32, (1, LANES), 1)
    lora = _dot3(jnp.where(lane < 64, jnp.tanh(wdad), wdad), wl_hi_ref[...], wl_lo_ref[...])
    w_raw = -_softplus(-(par(R_W0, BW) + lora[:, :BW])) - 0.5
    a = _sigmoid(par(R_A0, BW) + lora[:, BW:])
    kk = k * par(R_KK, BW)
    kk = kk / jnp.maximum(jnp.sqrt(_dot_lhs2(kk * kk, seg)), 1e-12)
    k2 = k * (1.0 + (a - 1.0) * par(R_KA, BW))
    ar_ref[...] = r
    aw_ref[...] = jnp.exp(-jnp.exp(w_raw))
    ak_ref[...] = k2
    av_ref[...] = v
    akk_ref[...] = kk
    ab_ref[...] = kk * a
    posta_ref[:, :BW] = _dot_lhs2(r * k2 * par(R_RK, BW), seg) * v
    posta_ref[:, BW:] = _silu(pa_all[:, SHIFT_COLS:])

    pb = _dot(u, w_ref[:, B0:B1])
    lg = par_ref[R_LB:R_LB + DEPTH, 0:BW]
    e = jnp.exp(lg - jnp.max(lg, axis=0, keepdims=True))
    lb = jnp.zeros((1, BW), F32)
    for j in range(1, layer + 1):
        lb = lb + e[j:j + 1]
    lb = lb / jnp.sum(e, axis=0, keepdims=True)
    z = pb[:, BW:2 * BW]
    bq_ref[...] = _silu(pb[:, :BW]) * (HEAD_DIM ** -0.5)
    bf_ref[...] = lb + (1.0 - lb) * _sigmoid(z)
    bv_ref[...] = pb[:, 2 * BW:3 * BW]
    gb_ref[...] = _silu(pb[:, 3 * BW:])

    pc = _dot(u, w_ref[:, C0:C1])
    qn = _rms(pc[:, :Q_LORA], par(R_QG, Q_LORA), NORM_EPS).astype(BF16)
    q = _dot(qn, wq_ref[...]) * Q_SCALE
    tab = lambda c, n: jnp.concatenate([tab_ref[:, c:c + LANES]] * n, axis=1)
    qp_ref[0] = _rope(q, tab(T_CQ, N_HEADS), tab(T_SQ, N_HEADS), ROPE_D // 2, HEAD_PAD, NOPE).astype(BF16)
    lat_ref[0] = _rms(pc[:, Q_LORA:Q_LORA + KV_LORA], par(R_KVG, KV_LORA), NORM_EPS)
    kr_ref[0] = _rope(pc[:, Q_LORA + KV_LORA:Q_LORA + KV_LORA + LANES], tab_ref[:, T_CK:T_CK + LANES],
                      tab_ref[:, T_SK:T_SK + LANES], ROPE_D // 2, LANES)
    gc_ref[0] = _silu(pc[:, Q_LORA + KV_LORA + LANES:])

    pd = _dot(u, w_ref[:, D0:D1])
    cos, sin = tab(T_CD, BW // LANES), tab(T_SD, BW // LANES)
    dq_ref[0] = _rope(pd[:, :BW], cos, sin, HEAD_DIM // 2, HEAD_DIM)
    dk_ref[0] = _rope(pd[:, BW:2 * BW], cos, sin, HEAD_DIM // 2, HEAD_DIM) * (HEAD_DIM ** -0.5)
    dv_ref[0] = pd[:, 2 * BW:3 * BW]
    gd_ref[0] = _silu(pd[:, 3 * BW:])


def _proj(layer, h, shift0, tab, par, w_rest, wl_hi, wl_lo, seg, wq):
    B, T, _ = h.shape
    tt = min(512, T)
    nt = T // tt
    bspec = lambda c: pl.BlockSpec((1, tt, c), lambda t, b: (b, t, 0))
    tspec = lambda c: pl.BlockSpec((tt, c), lambda t, b: (t, b))
    full = lambda a: pl.BlockSpec(a.shape, lambda t, b: (0,) * a.ndim)
    bm = lambda c, dt=F32: jax.ShapeDtypeStruct((B, T, c), dt)
    tm = lambda c: jax.ShapeDtypeStruct((T, B * c), F32)
    outs = ([(tm(BW), tspec(BW))] * 6 + [(tm(2 * BW), tspec(2 * BW))]
            + [(tm(BW), tspec(BW))] * 4
            + [(bm(QP, BF16), bspec(QP)), (bm(KV_LORA), bspec(KV_LORA)), (bm(LANES), bspec(LANES)),
               (bm(BW), bspec(BW))]
            + [(bm(BW), bspec(BW))] * 4
            + [(jax.ShapeDtypeStruct((nt, B, 1, SHIFT_COLS), F32),
                pl.BlockSpec((1, 1, 1, SHIFT_COLS), lambda t, b: (t, b, 0, 0)))])
    return pl.pallas_call(
        functools.partial(_proj_kernel, layer, tt),
        grid=(nt, B),
        in_specs=[bspec(D_MODEL), full(shift0), pl.BlockSpec((tt, TAB_COLS), lambda t, b: (t, 0)), full(par),
                  full(w_rest), full(wl_hi), full(wl_lo), full(seg), full(wq)],
        out_specs=tuple(o[1] for o in outs),
        out_shape=tuple(o[0] for o in outs),
        scratch_shapes=[pltpu.VMEM((B, 1, SHIFT_COLS), F32)],
        compiler_params=_cparams(("arbitrary", "arbitrary")),
        name="proj",
    )(h, shift0, tab, par, w_rest, wl_hi, wl_lo, seg, wq)


N_ACC = 1


N_COL = 8
LOOKAHEAD = 2
N_SLOTS = 4
SLOT_PAD = SUBLANES


def _seq_kernel(tc, bh, r_ref, w_ref, k_ref, kk_ref, b_ref, va_ref, q_ref, f_ref, vb_ref, sa0_ref, sb0_ref,
                ya_ref, yb_ref, sa_out_ref, sb_out_ref, st_a, st_b, *ring):
    slots, vslots, yslots = ring[0:N_SLOTS], ring[N_SLOTS:2 * N_SLOTS], ring[2 * N_SLOTS:3 * N_SLOTS]
    g_sz = bh // 2
    rep = LANES // bh
    vr = HEAD_DIM // rep
    n_pairs = tc // 2
    lane = lax.broadcasted_iota(jnp.int32, (1, LANES), 1)
    low = lane < HEAD_DIM
    hi_id = lane // bh
    grid_step = pl.program_id(0)

    @pl.when(grid_step == 0)
    def _():
        st_a[...] = sa0_ref[...]
        st_b[...] = sb0_ref[...]

    def paired(ref, p):
        a, b = ref[2 * p], ref[2 * p + 1]
        s0 = jnp.where(low, a, pltpu.roll(b, HEAD_DIM, axis=1))
        s1 = jnp.where(low, pltpu.roll(a, HEAD_DIM, axis=1), b)
        return jnp.concatenate([s0, s1] * rep, axis=0).T

    def prepare(p, slot, vslot):
        for o, ref in enumerate((r_ref, w_ref, k_ref, kk_ref, b_ref, q_ref, f_ref)):
            xt = paired(ref, p)
            slot[o, 0:2 * HEAD_DIM] = xt
            if ref is f_ref:
                slot[o + 1, 0:2 * HEAD_DIM] = 1.0 - xt
        for ri, ref in enumerate((va_ref, vb_ref)):
            xt = paired(ref, p)
            for s in range(2):
                vv = xt[s * HEAD_DIM:s * HEAD_DIM + vr]
                for g in range(1, rep):
                    vv = jnp.where(hi_id == g, xt[s * HEAD_DIM + g * vr:s * HEAD_DIM + (g + 1) * vr], vv)
                vslot[ri, s] = vv

    def emit(p, yslot):
        for ri, y_ref in enumerate((ya_ref, yb_ref)):
            blocks = [jnp.where(hi_id == g, yslot[ri, s], 0.0) for s in range(2) for g in range(rep)]
            zt = jnp.concatenate(blocks, axis=0).T
            yp = zt[0:bh]
            for g in range(1, rep):
                yp = yp + zt[g * bh:(g + 1) * bh]
            p0, p1 = yp[:g_sz], yp[g_sz:]
            y_ref[2 * p] = jnp.where(low, p0, pltpu.roll(p1, HEAD_DIM, axis=1))
            y_ref[2 * p + 1] = jnp.where(low, pltpu.roll(p0, HEAD_DIM, axis=1), p1)

    def tree(xs):
        while len(xs) > 1:
            xs = [xs[i] + xs[i + 1] for i in range(0, len(xs), 2)]
        return xs[0]

    def bc(slot, o, row):
        return jnp.broadcast_to(slot[o, pl.ds(row, 1), :], (vr, LANES))

    def accumulate(acc, k, p):
        acc[k % N_ACC] = p if acc[k % N_ACC] is None else acc[k % N_ACC] + p

    def first_sa(slot):
        acc = [None] * N_ACC
        for k in range(HEAD_DIM):
            accumulate(acc, k, st_a[k] * bc(slot, 3, k))
        return -tree(acc)

    def rwkv_step(s, slot, kk_slot, kk_row, vslot, yslot, sa):
        vv = vslot[0, s]
        acc_y, acc_s = [None] * N_ACC, [None] * N_ACC
        for k in range(HEAD_DIM):
            row = s * HEAD_DIM + k
            new = st_a[k] * bc(slot, 1, row) + sa * bc(slot, 4, row) + vv * bc(slot, 2, row)
            st_a[k] = new
            accumulate(acc_y, k, new * bc(slot, 0, row))
            accumulate(acc_s, k, new * bc(kk_slot, 3, kk_row + k))
        yslot[0, s] = tree(acc_y)
        return -tree(acc_s)

    def hgrn_pair(slot, vslot, yslot):
        v0, v1 = vslot[1, 0], vslot[1, 1]
        acc0, acc1 = [None] * N_ACC, [None] * N_ACC
        for k in range(HEAD_DIM):
            mid = st_b[k] * bc(slot, 6, k) + v0 * bc(slot, 7, k)
            accumulate(acc0, k, mid * bc(slot, 5, k))
            new = mid * bc(slot, 6, HEAD_DIM + k) + v1 * bc(slot, 7, HEAD_DIM + k)
            st_b[k] = new
            accumulate(acc1, k, new * bc(slot, 5, HEAD_DIM + k))
        yslot[1, 0] = tree(acc0)
        yslot[1, 1] = tree(acc1)

    def pair_of_steps(p, j, sa):
        emit(jnp.maximum(p - 1, 0), yslots[(j - 1) % N_SLOTS])
        nxt = (j + LOOKAHEAD) % N_SLOTS
        prepare(jnp.minimum(p + LOOKAHEAD, n_pairs - 1), slots[nxt], vslots[nxt])
        sa = rwkv_step(0, slots[j], slots[j], HEAD_DIM, vslots[j], yslots[j], sa)
        sa = rwkv_step(1, slots[j], slots[(j + 1) % N_SLOTS], 0, vslots[j], yslots[j], sa)
        hgrn_pair(slots[j], vslots[j], yslots[j])
        return sa

    for j in range(LOOKAHEAD):
        prepare(j, slots[j], vslots[j])
    yslots[N_SLOTS - 1][...] = jnp.zeros(yslots[N_SLOTS - 1].shape, F32)

    def body(i, sa):
        for j in range(N_SLOTS):
            sa = pair_of_steps(N_SLOTS * i + j, j, sa)
        return sa

    lax.fori_loop(0, n_pairs // N_SLOTS, body, first_sa(slots[0]))
    emit(n_pairs - 1, yslots[N_SLOTS - 1])

    @pl.when(grid_step == pl.num_programs(0) - 1)
    def _():
        sa_out_ref[...] = st_a[...]
        sb_out_ref[...] = st_b[...]


def _seq(cols_a, va, cols_b, vb, sa0, sb0, bh):
    T = va.shape[0]
    g_sz = bh // 2
    vr = sa0.shape[1]
    tc = min(128, T)
    assert tc % (2 * N_SLOTS) == 0
    view = lambda a: a.reshape(T, g_sz, LANES)
    ospec = pl.BlockSpec((tc, g_sz, LANES), lambda g: (g, 0, 0))
    sspec = pl.BlockSpec((HEAD_DIM, vr, LANES), lambda g: (0, 0, 0))
    args = [view(c) for c in (*cols_a, va, *cols_b, vb)] + [sa0, sb0]
    y_shape = jax.ShapeDtypeStruct((T, g_sz, LANES), F32)
    s_shape = jax.ShapeDtypeStruct((HEAD_DIM, vr, LANES), F32)
    slot = pltpu.VMEM((N_COL, 2 * HEAD_DIM + SLOT_PAD, LANES), F32)
    small = pltpu.VMEM((2, 2, vr, LANES), F32)
    ya, yb, sa, sb = pl.pallas_call(
        functools.partial(_seq_kernel, tc, bh),
        grid=(T // tc,),
        in_specs=[ospec] * (len(args) - 2) + [sspec, sspec],
        out_specs=(ospec, ospec, sspec, sspec),
        out_shape=(y_shape, y_shape, s_shape, s_shape),
        scratch_shapes=[pltpu.VMEM((HEAD_DIM, vr, LANES), F32), pltpu.VMEM((HEAD_DIM, vr, LANES), F32),
                        *([slot] * N_SLOTS + [small] * (2 * N_SLOTS))],
        compiler_params=_cparams(("arbitrary",)),
        name="seq",
    )(*args)
    return ya.reshape(T, bh * HEAD_DIM), yb.reshape(T, bh * HEAD_DIM), sa, sb


def _ret_kernel(L, q_ref, k_ref, v_ref, s0_ref, o_ref, s_out_ref, st, dm, gq, gk):
    i = pl.program_id(1)
    log2_gamma = [float(np.log2(1.0 - 2.0 ** (-5.0 - h))) for h in range(N_HEADS)]

    @pl.when(i == 0)
    def _():
        st[...] = s0_ref[0]
        t_idx = lax.broadcasted_iota(jnp.int32, (L, L), 0)
        s_idx = lax.broadcasted_iota(jnp.int32, (L, L), 1)
        row = lax.broadcasted_iota(jnp.int32, (L, HEAD_DIM), 0).astype(F32)
        for h in range(N_HEADS):
            d = (t_idx - s_idx).astype(F32)
            dm[h] = jnp.where(t_idx >= s_idx, jnp.exp2(d * log2_gamma[h]), 0.0)
            gq[h] = jnp.exp2((row + 1.0) * log2_gamma[h])
            gk[h] = jnp.exp2((L - 1.0 - row) * log2_gamma[h])

    for h in range(N_HEADS):
        hs = slice(h * HEAD_DIM, (h + 1) * HEAD_DIM)
        q, k, v = q_ref[0, :, hs], k_ref[0, :, hs], v_ref[0, :, hs].astype(BF16)
        a = lax.dot_general(q.astype(BF16), k.astype(BF16), (((1,), (1,)), ((), ())),
                            preferred_element_type=F32) * dm[h]
        s_prev = st[h]
        o_ref[0, :, hs] = _dot(a.astype(BF16), v) + _dot((q * gq[h]).astype(BF16), s_prev.astype(BF16))
        kd = (k * gk[h]).astype(BF16)
        st[h] = s_prev * (2.0 ** (L * log2_gamma[h])) + lax.dot_general(
            kd, v, (((0,), (0,)), ((), ())), preferred_element_type=F32)

    @pl.when(i == pl.num_programs(1) - 1)
    def _():
        s_out_ref[0] = st[...]


def _ret(q, k, v, s0):
    B, T, _ = q.shape
    L = min(512, T)
    tspec = pl.BlockSpec((1, L, BW), lambda b, i: (b, i, 0))
    sspec = pl.BlockSpec((1, N_HEADS, HEAD_DIM, HEAD_DIM), lambda b, i: (b, 0, 0, 0))
    return pl.pallas_call(
        functools.partial(_ret_kernel, L),
        grid=(B, T // L),
        in_specs=[tspec, tspec, tspec, sspec],
        out_specs=(tspec, sspec),
        out_shape=(jax.ShapeDtypeStruct((B, T, BW), F32),
                   jax.ShapeDtypeStruct((B, N_HEADS, HEAD_DIM, HEAD_DIM), F32)),
        scratch_shapes=[pltpu.VMEM((N_HEADS, HEAD_DIM, HEAD_DIM), F32), pltpu.VMEM((N_HEADS, L, L), F32),
                        pltpu.VMEM((N_HEADS, L, HEAD_DIM), F32), pltpu.VMEM((N_HEADS, L, HEAD_DIM), F32)],
        compiler_params=_cparams(("arbitrary", "arbitrary")),
        name="ret",
    )(q, k, v, s0)


def _keys_values(lat, kr, wkt_ref, wv_ref, placet):
    lat = lat.astype(BF16)
    nt = (((1,), (1,)), ((), ()))
    kt = (lax.dot_general(wkt_ref[...], lat, nt, preferred_element_type=F32)
          + lax.dot_general(placet, kr.astype(BF16), nt, preferred_element_type=F32)).astype(BF16)
    lane = lax.broadcasted_iota(jnp.int32, (1, QP), 1) % HEAD_PAD
    return kt, jnp.where(lane == V_D, 1.0, _dot(lat, wv_ref[...])).astype(BF16)


def _kvprep_kernel(lat_ref, kr_ref, wkt_ref, wv_ref, placet_ref, kt_ref, v_ref):
    kt_ref[0], v_ref[0] = _keys_values(lat_ref[0], kr_ref[0], wkt_ref, wv_ref, placet_ref[...])


def _kvprep(lat, kr, wkt, wv, placet, tr):
    B, tk, _ = lat.shape
    rspec = lambda c: pl.BlockSpec((1, tr, c), lambda b, i: (b, i, 0))
    full = lambda a: pl.BlockSpec(a.shape, lambda b, i: (0,) * a.ndim)
    return pl.pallas_call(
        _kvprep_kernel,
        grid=(B, tk // tr),
        in_specs=[rspec(KV_LORA), rspec(LANES), full(wkt), full(wv), full(placet)],
        out_specs=(pl.BlockSpec((1, QP, tr), lambda b, i: (b, 0, i)), rspec(QP)),
        out_shape=(jax.ShapeDtypeStruct((B, QP, tk), BF16), jax.ShapeDtypeStruct((B, tk, QP), BF16)),
        compiler_params=_cparams(("arbitrary", "arbitrary")),
        name="kvprep",
    )(lat, kr, wkt, wv, placet)


def _kvprep_cached_kernel(past, T, plat_ref, pkr_ref, lat_ref, kr_ref, wkt_ref, wv_ref, placet_ref, kt_ref, v_ref):
    tk = past + T
    kt_ref[0, :, 0:past], v_ref[0, 0:past, :] = _keys_values(plat_ref[0], pkr_ref[0], wkt_ref, wv_ref,
                                                             placet_ref[:, 0:ROPE_D])
    kt_ref[0, :, past:tk], v_ref[0, past:tk, :] = _keys_values(lat_ref[0], kr_ref[0], wkt_ref, wv_ref, placet_ref[...])
    pad = kt_ref.shape[2] - tk
    if pad:
        kt_ref[0, :, tk:] = jnp.zeros((QP, pad), BF16)
        v_ref[0, tk:, :] = jnp.zeros((pad, QP), BF16)


def _kvprep_cached(past_lat, past_kr, lat, kr, wkt, wv, placet, tk_pad):
    B, past, _ = past_lat.shape
    T = lat.shape[1]
    whole = lambda a: pl.BlockSpec((1,) + a.shape[1:], lambda b: (b, 0, 0))
    full = lambda a: pl.BlockSpec(a.shape, lambda b: (0,) * a.ndim)
    return pl.pallas_call(
        functools.partial(_kvprep_cached_kernel, past, T),
        grid=(B,),
        in_specs=[whole(past_lat), whole(past_kr), whole(lat), whole(kr), full(wkt), full(wv), full(placet)],
        out_specs=(pl.BlockSpec((1, QP, tk_pad), lambda b: (b, 0, 0)), pl.BlockSpec((1, tk_pad, QP), lambda b: (b, 0, 0))),
        out_shape=(jax.ShapeDtypeStruct((B, QP, tk_pad), BF16), jax.ShapeDtypeStruct((B, tk_pad, QP), BF16)),
        compiler_params=_cparams(("arbitrary",)),
        name="kvprep_cached",
    )(past_lat, past_kr, lat, kr, wkt, wv, placet)


A_FIRST, A_LAST, A_MASKED = 1, 2, 4


def _attn_kernel(bq, bk, q0, kv_len, qi_ref, kj_ref, flag_ref, q_ref, kt_ref, v_ref, o_ref, m_ref, acc_ref):
    step = pl.program_id(1)
    i = qi_ref[step]
    j = kj_ref[step]
    flags = flag_ref[step]

    @pl.when((flags & A_FIRST) != 0)
    def _():
        m_ref[...] = jnp.full(m_ref.shape, -jnp.inf, F32)
        acc_ref[...] = jnp.zeros(acc_ref.shape, F32)

    def block(masked):
        if masked:
            qi = q0 + i * bq + lax.broadcasted_iota(jnp.int32, (bq, bk), 0)
            ki = j * bk + lax.broadcasted_iota(jnp.int32, (bq, bk), 1)
            keep = jnp.logical_and(ki // CHUNK <= qi // CHUNK, ki < kv_len)
        for h in range(N_HEADS):
            hs = slice(h * HEAD_PAD, (h + 1) * HEAD_PAD)
            s = _dot(q_ref[0, :, hs], kt_ref[0, hs, :])
            if masked:
                s = jnp.where(keep, s, -jnp.inf)
            m_old = m_ref[h]
            m_new = jnp.maximum(m_old, jnp.max(s, axis=-1, keepdims=True))
            alpha = jnp.exp2(m_old - m_new)
            m_full = jnp.concatenate([m_new] * (bk // LANES), axis=1) if bk % LANES == 0 else m_new[:, 0:1]
            p = jnp.exp2(s - m_full)
            acc_ref[:, hs] = alpha * acc_ref[:, hs] + _dot(p.astype(BF16), v_ref[0, :, hs])
            m_ref[h] = m_new

    @pl.when((flags & A_MASKED) == 0)
    def _():
        block(False)

    @pl.when((flags & A_MASKED) != 0)
    def _():
        block(True)

    @pl.when((flags & A_LAST) != 0)
    def _():
        for h in range(N_HEADS):
            a = acc_ref[:, h * HEAD_PAD:(h + 1) * HEAD_PAD]
            o_ref[0, :, h * V_D:(h + 1) * V_D] = a[:, :V_D] / a[:, V_D:V_D + 1]


def _attn(qp, ktp, vp, q0, kv_len, bk):
    B, T, _ = qp.shape
    tk = vp.shape[1]
    bq = min(512, T)
    nq, nk = T // bq, tk // bk
    qi, kj, flags = [], [], []
    for i in range(nq):
        first_q, last_q = q0 + i * bq, q0 + (i + 1) * bq - 1
        n_blocks = min(nk, (min(kv_len, (last_q // CHUNK + 1) * CHUNK) - 1) // bk + 1)
        for j in range(n_blocks):
            fully_visible = (j + 1) * bk <= min(kv_len, (first_q // CHUNK + 1) * CHUNK)
            qi.append(i)
            kj.append(j)
            flags.append((A_FIRST if j == 0 else 0) | (A_LAST if j == n_blocks - 1 else 0)
                         | (0 if fully_visible else A_MASKED))
    sched = [jnp.asarray(np.asarray(a, np.int32)) for a in (qi, kj, flags)]
    grid_spec = pltpu.PrefetchScalarGridSpec(
        num_scalar_prefetch=3,
        grid=(B, len(qi)),
        in_specs=[pl.BlockSpec((1, bq, QP), lambda b, s, qi, kj, fl: (b, qi[s], 0)),
                  pl.BlockSpec((1, QP, bk), lambda b, s, qi, kj, fl: (b, 0, kj[s])),
                  pl.BlockSpec((1, bk, QP), lambda b, s, qi, kj, fl: (b, kj[s], 0))],
        out_specs=pl.BlockSpec((1, bq, BW), lambda b, s, qi, kj, fl: (b, qi[s], 0)),
        scratch_shapes=[pltpu.VMEM((N_HEADS, bq, LANES), F32), pltpu.VMEM((bq, QP), F32)])
    return pl.pallas_call(
        functools.partial(_attn_kernel, bq, bk, q0, kv_len),
        grid_spec=grid_spec,
        out_shape=jax.ShapeDtypeStruct((B, T, BW), F32),
        compiler_params=_cparams(("arbitrary", "arbitrary")),
        name="attn",
    )(*sched, qp, ktp, vp)


def _merge_kernel(final, h_ref, ya_ref, posta_ref, ob_ref, gb_ref, oc_ref, gc_ref, od_ref, gd_ref,
                  par_ref, seg_ref, wg_ref, wb_ref, wo_ref, out_ref):
    par = lambda r, n: par_ref[r:r + 1, 0:n]
    seg = seg_ref[...]
    h = h_ref[0]
    u = _rms(h, par(R_NORM, D_MODEL), NORM_EPS).astype(BF16)
    inv = 1.0 / HEAD_DIM

    ya = ya_ref[...]
    xc = ya - _dot_lhs2(ya, seg) * inv
    yn = xc * lax.rsqrt(_dot_lhs2(xc * xc, seg) * inv + GN_EPS) * par(R_LNW, BW) + par(R_LNB, BW)
    oa = (yn + posta_ref[:, :BW]) * posta_ref[:, BW:]

    def head_rms(o, g):
        return o * lax.rsqrt(_dot_lhs2(o * o, seg) * inv + HEAD_NORM_EPS) * g

    ob = head_rms(ob_ref[...], par(R_HG, BW)) * gb_ref[...]
    oc = oc_ref[0] * gc_ref[0]
    od = head_rms(od_ref[0], par(R_RG, BW)) * gd_ref[0]

    merged = None
    for n, o in enumerate((oa, ob, oc, od)):
        gate = _sigmoid(_dot(u, wg_ref[:, n * D_MODEL:(n + 1) * D_MODEL]))
        term = gate * _dot(o.astype(BF16), wb_ref[n * BW:(n + 1) * BW, :])
        merged = term if merged is None else merged + term
    out = h + _dot(merged.astype(BF16), wo_ref[...])
    if final:
        out = _rms(out, par(R_FG, D_MODEL), NORM_EPS)
    out_ref[0] = out


def _merge(final, h, ya, posta, ob, gb, oc, gc, od, gd, par, seg, wg, wb, wo):
    B, T, _ = h.shape
    tt = min(512, T)
    bspec = lambda c: pl.BlockSpec((1, tt, c), lambda t, b: (b, t, 0))
    tspec = lambda c: pl.BlockSpec((tt, c), lambda t, b: (t, b))
    full = lambda a: pl.BlockSpec(a.shape, lambda t, b: (0,) * a.ndim)
    consts = (par, seg, wg, wb, wo)
    return pl.pallas_call(
        functools.partial(_merge_kernel, final),
        grid=(T // tt, B),
        in_specs=[bspec(D_MODEL), tspec(BW), tspec(2 * BW), tspec(BW), tspec(BW), bspec(BW), bspec(BW), bspec(BW),
                  bspec(BW)] + [full(a) for a in consts],
        out_specs=bspec(D_MODEL),
        out_shape=jax.ShapeDtypeStruct((B, T, D_MODEL), F32),
        compiler_params=_cparams(("arbitrary", "arbitrary")),
        name="merge",
    )(h, ya, posta, ob, gb, oc, gc, od, gd, *consts)


def _rope_tables(past, T):
    pos = (past + jnp.arange(T, dtype=jnp.int32)).astype(F32)[:, None]

    def tables(width, group, start, d):
        lane = np.arange(width) % group - start
        on = (lane >= 0) & (lane < d)
        idx = np.where(on, lane % (d // 2), 0)
        inv_freq = jnp.power(ROPE_BASE, -jnp.arange(0, d, 2, dtype=F32) / d)
        ang = pos * inv_freq[None, :]
        cos = jnp.where(on[None, :], jnp.cos(ang)[:, idx], 1.0)
        sign = np.where(lane < d // 2, -1.0, 1.0).astype(np.float32)
        sin = jnp.where(on[None, :], jnp.sin(ang)[:, idx] * sign[None, :], 0.0)
        return cos, sin

    cd, sd = tables(LANES, HEAD_DIM, 0, HEAD_DIM)
    cq, sq = tables(LANES, HEAD_PAD, NOPE, ROPE_D)
    ck, sk = tables(LANES, LANES, 0, ROPE_D)
    return jnp.concatenate([cd, sd, cq, sq, ck, sk], axis=1)


def _layer_weights(l, p):
    w_in = p["w_in"][l]
    gate_cols = N_BRANCH * D_MODEL
    c_kr = gate_cols + SHIFT_COLS + BW + 4 * BW + Q_LORA + KV_LORA
    c_g = c_kr + ROPE_D
    w_rest = jnp.concatenate([
        w_in[:, gate_cols:c_kr],
        jnp.pad(w_in[:, c_kr:c_g], ((0, 0), (0, LANES - ROPE_D))),
        w_in[:, c_g:],
    ], axis=1).astype(BF16)
    wg = w_in[:, :gate_cols].astype(BF16)

    z = jnp.zeros((64, BW), F32)
    wl = jnp.concatenate([jnp.concatenate([p["rwkv_w_up"][l], z], axis=1),
                          jnp.concatenate([z, p["rwkv_a_up"][l]], axis=1)], axis=0)
    wl_hi = wl.astype(BF16)
    wl_lo = (wl - wl_hi.astype(F32)).astype(BF16)

    wq = p["mla_w_q_up"][l].reshape(Q_LORA, N_HEADS, QK_D)
    wq = jnp.pad(wq, ((0, 0), (0, 0), (0, HEAD_PAD - QK_D))).reshape(Q_LORA, QP).astype(BF16)
    wkv = p["mla_w_kv_up"][l].reshape(KV_LORA, N_HEADS, NOPE + V_D)
    wkt = jnp.pad(wkv[:, :, :NOPE], ((0, 0), (0, 0), (0, HEAD_PAD - NOPE))).reshape(KV_LORA, QP).T.astype(BF16)
    wv = jnp.pad(wkv[:, :, NOPE:], ((0, 0), (0, 0), (0, HEAD_PAD - V_D))).reshape(KV_LORA, QP).astype(BF16)

    wb = p["w_branch"][l].reshape(N_BRANCH * BW, D_MODEL).astype(BF16)
    wo = p["w_out"][l].astype(BF16)

    return dict(w_rest=w_rest, wg=wg, wl_hi=wl_hi, wl_lo=wl_lo, wq=wq, wkt=wkt, wv=wv, wb=wb, wo=wo)


def _param_tables(p):
    def rows(v, n=1):
        v = v.reshape(DEPTH, n, -1).astype(F32)
        return jnp.pad(v, ((0, 0), (0, 0), (0, D_MODEL - v.shape[-1])))

    every = lambda v, n: jnp.broadcast_to(v.reshape(1, n, -1), (DEPTH, n, v.shape[-1]))
    parts = [rows(p["norm_g"]), rows(p["rwkv_mu"]), rows(p["rwkv_w0"]), rows(p["rwkv_a0"]), rows(p["rwkv_k_k"]),
             rows(p["rwkv_k_a"]), rows(p["rwkv_r_k"]), rows(every(p["hgrn_lb_logits"], DEPTH), DEPTH),
             rows(p["mla_q_norm_g"]), rows(p["mla_kv_norm_g"]), rows(p["rwkv_ln_w"]), rows(p["rwkv_ln_b"]),
             rows(p["hgrn_norm_g"]), rows(p["ret_norm_g"]), rows(every(p["final_norm_g"], 1))]
    used = sum(a.shape[1] for a in parts)
    return jnp.concatenate(parts + [jnp.zeros((DEPTH, N_ROWS - used, D_MODEL), F32)], axis=1)


def _all_weights(p):
    tables = _param_tables(p)
    return [dict(_layer_weights(l, p), par=tables[l]) for l in range(DEPTH)]


def _constants():
    head = np.arange(BW) // HEAD_DIM
    seg = jnp.asarray((head[:, None] == head[None, :]).astype(np.float32), dtype=BF16)
    place = np.zeros((LANES, QP), np.float32)
    for h in range(N_HEADS):
        for j in range(ROPE_D):
            place[j, h * HEAD_PAD + NOPE + j] = 1.0
    return seg, jnp.asarray(place.T, dtype=BF16)


def _state_in(s, B, rep, key_last):
    vr = HEAD_DIM // rep
    if key_last:
        y = s.reshape(B, 2, 2, rep, vr, HEAD_DIM).transpose(5, 4, 3, 2, 0, 1)
    else:
        y = s.reshape(B, 2, 2, HEAD_DIM, rep, vr).transpose(3, 5, 4, 2, 0, 1)
    return y.reshape(HEAD_DIM, vr, LANES).astype(F32)


def _state_out(y, B, rep, key_last):
    vr = HEAD_DIM // rep
    y = y.reshape(HEAD_DIM, vr, rep, 2, B, 2)
    if key_last:
        y = y.transpose(4, 5, 3, 2, 1, 0)
    else:
        y = y.transpose(4, 5, 3, 0, 2, 1)
    return y.reshape(B, N_HEADS, HEAD_DIM, HEAD_DIM)


def _run_trunk(x, shift0, rwkv0, hgrn0, ret0, lat0, kr0, p, weights, consts):
    B, T, _ = x.shape
    past = lat0.shape[2]
    bh = B * N_HEADS
    rep = LANES // bh
    assert rep * bh == LANES and HEAD_DIM % rep == 0 and (HEAD_DIM // rep) % SUBLANES == 0 and N_HEADS == 4
    seg, placet = consts
    tab = _rope_tables(past, T)
    h = x
    per_layer = []
    for l in range(DEPTH):
        w = weights[l]
        (ar, aw, ak, av, akk, ab, posta, bq, bf, bv, gb, qp, lat, kr, gc, dq, dk, dv, gd, shn) = _proj(
            l, h, shift0[l][:, None, :], tab, w["par"], w["w_rest"], w["wl_hi"], w["wl_lo"], seg, w["wq"])

        ya, ob, s_rwkv, s_hgrn = _seq([ar, aw, ak, akk, ab], av, [bq, bf], bv, _state_in(rwkv0[l], B, rep, True),
                                      _state_in(hgrn0[l], B, rep, False), bh)
        od, s_ret = _ret(dq, dk, dv, ret0[l].astype(F32))

        tk = past + T
        if past == 0:
            blk = min(512, T)
            ktp, vp = _kvprep(lat, kr, w["wkt"], w["wv"], placet, blk)
        else:
            blk = -(-tk // LANES) * LANES
            assert blk * QP * 2 * 4 <= VMEM_LIMIT // 4 and past % LANES == 0
            ktp, vp = _kvprep_cached(lat0[l], kr0[l], lat, kr, w["wkt"], w["wv"], placet, blk)
        oc = _attn(qp, ktp, vp, past, tk, blk)

        h = _merge(l == DEPTH - 1, h, ya, posta, ob, gb, oc, gc, od, gd, w["par"], seg, w["wg"], w["wb"], w["wo"])
        per_layer.append((shn[-1, :, 0, :], _state_out(s_rwkv, B, rep, True), _state_out(s_hgrn, B, rep, False),
                          s_ret, lat, kr[:, :, :ROPE_D]))
    stacked = [jnp.stack([st[j] for st in per_layer]) for j in range(6)]
    return h, stacked


def kernel(x_prompt, x_sample, state_rwkv_shift, state_rwkv, state_hgrn, cache_mla_latent, cache_mla_krope, state_ret, norm_g, w_in, rwkv_mu, rwkv_w0, rwkv_w_up, rwkv_a0, rwkv_a_up, rwkv_k_k, rwkv_k_a, rwkv_r_k, rwkv_ln_w, rwkv_ln_b, hgrn_lb_logits, hgrn_norm_g, mla_q_norm_g, mla_w_q_up, mla_kv_norm_g, mla_w_kv_up, ret_norm_g, w_branch, w_out, final_norm_g):
    p = dict(norm_g=norm_g, w_in=w_in, rwkv_mu=rwkv_mu, rwkv_w0=rwkv_w0, rwkv_w_up=rwkv_w_up, rwkv_a0=rwkv_a0,
             rwkv_a_up=rwkv_a_up, rwkv_k_k=rwkv_k_k, rwkv_k_a=rwkv_k_a, rwkv_r_k=rwkv_r_k, rwkv_ln_w=rwkv_ln_w,
             rwkv_ln_b=rwkv_ln_b, hgrn_lb_logits=hgrn_lb_logits, hgrn_norm_g=hgrn_norm_g, mla_q_norm_g=mla_q_norm_g,
             mla_w_q_up=mla_w_q_up, mla_kv_norm_g=mla_kv_norm_g, mla_w_kv_up=mla_w_kv_up, ret_norm_g=ret_norm_g,
             w_branch=w_branch, w_out=w_out, final_norm_g=final_norm_g)
    weights = _all_weights(p)
    consts = _constants()
    bp, dt = x_prompt.shape[0], x_prompt.dtype
    zs = jnp.zeros((DEPTH, bp, N_HEADS, HEAD_DIM, HEAD_DIM), dt)
    y_p, (p_shift, p_rwkv, p_hgrn, p_ret, p_lat, p_kr) = _run_trunk(
        x_prompt, jnp.zeros((DEPTH, bp, SHIFT_COLS), dt), zs, zs, zs, jnp.zeros((DEPTH, bp, 0, KV_LORA), dt),
        jnp.zeros((DEPTH, bp, 0, ROPE_D), dt), p, weights, consts)
    y_s, (s_shift, s_rwkv, s_hgrn, s_ret, s_lat, s_kr) = _run_trunk(
        x_sample, state_rwkv_shift, state_rwkv, state_hgrn, state_ret, cache_mla_latent, cache_mla_krope,
        p, weights, consts)
    return (y_p, y_s, p_shift, s_shift, p_rwkv, s_rwkv, p_hgrn, s_hgrn, p_lat, s_lat, p_kr, s_kr, p_ret, s_ret)
```

```python
import functools

import jax
import jax.numpy as jnp
import numpy as np
from jax import lax
from jax.experimental import pallas as pl
from jax.experimental.pallas import tpu as pltpu

F32 = jnp.float32
BF16 = jnp.bfloat16

D_MODEL = 1024
DEPTH = 4
N_HEADS = 4
HEAD_DIM = 64
BW = N_HEADS * HEAD_DIM
N_BRANCH = 4
CHUNK = 64
SHIFT_COLS = 3 * BW + 64 + 64
GN_EPS = 64e-5
NORM_EPS = 1e-6
HEAD_NORM_EPS = 1e-5
ROPE_BASE = 10000.0
Q_LORA, KV_LORA, NOPE, ROPE_D, V_D = 256, 128, 64, 32, 64
QK_D = NOPE + ROPE_D
HEAD_PAD = 128
Q_SCALE = QK_D ** -0.5 * 1.4426950408889634
QP = N_HEADS * HEAD_PAD
LANES = 128
SUBLANES = 8

A0, A1 = 0, SHIFT_COLS + BW
B0, B1 = A1, A1 + 4 * BW
C0, C1 = B1, B1 + Q_LORA + KV_LORA + LANES + BW
D0, D1 = C1, C1 + 4 * BW
PROJ_COLS = D1

T_CD, T_SD, T_CQ, T_SQ, T_CK, T_SK, TAB_COLS = 0, 128, 256, 384, 512, 640, 768

(R_NORM, R_MU, R_W0, R_A0, R_KK, R_KA, R_RK, R_LB, R_QG, R_KVG,
 R_LNW, R_LNB, R_HG, R_RG, R_FG, N_ROWS) = (0, 1, 2, 3, 4, 5, 6, 7, 11, 12, 13, 14, 15, 16, 17, 24)

VMEM_LIMIT = 56 * 1024 * 1024


def _cparams(sem):
    return pltpu.CompilerParams(dimension_semantics=sem, vmem_limit_bytes=VMEM_LIMIT)


def _sigmoid(x):
    return 1.0 / (1.0 + jnp.exp(-x))


def _silu(x):
    return x * _sigmoid(x)


def _softplus(x):
    return jnp.maximum(x, 0.0) + jnp.log(1.0 + jnp.exp(-jnp.abs(x)))


def _dot(a, b):
    return jnp.dot(a, b, preferred_element_type=F32)


def _split(x):
    hi = x.astype(BF16)
    lo = (x - hi.astype(F32)).astype(BF16)
    return hi, lo


def _dot_lhs2(x, m_bf16):
    hi, lo = _split(x)
    return _dot(hi, m_bf16) + _dot(lo, m_bf16)


def _dot3(x, m_hi, m_lo):
    hi, lo = _split(x)
    return _dot(hi, m_hi) + _dot(lo, m_hi) + _dot(hi, m_lo)


def _rms(x, g, eps):
    return x * lax.rsqrt(jnp.mean(x * x, axis=-1, keepdims=True) + eps) * g


def _rope(x, cos, sin_signed, half, group, start=0):
    w = x.shape[-1]
    lane = lax.broadcasted_iota(jnp.int32, (1, w), 1) % group
    up = pltpu.roll(x, w - half, axis=1)
    dn = pltpu.roll(x, half, axis=1)
    return x * cos + jnp.where(lane < start + half, up, dn) * sin_signed


def _proj_kernel(layer, tt, h_ref, sh0_ref, tab_ref, par_ref, w_ref, wl_hi_ref, wl_lo_ref, seg_ref, wq_ref,
                 wkt_ref, wv_ref, placet_ref,
                 ar_ref, aw_ref, ak_ref, av_ref, akk_ref, ab_ref, posta_ref,
                 bq_ref, bf_ref, bv_ref, gb_ref,
                 qp_ref, lat_ref, kr_ref, gc_ref, kt_ref, vkv_ref,
                 dq_ref, dk_ref, dv_ref, gd_ref, shn_ref, carry_ref):
    t = pl.program_id(0)
    b = pl.program_id(1)
    par = lambda r, n: par_ref[r:r + 1, 0:n]
    seg = seg_ref[...]

    u = _rms(h_ref[0], par(R_NORM, D_MODEL), NORM_EPS).astype(BF16)

    pa_all = _dot(u, w_ref[:, A0:A1])
    pa = pa_all[:, :SHIFT_COLS]
    prev_row = jnp.where(t == 0, sh0_ref[b], carry_ref[b])
    row = lax.broadcasted_iota(jnp.int32, (tt, 1), 0)
    p_prev = jnp.where(row == 0, prev_row, pltpu.roll(pa, 1, axis=0))
    last = pa[tt - 1:tt, :]
    carry_ref[b] = last
    shn_ref[0, 0] = last
    xs = pa + (p_prev - pa) * par(R_MU, SHIFT_COLS)
    r, k, v, wdad = xs[:, :BW], xs[:, BW:2 * BW], xs[:, 2 * BW:3 * BW], xs[:, 3 * BW:]
    lane = lax.broadcasted_iota(jnp.int32, (1, LANES), 1)
    lora = _dot3(jnp.where(lane < 64, jnp.tanh(wdad), wdad), wl_hi_ref[...], wl_lo_ref[...])
    w_raw = -_softplus(-(par(R_W0, BW) + lora[:, :BW])) - 0.5
    a = _sigmoid(par(R_A0, BW) + lora[:, BW:])
    kk = k * par(R_KK, BW)
    kk = kk / jnp.maximum(jnp.sqrt(_dot_lhs2(kk * kk, seg)), 1e-12)
    k2 = k * (1.0 + (a - 1.0) * par(R_KA, BW))
    ar_ref[...] = r
    aw_ref[...] = jnp.exp(-jnp.exp(w_raw))
    ak_ref[...] = k2
    av_ref[...] = v
    akk_ref[...] = kk
    ab_ref[...] = kk * a
    posta_ref[:, :BW] = _dot_lhs2(r * k2 * par(R_RK, BW), seg) * v
    posta_ref[:, BW:] = _silu(pa_all[:, SHIFT_COLS:])

    pb = _dot(u, w_ref[:, B0:B1])
    lg = par_ref[R_LB:R_LB + DEPTH, 0:BW]
    e = jnp.exp(lg - jnp.max(lg, axis=0, keepdims=True))
    lb = jnp.zeros((1, BW), F32)
    for j in range(1, layer + 1):
        lb = lb + e[j:j + 1]
    lb = lb / jnp.sum(e, axis=0, keepdims=True)
    z = pb[:, BW:2 * BW]
    bq_ref[...] = _silu(pb[:, :BW]) * (HEAD_DIM ** -0.5)
    bf_ref[...] = lb + (1.0 - lb) * _sigmoid(z)
    bv_ref[...] = pb[:, 2 * BW:3 * BW]
    gb_ref[...] = _silu(pb[:, 3 * BW:])

    pc = _dot(u, w_ref[:, C0:C1])
    qn = _rms(pc[:, :Q_LORA], par(R_QG, Q_LORA), NORM_EPS).astype(BF16)
    q = _dot(qn, wq_ref[...]) * Q_SCALE
    tab = lambda c, n: jnp.concatenate([tab_ref[:, c:c + LANES]] * n, axis=1)
    qp_ref[0] = _rope(q, tab(T_CQ, N_HEADS), tab(T_SQ, N_HEADS), ROPE_D // 2, HEAD_PAD, NOPE).astype(BF16)
    lat = _rms(pc[:, Q_LORA:Q_LORA + KV_LORA], par(R_KVG, KV_LORA), NORM_EPS)
    kr = _rope(pc[:, Q_LORA + KV_LORA:Q_LORA + KV_LORA + LANES], tab_ref[:, T_CK:T_CK + LANES],
               tab_ref[:, T_SK:T_SK + LANES], ROPE_D // 2, LANES)
    lat_ref[0] = lat
    kr_ref[0] = kr
    kt_ref[0], vkv_ref[0] = _keys_values(lat, kr, wkt_ref, wv_ref, placet_ref[...])
    gc_ref[0] = _silu(pc[:, Q_LORA + KV_LORA + LANES:])

    pd = _dot(u, w_ref[:, D0:D1])
    cos, sin = tab(T_CD, BW // LANES), tab(T_SD, BW // LANES)
    dq_ref[0] = _rope(pd[:, :BW], cos, sin, HEAD_DIM // 2, HEAD_DIM)
    dk_ref[0] = _rope(pd[:, BW:2 * BW], cos, sin, HEAD_DIM // 2, HEAD_DIM) * (HEAD_DIM ** -0.5)
    dv_ref[0] = pd[:, 2 * BW:3 * BW]
    gd_ref[0] = _silu(pd[:, 3 * BW:])


def _proj(layer, h, shift0, tab, par, w_rest, wl_hi, wl_lo, seg, wq, wkt, wv, placet):
    B, T, _ = h.shape
    tt = min(512, T)
    nt = T // tt
    bspec = lambda c: pl.BlockSpec((1, tt, c), lambda t, b: (b, t, 0))
    tspec = lambda c: pl.BlockSpec((tt, c), lambda t, b: (t, b))
    full = lambda a: pl.BlockSpec(a.shape, lambda t, b: (0,) * a.ndim)
    bm = lambda c, dt=F32: jax.ShapeDtypeStruct((B, T, c), dt)
    tm = lambda c: jax.ShapeDtypeStruct((T, B * c), F32)
    outs = ([(tm(BW), tspec(BW))] * 6 + [(tm(2 * BW), tspec(2 * BW))]
            + [(tm(BW), tspec(BW))] * 4
            + [(bm(QP, BF16), bspec(QP)), (bm(KV_LORA), bspec(KV_LORA)), (bm(LANES), bspec(LANES)),
               (bm(BW), bspec(BW)),
               (jax.ShapeDtypeStruct((B, QP, T), BF16), pl.BlockSpec((1, QP, tt), lambda t, b: (b, 0, t))),
               (bm(QP, BF16), bspec(QP))]
            + [(bm(BW), bspec(BW))] * 4
            + [(jax.ShapeDtypeStruct((nt, B, 1, SHIFT_COLS), F32),
                pl.BlockSpec((1, 1, 1, SHIFT_COLS), lambda t, b: (t, b, 0, 0)))])
    return pl.pallas_call(
        functools.partial(_proj_kernel, layer, tt),
        grid=(nt, B),
        in_specs=[bspec(D_MODEL), full(shift0), pl.BlockSpec((tt, TAB_COLS), lambda t, b: (t, 0)), full(par),
                  full(w_rest), full(wl_hi), full(wl_lo), full(seg), full(wq), full(wkt), full(wv), full(placet)],
        out_specs=tuple(o[1] for o in outs),
        out_shape=tuple(o[0] for o in outs),
        scratch_shapes=[pltpu.VMEM((B, 1, SHIFT_COLS), F32)],
        compiler_params=_cparams(("arbitrary", "arbitrary")),
        name="proj",
    )(h, shift0, tab, par, w_rest, wl_hi, wl_lo, seg, wq, wkt, wv, placet)


N_ACC = 1


N_COL = 8
LOOKAHEAD = 2
N_SLOTS = 4


def _seq_kernel(tc, bh, r_ref, w_ref, k_ref, kk_ref, b_ref, va_ref, q_ref, f_ref, vb_ref, sa0_ref, sb0_ref,
                ya_ref, yb_ref, sa_out_ref, sb_out_ref, st_a, st_b, *ring):
    slots, vslots, yslots = ring[0:N_SLOTS], ring[N_SLOTS:2 * N_SLOTS], ring[2 * N_SLOTS:3 * N_SLOTS]
    g_sz = bh // 2
    rep = LANES // bh
    vr = HEAD_DIM // rep
    n_pairs = tc // 2
    lane = lax.broadcasted_iota(jnp.int32, (1, LANES), 1)
    low = lane < HEAD_DIM
    hi_id = lane // bh
    grid_step = pl.program_id(0)

    @pl.when(grid_step == 0)
    def _():
        st_a[...] = sa0_ref[...]
        st_b[...] = sb0_ref[...]

    def paired(ref, p):
        a, b = ref[2 * p], ref[2 * p + 1]
        s0 = jnp.where(low, a, pltpu.roll(b, HEAD_DIM, axis=1))
        s1 = jnp.where(low, pltpu.roll(a, HEAD_DIM, axis=1), b)
        return jnp.concatenate([s0, s1] * rep, axis=0).T

    def prepare(p, slot, vslot):
        for o, ref in enumerate((r_ref, w_ref, k_ref, kk_ref, b_ref, q_ref, f_ref)):
            xt = paired(ref, p)
            slot[o] = xt
            if ref is f_ref:
                slot[o + 1] = 1.0 - xt
        for ri, ref in enumerate((va_ref, vb_ref)):
            xt = paired(ref, p)
            for s in range(2):
                vv = xt[s * HEAD_DIM:s * HEAD_DIM + vr]
                for g in range(1, rep):
                    vv = jnp.where(hi_id == g, xt[s * HEAD_DIM + g * vr:s * HEAD_DIM + (g + 1) * vr], vv)
                vslot[ri, s] = vv

    def emit(p, yslot):
        for ri, y_ref in enumerate((ya_ref, yb_ref)):
            blocks = [jnp.where(hi_id == g, yslot[ri, s], 0.0) for s in range(2) for g in range(rep)]
            zt = jnp.concatenate(blocks, axis=0).T
            yp = zt[0:bh]
            for g in range(1, rep):
                yp = yp + zt[g * bh:(g + 1) * bh]
            p0, p1 = yp[:g_sz], yp[g_sz:]
            y_ref[2 * p] = jnp.where(low, p0, pltpu.roll(p1, HEAD_DIM, axis=1))
            y_ref[2 * p + 1] = jnp.where(low, pltpu.roll(p0, HEAD_DIM, axis=1), p1)

    def tree(xs):
        while len(xs) > 1:
            xs = [xs[i] + xs[i + 1] for i in range(0, len(xs), 2)]
        return xs[0]

    def bc(slot, o, row):
        return jnp.broadcast_to(slot[o, pl.ds(row, 1), :], (vr, LANES))

    def accumulate(acc, k, p):
        acc[k % N_ACC] = p if acc[k % N_ACC] is None else acc[k % N_ACC] + p

    def first_sa(slot):
        acc = [None] * N_ACC
        for k in range(HEAD_DIM):
            accumulate(acc, k, st_a[k] * bc(slot, 3, k))
        return -tree(acc)

    def rwkv_step(s, slot, kk_slot, kk_row, vslot, yslot, sa):
        vv = vslot[0, s]
        acc_y, acc_s = [None] * N_ACC, [None] * N_ACC
        for k in range(HEAD_DIM):
            row = s * HEAD_DIM + k
            new = st_a[k] * bc(slot, 1, row) + sa * bc(slot, 4, row) + vv * bc(slot, 2, row)
            st_a[k] = new
            accumulate(acc_y, k, new * bc(slot, 0, row))
            accumulate(acc_s, k, new * bc(kk_slot, 3, kk_row + k))
        yslot[0, s] = tree(acc_y)
        return -tree(acc_s)

    def hgrn_pair(slot, vslot, yslot):
        v0, v1 = vslot[1, 0], vslot[1, 1]
        acc0, acc1 = [None] * N_ACC, [None] * N_ACC
        for k in range(HEAD_DIM):
            mid = st_b[k] * bc(slot, 6, k) + v0 * bc(slot, 7, k)
            accumulate(acc0, k, mid * bc(slot, 5, k))
            new = mid * bc(slot, 6, HEAD_DIM + k) + v1 * bc(slot, 7, HEAD_DIM + k)
            st_b[k] = new
            accumulate(acc1, k, new * bc(slot, 5, HEAD_DIM + k))
        yslot[1, 0] = tree(acc0)
        yslot[1, 1] = tree(acc1)

    def pair_of_steps(p, j, sa):
        emit(jnp.maximum(p - 1, 0), yslots[(j - 1) % N_SLOTS])
        nxt = (j + LOOKAHEAD) % N_SLOTS
        prepare(jnp.minimum(p + LOOKAHEAD, n_pairs - 1), slots[nxt], vslots[nxt])
        sa = rwkv_step(0, slots[j], slots[j], HEAD_DIM, vslots[j], yslots[j], sa)
        sa = rwkv_step(1, slots[j], slots[(j + 1) % N_SLOTS], 0, vslots[j], yslots[j], sa)
        hgrn_pair(slots[j], vslots[j], yslots[j])
        return sa

    for j in range(LOOKAHEAD):
        prepare(j, slots[j], vslots[j])
    yslots[N_SLOTS - 1][...] = jnp.zeros(yslots[N_SLOTS - 1].shape, F32)

    def body(i, sa):
        for j in range(N_SLOTS):
            sa = pair_of_steps(N_SLOTS * i + j, j, sa)
        return sa

    lax.fori_loop(0, n_pairs // N_SLOTS, body, first_sa(slots[0]))
    emit(n_pairs - 1, yslots[N_SLOTS - 1])

    @pl.when(grid_step == pl.num_programs(0) - 1)
    def _():
        sa_out_ref[...] = st_a[...]
        sb_out_ref[...] = st_b[...]


def _seq(cols_a, va, cols_b, vb, sa0, sb0, bh):
    T = va.shape[0]
    g_sz = bh // 2
    vr = sa0.shape[1]
    tc = min(128, T)
    assert tc % (2 * N_SLOTS) == 0
    view = lambda a: a.reshape(T, g_sz, LANES)
    ospec = pl.BlockSpec((tc, g_sz, LANES), lambda g: (g, 0, 0))
    sspec = pl.BlockSpec((HEAD_DIM, vr, LANES), lambda g: (0, 0, 0))
    args = [view(c) for c in (*cols_a, va, *cols_b, vb)] + [sa0, sb0]
    y_shape = jax.ShapeDtypeStruct((T, g_sz, LANES), F32)
    s_shape = jax.ShapeDtypeStruct((HEAD_DIM, vr, LANES), F32)
    slot = pltpu.VMEM((N_COL, 2 * HEAD_DIM, LANES), F32)
    small = pltpu.VMEM((2, 2, vr, LANES), F32)
    ya, yb, sa, sb = pl.pallas_call(
        functools.partial(_seq_kernel, tc, bh),
        grid=(T // tc,),
        in_specs=[ospec] * (len(args) - 2) + [sspec, sspec],
        out_specs=(ospec, ospec, sspec, sspec),
        out_shape=(y_shape, y_shape, s_shape, s_shape),
        scratch_shapes=[pltpu.VMEM((HEAD_DIM, vr, LANES), F32), pltpu.VMEM((HEAD_DIM, vr, LANES), F32),
                        *([slot] * N_SLOTS + [small] * (2 * N_SLOTS))],
        compiler_params=_cparams(("arbitrary",)),
        name="seq",
    )(*args)
    return ya.reshape(T, bh * HEAD_DIM), yb.reshape(T, bh * HEAD_DIM), sa, sb


def _ret_kernel(L, q_ref, k_ref, v_ref, s0_ref, o_ref, s_out_ref, st, dm, gq, gk):
    i = pl.program_id(1)
    log2_gamma = [float(np.log2(1.0 - 2.0 ** (-5.0 - h))) for h in range(N_HEADS)]

    @pl.when(i == 0)
    def _():
        st[...] = s0_ref[0]
        t_idx = lax.broadcasted_iota(jnp.int32, (L, L), 0)
        s_idx = lax.broadcasted_iota(jnp.int32, (L, L), 1)
        row = lax.broadcasted_iota(jnp.int32, (L, HEAD_DIM), 0).astype(F32)
        for h in range(N_HEADS):
            d = (t_idx - s_idx).astype(F32)
            dm[h] = jnp.where(t_idx >= s_idx, jnp.exp2(d * log2_gamma[h]), 0.0)
            gq[h] = jnp.exp2((row + 1.0) * log2_gamma[h])
            gk[h] = jnp.exp2((L - 1.0 - row) * log2_gamma[h])

    for h in range(N_HEADS):
        hs = slice(h * HEAD_DIM, (h + 1) * HEAD_DIM)
        q, k, v = q_ref[0, :, hs], k_ref[0, :, hs], v_ref[0, :, hs].astype(BF16)
        a = lax.dot_general(q.astype(BF16), k.astype(BF16), (((1,), (1,)), ((), ())),
                            preferred_element_type=F32) * dm[h]
        s_prev = st[h]
        o_ref[0, :, hs] = _dot(a.astype(BF16), v) + _dot((q * gq[h]).astype(BF16), s_prev.astype(BF16))
        kd = (k * gk[h]).astype(BF16)
        st[h] = s_prev * (2.0 ** (L * log2_gamma[h])) + lax.dot_general(
            kd, v, (((0,), (0,)), ((), ())), preferred_element_type=F32)

    @pl.when(i == pl.num_programs(1) - 1)
    def _():
        s_out_ref[0] = st[...]


def _ret(q, k, v, s0):
    B, T, _ = q.shape
    L = min(512, T)
    tspec = pl.BlockSpec((1, L, BW), lambda b, i: (b, i, 0))
    sspec = pl.BlockSpec((1, N_HEADS, HEAD_DIM, HEAD_DIM), lambda b, i: (b, 0, 0, 0))
    return pl.pallas_call(
        functools.partial(_ret_kernel, L),
        grid=(B, T // L),
        in_specs=[tspec, tspec, tspec, sspec],
        out_specs=(tspec, sspec),
        out_shape=(jax.ShapeDtypeStruct((B, T, BW), F32),
                   jax.ShapeDtypeStruct((B, N_HEADS, HEAD_DIM, HEAD_DIM), F32)),
        scratch_shapes=[pltpu.VMEM((N_HEADS, HEAD_DIM, HEAD_DIM), F32), pltpu.VMEM((N_HEADS, L, L), F32),
                        pltpu.VMEM((N_HEADS, L, HEAD_DIM), F32), pltpu.VMEM((N_HEADS, L, HEAD_DIM), F32)],
        compiler_params=_cparams(("arbitrary", "arbitrary")),
        name="ret",
    )(q, k, v, s0)


def _keys_values(lat, kr, wkt_ref, wv_ref, placet):
    lat = lat.astype(BF16)
    nt = (((1,), (1,)), ((), ()))
    kt = (lax.dot_general(wkt_ref[...], lat, nt, preferred_element_type=F32)
          + lax.dot_general(placet, kr.astype(BF16), nt, preferred_element_type=F32)).astype(BF16)
    lane = lax.broadcasted_iota(jnp.int32, (1, QP), 1) % HEAD_PAD
    return kt, jnp.where(lane == V_D, 1.0, _dot(lat, wv_ref[...])).astype(BF16)


def _kvprep_cached_kernel(past, T, plat_ref, pkr_ref, lat_ref, kr_ref, wkt_ref, wv_ref, placet_ref, kt_ref, v_ref):
    tk = past + T
    kt_ref[0, :, 0:past], v_ref[0, 0:past, :] = _keys_values(plat_ref[0], pkr_ref[0], wkt_ref, wv_ref,
                                                             placet_ref[:, 0:ROPE_D])
    kt_ref[0, :, past:tk], v_ref[0, past:tk, :] = _keys_values(lat_ref[0], kr_ref[0], wkt_ref, wv_ref, placet_ref[...])
    pad = kt_ref.shape[2] - tk
    if pad:
        kt_ref[0, :, tk:] = jnp.zeros((QP, pad), BF16)
        v_ref[0, tk:, :] = jnp.zeros((pad, QP), BF16)


def _kvprep_cached(past_lat, past_kr, lat, kr, wkt, wv, placet, tk_pad):
    B, past, _ = past_lat.shape
    T = lat.shape[1]
    whole = lambda a: pl.BlockSpec((1,) + a.shape[1:], lambda b: (b, 0, 0))
    full = lambda a: pl.BlockSpec(a.shape, lambda b: (0,) * a.ndim)
    return pl.pallas_call(
        functools.partial(_kvprep_cached_kernel, past, T),
        grid=(B,),
        in_specs=[whole(past_lat), whole(past_kr), whole(lat), whole(kr), full(wkt), full(wv), full(placet)],
        out_specs=(pl.BlockSpec((1, QP, tk_pad), lambda b: (b, 0, 0)), pl.BlockSpec((1, tk_pad, QP), lambda b: (b, 0, 0))),
        out_shape=(jax.ShapeDtypeStruct((B, QP, tk_pad), BF16), jax.ShapeDtypeStruct((B, tk_pad, QP), BF16)),
        compiler_params=_cparams(("arbitrary",)),
        name="kvprep_cached",
    )(past_lat, past_kr, lat, kr, wkt, wv, placet)


A_FIRST, A_LAST, A_MASKED = 1, 2, 4


def _attn_kernel(bq, bk, q0, kv_len, qi_ref, kj_ref, flag_ref, q_ref, kt_ref, v_ref, o_ref, m_ref, acc_ref):
    step = pl.program_id(1)
    i = qi_ref[step]
    j = kj_ref[step]
    flags = flag_ref[step]

    @pl.when((flags & A_FIRST) != 0)
    def _():
        m_ref[...] = jnp.full(m_ref.shape, -jnp.inf, F32)
        acc_ref[...] = jnp.zeros(acc_ref.shape, F32)

    def block(masked):
        if masked:
            qi = q0 + i * bq + lax.broadcasted_iota(jnp.int32, (bq, bk), 0)
            ki = j * bk + lax.broadcasted_iota(jnp.int32, (bq, bk), 1)
            keep = jnp.logical_and(ki // CHUNK <= qi // CHUNK, ki < kv_len)
        for h in range(N_HEADS):
            hs = slice(h * HEAD_PAD, (h + 1) * HEAD_PAD)
            s = _dot(q_ref[0, :, hs], kt_ref[0, hs, :])
            if masked:
                s = jnp.where(keep, s, -jnp.inf)
            m_old = m_ref[h]
            m_new = jnp.maximum(m_old, jnp.max(s, axis=-1, keepdims=True))
            alpha = jnp.exp2(m_old - m_new)
            m_full = jnp.concatenate([m_new] * (bk // LANES), axis=1) if bk % LANES == 0 else m_new[:, 0:1]
            p = jnp.exp2(s - m_full)
            acc_ref[:, hs] = alpha * acc_ref[:, hs] + _dot(p.astype(BF16), v_ref[0, :, hs])
            m_ref[h] = m_new

    @pl.when((flags & A_MASKED) == 0)
    def _():
        block(False)

    @pl.when((flags & A_MASKED) != 0)
    def _():
        block(True)

    @pl.when((flags & A_LAST) != 0)
    def _():
        for h in range(N_HEADS):
            a = acc_ref[:, h * HEAD_PAD:(h + 1) * HEAD_PAD]
            o_ref[0, :, h * V_D:(h + 1) * V_D] = a[:, :V_D] / a[:, V_D:V_D + 1]


def _attn(qp, ktp, vp, q0, kv_len, bk):
    B, T, _ = qp.shape
    tk = vp.shape[1]
    bq = min(512, T)
    nq, nk = T // bq, tk // bk
    qi, kj, flags = [], [], []
    for i in range(nq):
        first_q, last_q = q0 + i * bq, q0 + (i + 1) * bq - 1
        n_blocks = min(nk, (min(kv_len, (last_q // CHUNK + 1) * CHUNK) - 1) // bk + 1)
        for j in range(n_blocks):
            fully_visible = (j + 1) * bk <= min(kv_len, (first_q // CHUNK + 1) * CHUNK)
            qi.append(i)
            kj.append(j)
            flags.append((A_FIRST if j == 0 else 0) | (A_LAST if j == n_blocks - 1 else 0)
                         | (0 if fully_visible else A_MASKED))
    sched = [jnp.asarray(np.asarray(a, np.int32)) for a in (qi, kj, flags)]
    grid_spec = pltpu.PrefetchScalarGridSpec(
        num_scalar_prefetch=3,
        grid=(B, len(qi)),
        in_specs=[pl.BlockSpec((1, bq, QP), lambda b, s, qi, kj, fl: (b, qi[s], 0)),
                  pl.BlockSpec((1, QP, bk), lambda b, s, qi, kj, fl: (b, 0, kj[s])),
                  pl.BlockSpec((1, bk, QP), lambda b, s, qi, kj, fl: (b, kj[s], 0))],
        out_specs=pl.BlockSpec((1, bq, BW), lambda b, s, qi, kj, fl: (b, qi[s], 0)),
        scratch_shapes=[pltpu.VMEM((N_HEADS, bq, LANES), F32), pltpu.VMEM((bq, QP), F32)])
    return pl.pallas_call(
        functools.partial(_attn_kernel, bq, bk, q0, kv_len),
        grid_spec=grid_spec,
        out_shape=jax.ShapeDtypeStruct((B, T, BW), F32),
        compiler_params=_cparams(("arbitrary", "arbitrary")),
        name="attn",
    )(*sched, qp, ktp, vp)


def _merge_kernel(final, h_ref, ya_ref, posta_ref, ob_ref, gb_ref, oc_ref, gc_ref, od_ref, gd_ref,
                  par_ref, seg_ref, wg_ref, wb_ref, wo_ref, out_ref):
    par = lambda r, n: par_ref[r:r + 1, 0:n]
    seg = seg_ref[...]
    h = h_ref[0]
    u = _rms(h, par(R_NORM, D_MODEL), NORM_EPS).astype(BF16)
    inv = 1.0 / HEAD_DIM

    ya = ya_ref[...]
    xc = ya - _dot_lhs2(ya, seg) * inv
    yn = xc * lax.rsqrt(_dot_lhs2(xc * xc, seg) * inv + GN_EPS) * par(R_LNW, BW) + par(R_LNB, BW)
    oa = (yn + posta_ref[:, :BW]) * posta_ref[:, BW:]

    def head_rms(o, g):
        return o * lax.rsqrt(_dot_lhs2(o * o, seg) * inv + HEAD_NORM_EPS) * g

    ob = head_rms(ob_ref[...], par(R_HG, BW)) * gb_ref[...]
    oc = oc_ref[0] * gc_ref[0]
    od = head_rms(od_ref[0], par(R_RG, BW)) * gd_ref[0]

    merged = None
    for n, o in enumerate((oa, ob, oc, od)):
        gate = _sigmoid(_dot(u, wg_ref[:, n * D_MODEL:(n + 1) * D_MODEL]))
        term = gate * _dot(o.astype(BF16), wb_ref[n * BW:(n + 1) * BW, :])
        merged = term if merged is None else merged + term
    out = h + _dot(merged.astype(BF16), wo_ref[...])
    if final:
        out = _rms(out, par(R_FG, D_MODEL), NORM_EPS)
    out_ref[0] = out


def _merge(final, h, ya, posta, ob, gb, oc, gc, od, gd, par, seg, wg, wb, wo):
    B, T, _ = h.shape
    tt = min(512, T)
    bspec = lambda c: pl.BlockSpec((1, tt, c), lambda t, b: (b, t, 0))
    tspec = lambda c: pl.BlockSpec((tt, c), lambda t, b: (t, b))
    full = lambda a: pl.BlockSpec(a.shape, lambda t, b: (0,) * a.ndim)
    consts = (par, seg, wg, wb, wo)
    return pl.pallas_call(
        functools.partial(_merge_kernel, final),
        grid=(T // tt, B),
        in_specs=[bspec(D_MODEL), tspec(BW), tspec(2 * BW), tspec(BW), tspec(BW), bspec(BW), bspec(BW), bspec(BW),
                  bspec(BW)] + [full(a) for a in consts],
        out_specs=bspec(D_MODEL),
        out_shape=jax.ShapeDtypeStruct((B, T, D_MODEL), F32),
        compiler_params=_cparams(("arbitrary", "arbitrary")),
        name="merge",
    )(h, ya, posta, ob, gb, oc, gc, od, gd, *consts)


def _rope_tables(past, T):
    pos = (past + jnp.arange(T, dtype=jnp.int32)).astype(F32)[:, None]

    def tables(width, group, start, d):
        lane = np.arange(width) % group - start
        on = (lane >= 0) & (lane < d)
        idx = np.where(on, lane % (d // 2), 0)
        inv_freq = jnp.power(ROPE_BASE, -jnp.arange(0, d, 2, dtype=F32) / d)
        ang = pos * inv_freq[None, :]
        cos = jnp.where(on[None, :], jnp.cos(ang)[:, idx], 1.0)
        sign = np.where(lane < d // 2, -1.0, 1.0).astype(np.float32)
        sin = jnp.where(on[None, :], jnp.sin(ang)[:, idx] * sign[None, :], 0.0)
        return cos, sin

    cd, sd = tables(LANES, HEAD_DIM, 0, HEAD_DIM)
    cq, sq = tables(LANES, HEAD_PAD, NOPE, ROPE_D)
    ck, sk = tables(LANES, LANES, 0, ROPE_D)
    return jnp.concatenate([cd, sd, cq, sq, ck, sk], axis=1)


def _layer_weights(l, p):
    w_in = p["w_in"][l]
    gate_cols = N_BRANCH * D_MODEL
    c_kr = gate_cols + SHIFT_COLS + BW + 4 * BW + Q_LORA + KV_LORA
    c_g = c_kr + ROPE_D
    w_rest = jnp.concatenate([
        w_in[:, gate_cols:c_kr],
        jnp.pad(w_in[:, c_kr:c_g], ((0, 0), (0, LANES - ROPE_D))),
        w_in[:, c_g:],
    ], axis=1).astype(BF16)
    wg = w_in[:, :gate_cols].astype(BF16)

    z = jnp.zeros((64, BW), F32)
    wl = jnp.concatenate([jnp.concatenate([p["rwkv_w_up"][l], z], axis=1),
                          jnp.concatenate([z, p["rwkv_a_up"][l]], axis=1)], axis=0)
    wl_hi = wl.astype(BF16)
    wl_lo = (wl - wl_hi.astype(F32)).astype(BF16)

    wq = p["mla_w_q_up"][l].reshape(Q_LORA, N_HEADS, QK_D)
    wq = jnp.pad(wq, ((0, 0), (0, 0), (0, HEAD_PAD - QK_D))).reshape(Q_LORA, QP).astype(BF16)
    wkv = p["mla_w_kv_up"][l].reshape(KV_LORA, N_HEADS, NOPE + V_D)
    wkt = jnp.pad(wkv[:, :, :NOPE], ((0, 0), (0, 0), (0, HEAD_PAD - NOPE))).reshape(KV_LORA, QP).T.astype(BF16)
    wv = jnp.pad(wkv[:, :, NOPE:], ((0, 0), (0, 0), (0, HEAD_PAD - V_D))).reshape(KV_LORA, QP).astype(BF16)

    wb = p["w_branch"][l].reshape(N_BRANCH * BW, D_MODEL).astype(BF16)
    wo = p["w_out"][l].astype(BF16)

    return dict(w_rest=w_rest, wg=wg, wl_hi=wl_hi, wl_lo=wl_lo, wq=wq, wkt=wkt, wv=wv, wb=wb, wo=wo)


def _param_tables(p):
    def rows(v, n=1):
        v = v.reshape(DEPTH, n, -1).astype(F32)
        return jnp.pad(v, ((0, 0), (0, 0), (0, D_MODEL - v.shape[-1])))

    every = lambda v, n: jnp.broadcast_to(v.reshape(1, n, -1), (DEPTH, n, v.shape[-1]))
    parts = [rows(p["norm_g"]), rows(p["rwkv_mu"]), rows(p["rwkv_w0"]), rows(p["rwkv_a0"]), rows(p["rwkv_k_k"]),
             rows(p["rwkv_k_a"]), rows(p["rwkv_r_k"]), rows(every(p["hgrn_lb_logits"], DEPTH), DEPTH),
             rows(p["mla_q_norm_g"]), rows(p["mla_kv_norm_g"]), rows(p["rwkv_ln_w"]), rows(p["rwkv_ln_b"]),
             rows(p["hgrn_norm_g"]), rows(p["ret_norm_g"]), rows(every(p["final_norm_g"], 1))]
    used = sum(a.shape[1] for a in parts)
    return jnp.concatenate(parts + [jnp.zeros((DEPTH, N_ROWS - used, D_MODEL), F32)], axis=1)


def _all_weights(p):
    tables = _param_tables(p)
    return [dict(_layer_weights(l, p), par=tables[l]) for l in range(DEPTH)]


def _constants():
    head = np.arange(BW) // HEAD_DIM
    seg = jnp.asarray((head[:, None] == head[None, :]).astype(np.float32), dtype=BF16)
    place = np.zeros((LANES, QP), np.float32)
    for h in range(N_HEADS):
        for j in range(ROPE_D):
            place[j, h * HEAD_PAD + NOPE + j] = 1.0
    return seg, jnp.asarray(place.T, dtype=BF16)


def _state_in(s, B, rep, key_last):
    vr = HEAD_DIM // rep
    if key_last:
        y = s.reshape(B, 2, 2, rep, vr, HEAD_DIM).transpose(5, 4, 3, 2, 0, 1)
    else:
        y = s.reshape(B, 2, 2, HEAD_DIM, rep, vr).transpose(3, 5, 4, 2, 0, 1)
    return y.reshape(HEAD_DIM, vr, LANES).astype(F32)


def _state_out(y, B, rep, key_last):
    vr = HEAD_DIM // rep
    y = y.reshape(HEAD_DIM, vr, rep, 2, B, 2)
    if key_last:
        y = y.transpose(4, 5, 3, 2, 1, 0)
    else:
        y = y.transpose(4, 5, 3, 0, 2, 1)
    return y.reshape(B, N_HEADS, HEAD_DIM, HEAD_DIM)


def _run_trunk(x, shift0, rwkv0, hgrn0, ret0, lat0, kr0, p, weights, consts):
    B, T, _ = x.shape
    past = lat0.shape[2]
    bh = B * N_HEADS
    rep = LANES // bh
    assert rep * bh == LANES and HEAD_DIM % rep == 0 and (HEAD_DIM // rep) % SUBLANES == 0 and N_HEADS == 4
    seg, placet = consts
    tab = _rope_tables(past, T)
    h = x
    per_layer = []
    for l in range(DEPTH):
        w = weights[l]
        (ar, aw, ak, av, akk, ab, posta, bq, bf, bv, gb, qp, lat, kr, gc, ktp, vp, dq, dk, dv, gd, shn) = _proj(
            l, h, shift0[l][:, None, :], tab, w["par"], w["w_rest"], w["wl_hi"], w["wl_lo"], seg, w["wq"],
            w["wkt"], w["wv"], placet)

        ya, ob, s_rwkv, s_hgrn = _seq([ar, aw, ak, akk, ab], av, [bq, bf], bv, _state_in(rwkv0[l], B, rep, True),
                                      _state_in(hgrn0[l], B, rep, False), bh)
        od, s_ret = _ret(dq, dk, dv, ret0[l].astype(F32))

        tk = past + T
        if past == 0:
            blk = min(512, T)
        else:
            blk = -(-tk // LANES) * LANES
            assert blk * QP * 2 * 4 <= VMEM_LIMIT // 4 and past % LANES == 0
            ktp, vp = _kvprep_cached(lat0[l], kr0[l], lat, kr, w["wkt"], w["wv"], placet, blk)
        oc = _attn(qp, ktp, vp, past, tk, blk)

        h = _merge(l == DEPTH - 1, h, ya, posta, ob, gb, oc, gc, od, gd, w["par"], seg, w["wg"], w["wb"], w["wo"])
        per_layer.append((shn[-1, :, 0, :], _state_out(s_rwkv, B, rep, True), _state_out(s_hgrn, B, rep, False),
                          s_ret, lat, kr[:, :, :ROPE_D]))
    stacked = [jnp.stack([st[j] for st in per_layer]) for j in range(6)]
    return h, stacked


def kernel(x_prompt, x_sample, state_rwkv_shift, state_rwkv, state_hgrn, cache_mla_latent, cache_mla_krope, state_ret, norm_g, w_in, rwkv_mu, rwkv_w0, rwkv_w_up, rwkv_a0, rwkv_a_up, rwkv_k_k, rwkv_k_a, rwkv_r_k, rwkv_ln_w, rwkv_ln_b, hgrn_lb_logits, hgrn_norm_g, mla_q_norm_g, mla_w_q_up, mla_kv_norm_g, mla_w_kv_up, ret_norm_g, w_branch, w_out, final_norm_g):
    p = dict(norm_g=norm_g, w_in=w_in, rwkv_mu=rwkv_mu, rwkv_w0=rwkv_w0, rwkv_w_up=rwkv_w_up, rwkv_a0=rwkv_a0,
             rwkv_a_up=rwkv_a_up, rwkv_k_k=rwkv_k_k, rwkv_k_a=rwkv_k_a, rwkv_r_k=rwkv_r_k, rwkv_ln_w=rwkv_ln_w,
             rwkv_ln_b=rwkv_ln_b, hgrn_lb_logits=hgrn_lb_logits, hgrn_norm_g=hgrn_norm_g, mla_q_norm_g=mla_q_norm_g,
             mla_w_q_up=mla_w_q_up, mla_kv_norm_g=mla_kv_norm_g, mla_w_kv_up=mla_w_kv_up, ret_norm_g=ret_norm_g,
             w_branch=w_branch, w_out=w_out, final_norm_g=final_norm_g)
    weights = _all_weights(p)
    consts = _constants()
    bp, dt = x_prompt.shape[0], x_prompt.dtype
    zs = jnp.zeros((DEPTH, bp, N_HEADS, HEAD_DIM, HEAD_DIM), dt)
    y_p, (p_shift, p_rwkv, p_hgrn, p_ret, p_lat, p_kr) = _run_trunk(
        x_prompt, jnp.zeros((DEPTH, bp, SHIFT_COLS), dt), zs, zs, zs, jnp.zeros((DEPTH, bp, 0, KV_LORA), dt),
        jnp.zeros((DEPTH, bp, 0, ROPE_D), dt), p, weights, consts)
    y_s, (s_shift, s_rwkv, s_hgrn, s_ret, s_lat, s_kr) = _run_trunk(
        x_sample, state_rwkv_shift, state_rwkv, state_hgrn, state_ret, cache_mla_latent, cache_mla_krope,
        p, weights, consts)
    return (y_p, y_s, p_shift, s_shift, p_rwkv, s_rwkv, p_hgrn, s_hgrn, p_lat, s_lat, p_kr, s_kr, p_ret, s_ret)
```

```python
import functools

import jax
import jax.numpy as jnp
import numpy as np
from jax import lax
from jax.experimental import pallas as pl
from jax.experimental.pallas import tpu as pltpu

F32 = jnp.float32
BF16 = jnp.bfloat16

D_MODEL = 1024
DEPTH = 4
N_HEADS = 4
HEAD_DIM = 64
BW = N_HEADS * HEAD_DIM
N_BRANCH = 4
CHUNK = 64
SHIFT_COLS = 3 * BW + 64 + 64
GN_EPS = 64e-5
NORM_EPS = 1e-6
HEAD_NORM_EPS = 1e-5
ROPE_BASE = 10000.0
Q_LORA, KV_LORA, NOPE, ROPE_D, V_D = 256, 128, 64, 32, 64
QK_D = NOPE + ROPE_D
HEAD_PAD = 128
Q_SCALE = QK_D ** -0.5 * 1.4426950408889634
QP = N_HEADS * HEAD_PAD
LANES = 128
SUBLANES = 8

A0, A1 = 0, SHIFT_COLS + BW
B0, B1 = A1, A1 + 4 * BW
C0, C1 = B1, B1 + Q_LORA + KV_LORA + LANES + BW
D0, D1 = C1, C1 + 4 * BW
PROJ_COLS = D1

T_CD, T_SD, T_CQ, T_SQ, T_CK, T_SK, TAB_COLS = 0, 128, 256, 384, 512, 640, 768

(R_NORM, R_MU, R_W0, R_A0, R_KK, R_KA, R_RK, R_LB, R_QG, R_KVG,
 R_LNW, R_LNB, R_HG, R_RG, R_FG, N_ROWS) = (0, 1, 2, 3, 4, 5, 6, 7, 11, 12, 13, 14, 15, 16, 17, 24)

VMEM_LIMIT = 56 * 1024 * 1024


def _cparams(sem):
    return pltpu.CompilerParams(dimension_semantics=sem, vmem_limit_bytes=VMEM_LIMIT)


def _sigmoid(x):
    return 1.0 / (1.0 + jnp.exp(-x))


def _silu(x):
    return x * _sigmoid(x)


def _softplus(x):
    return jnp.maximum(x, 0.0) + jnp.log(1.0 + jnp.exp(-jnp.abs(x)))


def _dot(a, b):
    return jnp.dot(a, b, preferred_element_type=F32)


def _split(x):
    hi = x.astype(BF16)
    lo = (x - hi.astype(F32)).astype(BF16)
    return hi, lo


def _dot_lhs2(x, m_bf16):
    hi, lo = _split(x)
    return _dot(hi, m_bf16) + _dot(lo, m_bf16)


def _dot3(x, m_hi, m_lo):
    hi, lo = _split(x)
    return _dot(hi, m_hi) + _dot(lo, m_hi) + _dot(hi, m_lo)


def _rms(x, g, eps):
    return x * lax.rsqrt(jnp.mean(x * x, axis=-1, keepdims=True) + eps) * g


def _rope(x, cos, sin_signed, half, group, start=0):
    w = x.shape[-1]
    lane = lax.broadcasted_iota(jnp.int32, (1, w), 1) % group
    up = pltpu.roll(x, w - half, axis=1)
    dn = pltpu.roll(x, half, axis=1)
    return x * cos + jnp.where(lane < start + half, up, dn) * sin_signed


def _proj_kernel(layer, tt, h_ref, sh0_ref, tab_ref, par_ref, w_ref, wl_hi_ref, wl_lo_ref, seg_ref, wq_ref,
                 wkt_ref, wv_ref, placet_ref,
                 ar_ref, aw_ref, ak_ref, av_ref, akk_ref, ab_ref, posta_ref,
                 bq_ref, bf_ref, bv_ref, gb_ref,
                 qp_ref, lat_ref, kr_ref, gc_ref, kt_ref, vkv_ref,
                 dq_ref, dk_ref, dv_ref, gd_ref, shn_ref, carry_ref):
    t = pl.program_id(0)
    b = pl.program_id(1)
    par = lambda r, n: par_ref[r:r + 1, 0:n]
    seg = seg_ref[...]

    u = _rms(h_ref[0], par(R_NORM, D_MODEL), NORM_EPS).astype(BF16)

    pa_all = _dot(u, w_ref[:, A0:A1])
    pa = pa_all[:, :SHIFT_COLS]
    prev_row = jnp.where(t == 0, sh0_ref[b], carry_ref[b])
    row = lax.broadcasted_iota(jnp.int32, (tt, 1), 0)
    p_prev = jnp.where(row == 0, prev_row, pltpu.roll(pa, 1, axis=0))
    last = pa[tt - 1:tt, :]
    carry_ref[b] = last
    shn_ref[0, 0] = last
    xs = pa + (p_prev - pa) * par(R_MU, SHIFT_COLS)
    r, k, v, wdad = xs[:, :BW], xs[:, BW:2 * BW], xs[:, 2 * BW:3 * BW], xs[:, 3 * BW:]
    lane = lax.broadcasted_iota(jnp.int32, (1, LANES), 1)
    lora = _dot3(jnp.where(lane < 64, jnp.tanh(wdad), wdad), wl_hi_ref[...], wl_lo_ref[...])
    w_raw = -_softplus(-(par(R_W0, BW) + lora[:, :BW])) - 0.5
    a = _sigmoid(par(R_A0, BW) + lora[:, BW:])
    kk = k * par(R_KK, BW)
    kk = kk / jnp.maximum(jnp.sqrt(_dot_lhs2(kk * kk, seg)), 1e-12)
    k2 = k * (1.0 + (a - 1.0) * par(R_KA, BW))
    ar_ref[...] = r
    aw_ref[...] = jnp.exp(-jnp.exp(w_raw))
    ak_ref[...] = k2
    av_ref[...] = v
    akk_ref[...] = kk
    ab_ref[...] = kk * a
    posta_ref[:, :BW] = _dot_lhs2(r * k2 * par(R_RK, BW), seg) * v
    posta_ref[:, BW:] = _silu(pa_all[:, SHIFT_COLS:])

    pb = _dot(u, w_ref[:, B0:B1])
    lg = par_ref[R_LB:R_LB + DEPTH, 0:BW]
    e = jnp.exp(lg - jnp.max(lg, axis=0, keepdims=True))
    lb = jnp.zeros((1, BW), F32)
    for j in range(1, layer + 1):
        lb = lb + e[j:j + 1]
    lb = lb / jnp.sum(e, axis=0, keepdims=True)
    z = pb[:, BW:2 * BW]
    bq_ref[...] = _silu(pb[:, :BW]) * (HEAD_DIM ** -0.5)
    bf_ref[...] = lb + (1.0 - lb) * _sigmoid(z)
    bv_ref[...] = pb[:, 2 * BW:3 * BW]
    gb_ref[...] = _silu(pb[:, 3 * BW:])

    pc = _dot(u, w_ref[:, C0:C1])
    qn = _rms(pc[:, :Q_LORA], par(R_QG, Q_LORA), NORM_EPS).astype(BF16)
    q = _dot(qn, wq_ref[...]) * Q_SCALE
    tab = lambda c, n: jnp.concatenate([tab_ref[:, c:c + LANES]] * n, axis=1)
    qp_ref[0] = _rope(q, tab(T_CQ, N_HEADS), tab(T_SQ, N_HEADS), ROPE_D // 2, HEAD_PAD, NOPE).astype(BF16)
    lat = _rms(pc[:, Q_LORA:Q_LORA + KV_LORA], par(R_KVG, KV_LORA), NORM_EPS)
    kr = _rope(pc[:, Q_LORA + KV_LORA:Q_LORA + KV_LORA + LANES], tab_ref[:, T_CK:T_CK + LANES],
               tab_ref[:, T_SK:T_SK + LANES], ROPE_D // 2, LANES)
    lat_ref[0] = lat
    kr_ref[0] = kr
    kt_ref[0], vkv_ref[0] = _keys_values(lat, kr, wkt_ref, wv_ref, placet_ref[...])
    gc_ref[0] = _silu(pc[:, Q_LORA + KV_LORA + LANES:])

    pd = _dot(u, w_ref[:, D0:D1])
    cos, sin = tab(T_CD, BW // LANES), tab(T_SD, BW // LANES)
    dq_ref[0] = _rope(pd[:, :BW], cos, sin, HEAD_DIM // 2, HEAD_DIM)
    dk_ref[0] = _rope(pd[:, BW:2 * BW], cos, sin, HEAD_DIM // 2, HEAD_DIM) * (HEAD_DIM ** -0.5)
    dv_ref[0] = pd[:, 2 * BW:3 * BW]
    gd_ref[0] = _silu(pd[:, 3 * BW:])


def _proj(layer, h, shift0, tab, par, w_rest, wl_hi, wl_lo, seg, wq, wkt, wv, placet):
    B, T, _ = h.shape
    tt = min(512, T)
    nt = T // tt
    bspec = lambda c: pl.BlockSpec((1, tt, c), lambda t, b: (b, t, 0))
    tspec = lambda c: pl.BlockSpec((tt, c), lambda t, b: (t, b))
    full = lambda a: pl.BlockSpec(a.shape, lambda t, b: (0,) * a.ndim)
    bm = lambda c, dt=F32: jax.ShapeDtypeStruct((B, T, c), dt)
    tm = lambda c: jax.ShapeDtypeStruct((T, B * c), F32)
    outs = ([(tm(BW), tspec(BW))] * 6 + [(tm(2 * BW), tspec(2 * BW))]
            + [(tm(BW), tspec(BW))] * 4
            + [(bm(QP, BF16), bspec(QP)), (bm(KV_LORA), bspec(KV_LORA)), (bm(LANES), bspec(LANES)),
               (bm(BW), bspec(BW)),
               (jax.ShapeDtypeStruct((B, QP, T), BF16), pl.BlockSpec((1, QP, tt), lambda t, b: (b, 0, t))),
               (bm(QP, BF16), bspec(QP))]
            + [(bm(BW), bspec(BW))] * 4
            + [(jax.ShapeDtypeStruct((nt, B, 1, SHIFT_COLS), F32),
                pl.BlockSpec((1, 1, 1, SHIFT_COLS), lambda t, b: (t, b, 0, 0)))])
    return pl.pallas_call(
        functools.partial(_proj_kernel, layer, tt),
        grid=(nt, B),
        in_specs=[bspec(D_MODEL), full(shift0), pl.BlockSpec((tt, TAB_COLS), lambda t, b: (t, 0)), full(par),
                  full(w_rest), full(wl_hi), full(wl_lo), full(seg), full(wq), full(wkt), full(wv), full(placet)],
        out_specs=tuple(o[1] for o in outs),
        out_shape=tuple(o[0] for o in outs),
        scratch_shapes=[pltpu.VMEM((B, 1, SHIFT_COLS), F32)],
        compiler_params=_cparams(("arbitrary", "arbitrary")),
        name="proj",
    )(h, shift0, tab, par, w_rest, wl_hi, wl_lo, seg, wq, wkt, wv, placet)


N_ACC = 1


N_COL = 8
LOOKAHEAD = 2
N_SLOTS = 4


def _seq_kernel(tc, bh, r_ref, w_ref, k_ref, kk_ref, b_ref, va_ref, q_ref, f_ref, vb_ref, sa0_ref, sb0_ref,
                ya_ref, yb_ref, sa_out_ref, sb_out_ref, st_a, st_b, *ring):
    slots, vslots, yslots = ring[0:N_SLOTS], ring[N_SLOTS:2 * N_SLOTS], ring[2 * N_SLOTS:3 * N_SLOTS]
    g_sz = bh // 2
    rep = LANES // bh
    vr = HEAD_DIM // rep
    n_pairs = tc // 2
    lane = lax.broadcasted_iota(jnp.int32, (1, LANES), 1)
    low = lane < HEAD_DIM
    hi_id = lane // bh
    grid_step = pl.program_id(0)

    @pl.when(grid_step == 0)
    def _():
        st_a[...] = sa0_ref[...]
        st_b[...] = sb0_ref[...]

    def paired(ref, p):
        a, b = ref[2 * p], ref[2 * p + 1]
        s0 = jnp.where(low, a, pltpu.roll(b, HEAD_DIM, axis=1))
        s1 = jnp.where(low, pltpu.roll(a, HEAD_DIM, axis=1), b)
        return jnp.concatenate([s0, s1] * rep, axis=0).T

    def prepare(p, slot, vslot):
        for o, ref in enumerate((r_ref, w_ref, k_ref, kk_ref, b_ref, q_ref, f_ref)):
            xt = paired(ref, p)
            slot[o] = xt
            if ref is f_ref:
                slot[o + 1] = 1.0 - xt
        for ri, ref in enumerate((va_ref, vb_ref)):
            xt = paired(ref, p)
            for s in range(2):
                vv = xt[s * HEAD_DIM:s * HEAD_DIM + vr]
                for g in range(1, rep):
                    vv = jnp.where(hi_id == g, xt[s * HEAD_DIM + g * vr:s * HEAD_DIM + (g + 1) * vr], vv)
                vslot[ri, s] = vv

    def emit(p, yslot):
        for ri, y_ref in enumerate((ya_ref, yb_ref)):
            blocks = [jnp.where(hi_id == g, yslot[ri, s], 0.0) for s in range(2) for g in range(rep)]
            zt = jnp.concatenate(blocks, axis=0).T
            yp = zt[0:bh]
            for g in range(1, rep):
                yp = yp + zt[g * bh:(g + 1) * bh]
            p0, p1 = yp[:g_sz], yp[g_sz:]
            y_ref[2 * p] = jnp.where(low, p0, pltpu.roll(p1, HEAD_DIM, axis=1))
            y_ref[2 * p + 1] = jnp.where(low, pltpu.roll(p0, HEAD_DIM, axis=1), p1)

    def tree(xs):
        while len(xs) > 1:
            xs = [xs[i] + xs[i + 1] for i in range(0, len(xs), 2)]
        return xs[0]

    def bc(slot, o, row):
        return jnp.broadcast_to(slot[o, pl.ds(row, 1), :], (vr, LANES))

    def accumulate(acc, k, p):
        acc[k % N_ACC] = p if acc[k % N_ACC] is None else acc[k % N_ACC] + p

    def first_sa(slot):
        acc = [None] * N_ACC
        for k in range(HEAD_DIM):
            accumulate(acc, k, st_a[k] * bc(slot, 3, k))
        return -tree(acc)

    def rwkv_step(s, slot, kk_slot, kk_row, vslot, yslot, sa):
        vv = vslot[0, s]
        acc_y, acc_s = [None] * N_ACC, [None] * N_ACC
        for k in range(HEAD_DIM):
            row = s * HEAD_DIM + k
            new = st_a[k] * bc(slot, 1, row) + sa * bc(slot, 4, row) + vv * bc(slot, 2, row)
            st_a[k] = new
            accumulate(acc_y, k, new * bc(slot, 0, row))
            accumulate(acc_s, k, new * bc(kk_slot, 3, kk_row + k))
        yslot[0, s] = tree(acc_y)
        return -tree(acc_s)

    def hgrn_pair(slot, vslot, yslot):
        v0, v1 = vslot[1, 0], vslot[1, 1]
        acc0, acc1 = [None] * N_ACC, [None] * N_ACC
        for k in range(HEAD_DIM):
            mid = st_b[k] * bc(slot, 6, k) + v0 * bc(slot, 7, k)
            accumulate(acc0, k, mid * bc(slot, 5, k))
            new = mid * bc(slot, 6, HEAD_DIM + k) + v1 * bc(slot, 7, HEAD_DIM + k)
            st_b[k] = new
            accumulate(acc1, k, new * bc(slot, 5, HEAD_DIM + k))
        yslot[1, 0] = tree(acc0)
        yslot[1, 1] = tree(acc1)

    def pair_of_steps(p, j, sa):
        emit(jnp.maximum(p - 1, 0), yslots[(j - 1) % N_SLOTS])
        nxt = (j + LOOKAHEAD) % N_SLOTS
        prepare(jnp.minimum(p + LOOKAHEAD, n_pairs - 1), slots[nxt], vslots[nxt])
        sa = rwkv_step(0, slots[j], slots[j], HEAD_DIM, vslots[j], yslots[j], sa)
        sa = rwkv_step(1, slots[j], slots[(j + 1) % N_SLOTS], 0, vslots[j], yslots[j], sa)
        hgrn_pair(slots[j], vslots[j], yslots[j])
        return sa

    for j in range(LOOKAHEAD):
        prepare(j, slots[j], vslots[j])
    yslots[N_SLOTS - 1][...] = jnp.zeros(yslots[N_SLOTS - 1].shape, F32)

    def body(i, sa):
        for j in range(N_SLOTS):
            sa = pair_of_steps(N_SLOTS * i + j, j, sa)
        return sa

    lax.fori_loop(0, n_pairs // N_SLOTS, body, first_sa(slots[0]))
    emit(n_pairs - 1, yslots[N_SLOTS - 1])

    @pl.when(grid_step == pl.num_programs(0) - 1)
    def _():
        sa_out_ref[...] = st_a[...]
        sb_out_ref[...] = st_b[...]


def _seq(cols_a, va, cols_b, vb, sa0, sb0, bh):
    T = va.shape[0]
    g_sz = bh // 2
    vr = sa0.shape[1]
    tc = min(128, T)
    assert tc % (2 * N_SLOTS) == 0
    view = lambda a: a.reshape(T, g_sz, LANES)
    ospec = pl.BlockSpec((tc, g_sz, LANES), lambda g: (g, 0, 0))
    sspec = pl.BlockSpec((HEAD_DIM, vr, LANES), lambda g: (0, 0, 0))
    args = [view(c) for c in (*cols_a, va, *cols_b, vb)] + [sa0, sb0]
    y_shape = jax.ShapeDtypeStruct((T, g_sz, LANES), F32)
    s_shape = jax.ShapeDtypeStruct((HEAD_DIM, vr, LANES), F32)
    slot = pltpu.VMEM((N_COL, 2 * HEAD_DIM, LANES), F32)
    small = pltpu.VMEM((2, 2, vr, LANES), F32)
    ya, yb, sa, sb = pl.pallas_call(
        functools.partial(_seq_kernel, tc, bh),
        grid=(T // tc,),
        in_specs=[ospec] * (len(args) - 2) + [sspec, sspec],
        out_specs=(ospec, ospec, sspec, sspec),
        out_shape=(y_shape, y_shape, s_shape, s_shape),
        scratch_shapes=[pltpu.VMEM((HEAD_DIM, vr, LANES), F32), pltpu.VMEM((HEAD_DIM, vr, LANES), F32),
                        *([slot] * N_SLOTS + [small] * (2 * N_SLOTS))],
        compiler_params=_cparams(("arbitrary",)),
        name="seq",
    )(*args)
    return ya.reshape(T, bh * HEAD_DIM), yb.reshape(T, bh * HEAD_DIM), sa, sb


def _ret_kernel(L, q_ref, k_ref, v_ref, s0_ref, o_ref, s_out_ref, st, dm, gq, gk):
    i = pl.program_id(1)
    log2_gamma = [float(np.log2(1.0 - 2.0 ** (-5.0 - h))) for h in range(N_HEADS)]

    @pl.when(i == 0)
    def _():
        st[...] = s0_ref[0]
        t_idx = lax.broadcasted_iota(jnp.int32, (L, L), 0)
        s_idx = lax.broadcasted_iota(jnp.int32, (L, L), 1)
        row = lax.broadcasted_iota(jnp.int32, (L, HEAD_DIM), 0).astype(F32)
        for h in range(N_HEADS):
            d = (t_idx - s_idx).astype(F32)
            dm[h] = jnp.where(t_idx >= s_idx, jnp.exp2(d * log2_gamma[h]), 0.0)
            gq[h] = jnp.exp2((row + 1.0) * log2_gamma[h])
            gk[h] = jnp.exp2((L - 1.0 - row) * log2_gamma[h])

    for h in range(N_HEADS):
        hs = slice(h * HEAD_DIM, (h + 1) * HEAD_DIM)
        q, k, v = q_ref[0, :, hs], k_ref[0, :, hs], v_ref[0, :, hs].astype(BF16)
        a = lax.dot_general(q.astype(BF16), k.astype(BF16), (((1,), (1,)), ((), ())),
                            preferred_element_type=F32) * dm[h]
        s_prev = st[h]
        o_ref[0, :, hs] = _dot(a.astype(BF16), v) + _dot((q * gq[h]).astype(BF16), s_prev.astype(BF16))
        kd = (k * gk[h]).astype(BF16)
        st[h] = s_prev * (2.0 ** (L * log2_gamma[h])) + lax.dot_general(
            kd, v, (((0,), (0,)), ((), ())), preferred_element_type=F32)

    @pl.when(i == pl.num_programs(1) - 1)
    def _():
        s_out_ref[0] = st[...]


def _ret(q, k, v, s0):
    B, T, _ = q.shape
    L = min(512, T)
    tspec = pl.BlockSpec((1, L, BW), lambda b, i: (b, i, 0))
    sspec = pl.BlockSpec((1, N_HEADS, HEAD_DIM, HEAD_DIM), lambda b, i: (b, 0, 0, 0))
    return pl.pallas_call(
        functools.partial(_ret_kernel, L),
        grid=(B, T // L),
        in_specs=[tspec, tspec, tspec, sspec],
        out_specs=(tspec, sspec),
        out_shape=(jax.ShapeDtypeStruct((B, T, BW), F32),
                   jax.ShapeDtypeStruct((B, N_HEADS, HEAD_DIM, HEAD_DIM), F32)),
        scratch_shapes=[pltpu.VMEM((N_HEADS, HEAD_DIM, HEAD_DIM), F32), pltpu.VMEM((N_HEADS, L, L), F32),
                        pltpu.VMEM((N_HEADS, L, HEAD_DIM), F32), pltpu.VMEM((N_HEADS, L, HEAD_DIM), F32)],
        compiler_params=_cparams(("arbitrary", "arbitrary")),
        name="ret",
    )(q, k, v, s0)


def _keys_values(lat, kr, wkt_ref, wv_ref, placet):
    lat = lat.astype(BF16)
    nt = (((1,), (1,)), ((), ()))
    kt = (lax.dot_general(wkt_ref[...], lat, nt, preferred_element_type=F32)
          + lax.dot_general(placet, kr.astype(BF16), nt, preferred_element_type=F32)).astype(BF16)
    lane = lax.broadcasted_iota(jnp.int32, (1, QP), 1) % HEAD_PAD
    return kt, jnp.where(lane == V_D, 1.0, _dot(lat, wv_ref[...])).astype(BF16)


def _attn_cached_kernel(past, T, q_ref, plat_ref, pkr_ref, lat_ref, kr_ref, wkt_ref, wv_ref, placet_ref, o_ref,
                        kt_s, v_s):
    tk = past + T
    tk_pad = kt_s.shape[1]
    kt_s[:, 0:past], v_s[0:past, :] = _keys_values(plat_ref[0], pkr_ref[0], wkt_ref, wv_ref, placet_ref[:, 0:ROPE_D])
    kt_s[:, past:tk], v_s[past:tk, :] = _keys_values(lat_ref[0], kr_ref[0], wkt_ref, wv_ref, placet_ref[...])
    if tk_pad != tk:
        kt_s[:, tk:] = jnp.zeros((QP, tk_pad - tk), BF16)
        v_s[tk:, :] = jnp.zeros((tk_pad - tk, QP), BF16)
    qi = past + lax.broadcasted_iota(jnp.int32, (T, tk_pad), 0)
    ki = lax.broadcasted_iota(jnp.int32, (T, tk_pad), 1)
    keep = jnp.logical_and(ki // CHUNK <= qi // CHUNK, ki < tk)
    for h in range(N_HEADS):
        hs = slice(h * HEAD_PAD, (h + 1) * HEAD_PAD)
        s = jnp.where(keep, _dot(q_ref[0, :, hs], kt_s[hs, :]), -jnp.inf)
        p = jnp.exp2(s - jnp.max(s, axis=-1, keepdims=True))
        a = _dot(p.astype(BF16), v_s[:, hs])
        o_ref[0, :, h * V_D:(h + 1) * V_D] = a[:, :V_D] / a[:, V_D:V_D + 1]


def _attn_cached(qp, past_lat, past_kr, lat, kr, wkt, wv, placet):
    B, past, _ = past_lat.shape
    T = lat.shape[1]
    tk_pad = -(-(past + T) // LANES) * LANES
    assert tk_pad * QP * 2 * 2 <= VMEM_LIMIT // 4 and past % LANES == 0
    whole = lambda a: pl.BlockSpec((1,) + a.shape[1:], lambda b: (b, 0, 0))
    full = lambda a: pl.BlockSpec(a.shape, lambda b: (0,) * a.ndim)
    return pl.pallas_call(
        functools.partial(_attn_cached_kernel, past, T),
        grid=(B,),
        in_specs=[whole(qp), whole(past_lat), whole(past_kr), whole(lat), whole(kr), full(wkt), full(wv), full(placet)],
        out_specs=pl.BlockSpec((1, T, BW), lambda b: (b, 0, 0)),
        out_shape=jax.ShapeDtypeStruct((B, T, BW), F32),
        scratch_shapes=[pltpu.VMEM((QP, tk_pad), BF16), pltpu.VMEM((tk_pad, QP), BF16)],
        compiler_params=_cparams(("arbitrary",)),
        name="attn_cached",
    )(qp, past_lat, past_kr, lat, kr, wkt, wv, placet)


A_FIRST, A_LAST, A_MASKED = 1, 2, 4


def _attn_kernel(bq, bk, q0, kv_len, qi_ref, kj_ref, flag_ref, q_ref, kt_ref, v_ref, o_ref, m_ref, acc_ref):
    step = pl.program_id(1)
    i = qi_ref[step]
    j = kj_ref[step]
    flags = flag_ref[step]

    @pl.when((flags & A_FIRST) != 0)
    def _():
        m_ref[...] = jnp.full(m_ref.shape, -jnp.inf, F32)
        acc_ref[...] = jnp.zeros(acc_ref.shape, F32)

    def block(masked):
        if masked:
            qi = q0 + i * bq + lax.broadcasted_iota(jnp.int32, (bq, bk), 0)
            ki = j * bk + lax.broadcasted_iota(jnp.int32, (bq, bk), 1)
            keep = jnp.logical_and(ki // CHUNK <= qi // CHUNK, ki < kv_len)
        for h in range(N_HEADS):
            hs = slice(h * HEAD_PAD, (h + 1) * HEAD_PAD)
            s = _dot(q_ref[0, :, hs], kt_ref[0, hs, :])
            if masked:
                s = jnp.where(keep, s, -jnp.inf)
            m_old = m_ref[h]
            m_new = jnp.maximum(m_old, jnp.max(s, axis=-1, keepdims=True))
            alpha = jnp.exp2(m_old - m_new)
            m_full = jnp.concatenate([m_new] * (bk // LANES), axis=1) if bk % LANES == 0 else m_new[:, 0:1]
            p = jnp.exp2(s - m_full)
            acc_ref[:, hs] = alpha * acc_ref[:, hs] + _dot(p.astype(BF16), v_ref[0, :, hs])
            m_ref[h] = m_new

    @pl.when((flags & A_MASKED) == 0)
    def _():
        block(False)

    @pl.when((flags & A_MASKED) != 0)
    def _():
        block(True)

    @pl.when((flags & A_LAST) != 0)
    def _():
        for h in range(N_HEADS):
            a = acc_ref[:, h * HEAD_PAD:(h + 1) * HEAD_PAD]
            o_ref[0, :, h * V_D:(h + 1) * V_D] = a[:, :V_D] / a[:, V_D:V_D + 1]


def _attn(qp, ktp, vp, q0, kv_len, bk):
    B, T, _ = qp.shape
    tk = vp.shape[1]
    bq = min(512, T)
    nq, nk = T // bq, tk // bk
    qi, kj, flags = [], [], []
    for i in range(nq):
        first_q, last_q = q0 + i * bq, q0 + (i + 1) * bq - 1
        n_blocks = min(nk, (min(kv_len, (last_q // CHUNK + 1) * CHUNK) - 1) // bk + 1)
        for j in range(n_blocks):
            fully_visible = (j + 1) * bk <= min(kv_len, (first_q // CHUNK + 1) * CHUNK)
            qi.append(i)
            kj.append(j)
            flags.append((A_FIRST if j == 0 else 0) | (A_LAST if j == n_blocks - 1 else 0)
                         | (0 if fully_visible else A_MASKED))
    sched = [jnp.asarray(np.asarray(a, np.int32)) for a in (qi, kj, flags)]
    grid_spec = pltpu.PrefetchScalarGridSpec(
        num_scalar_prefetch=3,
        grid=(B, len(qi)),
        in_specs=[pl.BlockSpec((1, bq, QP), lambda b, s, qi, kj, fl: (b, qi[s], 0)),
                  pl.BlockSpec((1, QP, bk), lambda b, s, qi, kj, fl: (b, 0, kj[s])),
                  pl.BlockSpec((1, bk, QP), lambda b, s, qi, kj, fl: (b, kj[s], 0))],
        out_specs=pl.BlockSpec((1, bq, BW), lambda b, s, qi, kj, fl: (b, qi[s], 0)),
        scratch_shapes=[pltpu.VMEM((N_HEADS, bq, LANES), F32), pltpu.VMEM((bq, QP), F32)])
    return pl.pallas_call(
        functools.partial(_attn_kernel, bq, bk, q0, kv_len),
        grid_spec=grid_spec,
        out_shape=jax.ShapeDtypeStruct((B, T, BW), F32),
        compiler_params=_cparams(("arbitrary", "arbitrary")),
        name="attn",
    )(*sched, qp, ktp, vp)


def _merge_kernel(final, h_ref, ya_ref, posta_ref, ob_ref, gb_ref, oc_ref, gc_ref, od_ref, gd_ref,
                  par_ref, seg_ref, wg_ref, wb_ref, wo_ref, out_ref):
    par = lambda r, n: par_ref[r:r + 1, 0:n]
    seg = seg_ref[...]
    h = h_ref[0]
    u = _rms(h, par(R_NORM, D_MODEL), NORM_EPS).astype(BF16)
    inv = 1.0 / HEAD_DIM

    ya = ya_ref[...]
    xc = ya - _dot_lhs2(ya, seg) * inv
    yn = xc * lax.rsqrt(_dot_lhs2(xc * xc, seg) * inv + GN_EPS) * par(R_LNW, BW) + par(R_LNB, BW)
    oa = (yn + posta_ref[:, :BW]) * posta_ref[:, BW:]

    def head_rms(o, g):
        return o * lax.rsqrt(_dot_lhs2(o * o, seg) * inv + HEAD_NORM_EPS) * g

    ob = head_rms(ob_ref[...], par(R_HG, BW)) * gb_ref[...]
    oc = oc_ref[0] * gc_ref[0]
    od = head_rms(od_ref[0], par(R_RG, BW)) * gd_ref[0]

    merged = None
    for n, o in enumerate((oa, ob, oc, od)):
        gate = _sigmoid(_dot(u, wg_ref[:, n * D_MODEL:(n + 1) * D_MODEL]))
        term = gate * _dot(o.astype(BF16), wb_ref[n * BW:(n + 1) * BW, :])
        merged = term if merged is None else merged + term
    out = h + _dot(merged.astype(BF16), wo_ref[...])
    if final:
        out = _rms(out, par(R_FG, D_MODEL), NORM_EPS)
    out_ref[0] = out


def _merge(final, h, ya, posta, ob, gb, oc, gc, od, gd, par, seg, wg, wb, wo):
    B, T, _ = h.shape
    tt = min(512, T)
    bspec = lambda c: pl.BlockSpec((1, tt, c), lambda t, b: (b, t, 0))
    tspec = lambda c: pl.BlockSpec((tt, c), lambda t, b: (t, b))
    full = lambda a: pl.BlockSpec(a.shape, lambda t, b: (0,) * a.ndim)
    consts = (par, seg, wg, wb, wo)
    return pl.pallas_call(
        functools.partial(_merge_kernel, final),
        grid=(T // tt, B),
        in_specs=[bspec(D_MODEL), tspec(BW), tspec(2 * BW), tspec(BW), tspec(BW), bspec(BW), bspec(BW), bspec(BW),
                  bspec(BW)] + [full(a) for a in consts],
        out_specs=bspec(D_MODEL),
        out_shape=jax.ShapeDtypeStruct((B, T, D_MODEL), F32),
        compiler_params=_cparams(("arbitrary", "arbitrary")),
        name="merge",
    )(h, ya, posta, ob, gb, oc, gc, od, gd, *consts)


def _rope_tables(past, T):
    pos = (past + jnp.arange(T, dtype=jnp.int32)).astype(F32)[:, None]

    def tables(width, group, start, d):
        lane = np.arange(width) % group - start
        on = (lane >= 0) & (lane < d)
        idx = np.where(on, lane % (d // 2), 0)
        inv_freq = jnp.power(ROPE_BASE, -jnp.arange(0, d, 2, dtype=F32) / d)
        ang = pos * inv_freq[None, :]
        cos = jnp.where(on[None, :], jnp.cos(ang)[:, idx], 1.0)
        sign = np.where(lane < d // 2, -1.0, 1.0).astype(np.float32)
        sin = jnp.where(on[None, :], jnp.sin(ang)[:, idx] * sign[None, :], 0.0)
        return cos, sin

    cd, sd = tables(LANES, HEAD_DIM, 0, HEAD_DIM)
    cq, sq = tables(LANES, HEAD_PAD, NOPE, ROPE_D)
    ck, sk = tables(LANES, LANES, 0, ROPE_D)
    return jnp.concatenate([cd, sd, cq, sq, ck, sk], axis=1)


def _layer_weights(l, p):
    w_in = p["w_in"][l]
    gate_cols = N_BRANCH * D_MODEL
    c_kr = gate_cols + SHIFT_COLS + BW + 4 * BW + Q_LORA + KV_LORA
    c_g = c_kr + ROPE_D
    w_rest = jnp.concatenate([
        w_in[:, gate_cols:c_kr],
        jnp.pad(w_in[:, c_kr:c_g], ((0, 0), (0, LANES - ROPE_D))),
        w_in[:, c_g:],
    ], axis=1).astype(BF16)
    wg = w_in[:, :gate_cols].astype(BF16)

    z = jnp.zeros((64, BW), F32)
    wl = jnp.concatenate([jnp.concatenate([p["rwkv_w_up"][l], z], axis=1),
                          jnp.concatenate([z, p["rwkv_a_up"][l]], axis=1)], axis=0)
    wl_hi = wl.astype(BF16)
    wl_lo = (wl - wl_hi.astype(F32)).astype(BF16)

    wq = p["mla_w_q_up"][l].reshape(Q_LORA, N_HEADS, QK_D)
    wq = jnp.pad(wq, ((0, 0), (0, 0), (0, HEAD_PAD - QK_D))).reshape(Q_LORA, QP).astype(BF16)
    wkv = p["mla_w_kv_up"][l].reshape(KV_LORA, N_HEADS, NOPE + V_D)
    wkt = jnp.pad(wkv[:, :, :NOPE], ((0, 0), (0, 0), (0, HEAD_PAD - NOPE))).reshape(KV_LORA, QP).T.astype(BF16)
    wv = jnp.pad(wkv[:, :, NOPE:], ((0, 0), (0, 0), (0, HEAD_PAD - V_D))).reshape(KV_LORA, QP).astype(BF16)

    wb = p["w_branch"][l].reshape(N_BRANCH * BW, D_MODEL).astype(BF16)
    wo = p["w_out"][l].astype(BF16)

    return dict(w_rest=w_rest, wg=wg, wl_hi=wl_hi, wl_lo=wl_lo, wq=wq, wkt=wkt, wv=wv, wb=wb, wo=wo)


def _param_tables(p):
    def rows(v, n=1):
        v = v.reshape(DEPTH, n, -1).astype(F32)
        return jnp.pad(v, ((0, 0), (0, 0), (0, D_MODEL - v.shape[-1])))

    every = lambda v, n: jnp.broadcast_to(v.reshape(1, n, -1), (DEPTH, n, v.shape[-1]))
    parts = [rows(p["norm_g"]), rows(p["rwkv_mu"]), rows(p["rwkv_w0"]), rows(p["rwkv_a0"]), rows(p["rwkv_k_k"]),
             rows(p["rwkv_k_a"]), rows(p["rwkv_r_k"]), rows(every(p["hgrn_lb_logits"], DEPTH), DEPTH),
             rows(p["mla_q_norm_g"]), rows(p["mla_kv_norm_g"]), rows(p["rwkv_ln_w"]), rows(p["rwkv_ln_b"]),
             rows(p["hgrn_norm_g"]), rows(p["ret_norm_g"]), rows(every(p["final_norm_g"], 1))]
    used = sum(a.shape[1] for a in parts)
    return jnp.concatenate(parts + [jnp.zeros((DEPTH, N_ROWS - used, D_MODEL), F32)], axis=1)


def _all_weights(p):
    tables = _param_tables(p)
    return [dict(_layer_weights(l, p), par=tables[l]) for l in range(DEPTH)]


def _constants():
    head = np.arange(BW) // HEAD_DIM
    seg = jnp.asarray((head[:, None] == head[None, :]).astype(np.float32), dtype=BF16)
    place = np.zeros((LANES, QP), np.float32)
    for h in range(N_HEADS):
        for j in range(ROPE_D):
            place[j, h * HEAD_PAD + NOPE + j] = 1.0
    return seg, jnp.asarray(place.T, dtype=BF16)


def _state_in(s, B, rep, key_last):
    vr = HEAD_DIM // rep
    if key_last:
        y = s.reshape(B, 2, 2, rep, vr, HEAD_DIM).transpose(5, 4, 3, 2, 0, 1)
    else:
        y = s.reshape(B, 2, 2, HEAD_DIM, rep, vr).transpose(3, 5, 4, 2, 0, 1)
    return y.reshape(HEAD_DIM, vr, LANES).astype(F32)


def _state_out(y, B, rep, key_last):
    vr = HEAD_DIM // rep
    y = y.reshape(HEAD_DIM, vr, rep, 2, B, 2)
    if key_last:
        y = y.transpose(4, 5, 3, 2, 1, 0)
    else:
        y = y.transpose(4, 5, 3, 0, 2, 1)
    return y.reshape(B, N_HEADS, HEAD_DIM, HEAD_DIM)


def _run_trunk(x, shift0, rwkv0, hgrn0, ret0, lat0, kr0, p, weights, consts):
    B, T, _ = x.shape
    past = lat0.shape[2]
    bh = B * N_HEADS
    rep = LANES // bh
    assert rep * bh == LANES and HEAD_DIM % rep == 0 and (HEAD_DIM // rep) % SUBLANES == 0 and N_HEADS == 4
    seg, placet = consts
    tab = _rope_tables(past, T)
    h = x
    per_layer = []
    for l in range(DEPTH):
        w = weights[l]
        (ar, aw, ak, av, akk, ab, posta, bq, bf, bv, gb, qp, lat, kr, gc, ktp, vp, dq, dk, dv, gd, shn) = _proj(
            l, h, shift0[l][:, None, :], tab, w["par"], w["w_rest"], w["wl_hi"], w["wl_lo"], seg, w["wq"],
            w["wkt"], w["wv"], placet)

        ya, ob, s_rwkv, s_hgrn = _seq([ar, aw, ak, akk, ab], av, [bq, bf], bv, _state_in(rwkv0[l], B, rep, True),
                                      _state_in(hgrn0[l], B, rep, False), bh)
        od, s_ret = _ret(dq, dk, dv, ret0[l].astype(F32))

        if past == 0:
            oc = _attn(qp, ktp, vp, 0, T, min(512, T))
        else:
            oc = _attn_cached(qp, lat0[l], kr0[l], lat, kr, w["wkt"], w["wv"], placet)

        h = _merge(l == DEPTH - 1, h, ya, posta, ob, gb, oc, gc, od, gd, w["par"], seg, w["wg"], w["wb"], w["wo"])
        per_layer.append((shn[-1, :, 0, :], _state_out(s_rwkv, B, rep, True), _state_out(s_hgrn, B, rep, False),
                          s_ret, lat, kr[:, :, :ROPE_D]))
    stacked = [jnp.stack([st[j] for st in per_layer]) for j in range(6)]
    return h, stacked


def kernel(x_prompt, x_sample, state_rwkv_shift, state_rwkv, state_hgrn, cache_mla_latent, cache_mla_krope, state_ret, norm_g, w_in, rwkv_mu, rwkv_w0, rwkv_w_up, rwkv_a0, rwkv_a_up, rwkv_k_k, rwkv_k_a, rwkv_r_k, rwkv_ln_w, rwkv_ln_b, hgrn_lb_logits, hgrn_norm_g, mla_q_norm_g, mla_w_q_up, mla_kv_norm_g, mla_w_kv_up, ret_norm_g, w_branch, w_out, final_norm_g):
    p = dict(norm_g=norm_g, w_in=w_in, rwkv_mu=rwkv_mu, rwkv_w0=rwkv_w0, rwkv_w_up=rwkv_w_up, rwkv_a0=rwkv_a0,
             rwkv_a_up=rwkv_a_up, rwkv_k_k=rwkv_k_k, rwkv_k_a=rwkv_k_a, rwkv_r_k=rwkv_r_k, rwkv_ln_w=rwkv_ln_w,
             rwkv_ln_b=rwkv_ln_b, hgrn_lb_logits=hgrn_lb_logits, hgrn_norm_g=hgrn_norm_g, mla_q_norm_g=mla_q_norm_g,
             mla_w_q_up=mla_w_q_up, mla_kv_norm_g=mla_kv_norm_g, mla_w_kv_up=mla_w_kv_up, ret_norm_g=ret_norm_g,
             w_branch=w_branch, w_out=w_out, final_norm_g=final_norm_g)
    weights = _all_weights(p)
    consts = _constants()
    bp, dt = x_prompt.shape[0], x_prompt.dtype
    zs = jnp.zeros((DEPTH, bp, N_HEADS, HEAD_DIM, HEAD_DIM), dt)
    y_p, (p_shift, p_rwkv, p_hgrn, p_ret, p_lat, p_kr) = _run_trunk(
        x_prompt, jnp.zeros((DEPTH, bp, SHIFT_COLS), dt), zs, zs, zs, jnp.zeros((DEPTH, bp, 0, KV_LORA), dt),
        jnp.zeros((DEPTH, bp, 0, ROPE_D), dt), p, weights, consts)
    y_s, (s_shift, s_rwkv, s_hgrn, s_ret, s_lat, s_kr) = _run_trunk(
        x_sample, state_rwkv_shift, state_rwkv, state_hgrn, state_ret, cache_mla_latent, cache_mla_krope,
        p, weights, consts)
    return (y_p, y_s, p_shift, s_shift, p_rwkv, s_rwkv, p_hgrn, s_hgrn, p_lat, s_lat, p_kr, s_kr, p_ret, s_ret)
```

```python
import functools

import jax
import jax.numpy as jnp
import numpy as np
from jax import lax
from jax.experimental import pallas as pl
from jax.experimental.pallas import tpu as pltpu

F32 = jnp.float32
BF16 = jnp.bfloat16

D_MODEL = 1024
DEPTH = 4
N_HEADS = 4
HEAD_DIM = 64
BW = N_HEADS * HEAD_DIM
N_BRANCH = 4
CHUNK = 64
SHIFT_COLS = 3 * BW + 64 + 64
GN_EPS = 64e-5
NORM_EPS = 1e-6
HEAD_NORM_EPS = 1e-5
ROPE_BASE = 10000.0
Q_LORA, KV_LORA, NOPE, ROPE_D, V_D = 256, 128, 64, 32, 64
QK_D = NOPE + ROPE_D
HEAD_PAD = 128
Q_SCALE = QK_D ** -0.5 * 1.4426950408889634
QP = N_HEADS * HEAD_PAD
LANES = 128
SUBLANES = 8

A0, A1 = 0, SHIFT_COLS + BW
B0, B1 = A1, A1 + 4 * BW
C0, C1 = B1, B1 + Q_LORA + KV_LORA + LANES + BW
D0, D1 = C1, C1 + 4 * BW
PROJ_COLS = D1

T_CD, T_SD, T_CQ, T_SQ, T_CK, T_SK, TAB_COLS = 0, 128, 256, 384, 512, 640, 768

(R_NORM, R_MU, R_W0, R_A0, R_KK, R_KA, R_RK, R_LB, R_QG, R_KVG,
 R_LNW, R_LNB, R_HG, R_RG, R_FG, N_ROWS) = (0, 1, 2, 3, 4, 5, 6, 7, 11, 12, 13, 14, 15, 16, 17, 24)

VMEM_LIMIT = 56 * 1024 * 1024


def _cparams(sem):
    return pltpu.CompilerParams(dimension_semantics=sem, vmem_limit_bytes=VMEM_LIMIT)


def _sigmoid(x):
    return 1.0 / (1.0 + jnp.exp(-x))


def _silu(x):
    return x * _sigmoid(x)


def _softplus(x):
    return jnp.maximum(x, 0.0) + jnp.log(1.0 + jnp.exp(-jnp.abs(x)))


def _dot(a, b):
    return jnp.dot(a, b, preferred_element_type=F32)


def _split(x):
    hi = x.astype(BF16)
    lo = (x - hi.astype(F32)).astype(BF16)
    return hi, lo


def _dot_lhs2(x, m_bf16):
    hi, lo = _split(x)
    return _dot(hi, m_bf16) + _dot(lo, m_bf16)


def _dot3(x, m_hi, m_lo):
    hi, lo = _split(x)
    return _dot(hi, m_hi) + _dot(lo, m_hi) + _dot(hi, m_lo)


def _rms(x, g, eps):
    return x * lax.rsqrt(jnp.mean(x * x, axis=-1, keepdims=True) + eps) * g


def _rope(x, cos, sin_signed, half, group, start=0):
    w = x.shape[-1]
    lane = lax.broadcasted_iota(jnp.int32, (1, w), 1) % group
    up = pltpu.roll(x, w - half, axis=1)
    dn = pltpu.roll(x, half, axis=1)
    return x * cos + jnp.where(lane < start + half, up, dn) * sin_signed


def _proj_kernel(layer, tt, h_ref, sh0_ref, tab_ref, par_ref, w_ref, wl_hi_ref, wl_lo_ref, seg_ref, wq_ref,
                 wkt_ref, wv_ref, placet_ref,
                 ar_ref, aw_ref, ak_ref, av_ref, akk_ref, ab_ref, posta_ref,
                 bq_ref, bf_ref, bv_ref, gb_ref,
                 qp_ref, lat_ref, kr_ref, gc_ref, kt_ref, vkv_ref,
                 dq_ref, dk_ref, dv_ref, gd_ref, shn_ref, carry_ref):
    t = pl.program_id(0)
    b = pl.program_id(1)
    par = lambda r, n: par_ref[r:r + 1, 0:n]
    seg = seg_ref[...]

    u = _rms(h_ref[0], par(R_NORM, D_MODEL), NORM_EPS).astype(BF16)

    pa_all = _dot(u, w_ref[:, A0:A1])
    pa = pa_all[:, :SHIFT_COLS]
    prev_row = jnp.where(t == 0, sh0_ref[b], carry_ref[b])
    row = lax.broadcasted_iota(jnp.int32, (tt, 1), 0)
    p_prev = jnp.where(row == 0, prev_row, pltpu.roll(pa, 1, axis=0))
    last = pa[tt - 1:tt, :]
    carry_ref[b] = last
    shn_ref[0, 0] = last
    xs = pa + (p_prev - pa) * par(R_MU, SHIFT_COLS)
    r, k, v, wdad = xs[:, :BW], xs[:, BW:2 * BW], xs[:, 2 * BW:3 * BW], xs[:, 3 * BW:]
    lane = lax.broadcasted_iota(jnp.int32, (1, LANES), 1)
    lora = _dot3(jnp.where(lane < 64, jnp.tanh(wdad), wdad), wl_hi_ref[...], wl_lo_ref[...])
    w_raw = -_softplus(-(par(R_W0, BW) + lora[:, :BW])) - 0.5
    a = _sigmoid(par(R_A0, BW) + lora[:, BW:])
    kk = k * par(R_KK, BW)
    kk = kk / jnp.maximum(jnp.sqrt(_dot_lhs2(kk * kk, seg)), 1e-12)
    k2 = k * (1.0 + (a - 1.0) * par(R_KA, BW))
    ar_ref[...] = r
    aw_ref[...] = jnp.exp(-jnp.exp(w_raw))
    ak_ref[...] = k2
    av_ref[...] = v
    akk_ref[...] = kk
    ab_ref[...] = kk * a
    posta_ref[:, :BW] = _dot_lhs2(r * k2 * par(R_RK, BW), seg) * v
    posta_ref[:, BW:] = _silu(pa_all[:, SHIFT_COLS:])

    pb = _dot(u, w_ref[:, B0:B1])
    lg = par_ref[R_LB:R_LB + DEPTH, 0:BW]
    e = jnp.exp(lg - jnp.max(lg, axis=0, keepdims=True))
    lb = jnp.zeros((1, BW), F32)
    for j in range(1, layer + 1):
        lb = lb + e[j:j + 1]
    lb = lb / jnp.sum(e, axis=0, keepdims=True)
    z = pb[:, BW:2 * BW]
    bq_ref[...] = _silu(pb[:, :BW]) * (HEAD_DIM ** -0.5)
    bf_ref[...] = lb + (1.0 - lb) * _sigmoid(z)
    bv_ref[...] = pb[:, 2 * BW:3 * BW]
    gb_ref[...] = _silu(pb[:, 3 * BW:])

    pc = _dot(u, w_ref[:, C0:C1])
    qn = _rms(pc[:, :Q_LORA], par(R_QG, Q_LORA), NORM_EPS).astype(BF16)
    q = _dot(qn, wq_ref[...]) * Q_SCALE
    tab = lambda c, n: jnp.concatenate([tab_ref[:, c:c + LANES]] * n, axis=1)
    qp_ref[0] = _rope(q, tab(T_CQ, N_HEADS), tab(T_SQ, N_HEADS), ROPE_D // 2, HEAD_PAD, NOPE).astype(BF16)
    lat = _rms(pc[:, Q_LORA:Q_LORA + KV_LORA], par(R_KVG, KV_LORA), NORM_EPS)
    kr = _rope(pc[:, Q_LORA + KV_LORA:Q_LORA + KV_LORA + LANES], tab_ref[:, T_CK:T_CK + LANES],
               tab_ref[:, T_SK:T_SK + LANES], ROPE_D // 2, LANES)
    lat_ref[0] = lat
    kr_ref[0] = kr
    kt_ref[0], vkv_ref[0] = _keys_values(lat, kr, wkt_ref, wv_ref, placet_ref[...])
    gc_ref[0] = _silu(pc[:, Q_LORA + KV_LORA + LANES:])

    pd = _dot(u, w_ref[:, D0:D1])
    cos, sin = tab(T_CD, BW // LANES), tab(T_SD, BW // LANES)
    dq_ref[0] = _rope(pd[:, :BW], cos, sin, HEAD_DIM // 2, HEAD_DIM)
    dk_ref[0] = _rope(pd[:, BW:2 * BW], cos, sin, HEAD_DIM // 2, HEAD_DIM) * (HEAD_DIM ** -0.5)
    dv_ref[0] = pd[:, 2 * BW:3 * BW]
    gd_ref[0] = _silu(pd[:, 3 * BW:])


def _proj(layer, h, shift0, tab, par, w_rest, wl_hi, wl_lo, seg, wq, wkt, wv, placet):
    B, T, _ = h.shape
    tt = min(512, T)
    nt = T // tt
    bspec = lambda c: pl.BlockSpec((1, tt, c), lambda t, b: (b, t, 0))
    tspec = lambda c: pl.BlockSpec((tt, c), lambda t, b: (t, b))
    full = lambda a: pl.BlockSpec(a.shape, lambda t, b: (0,) * a.ndim)
    bm = lambda c, dt=F32: jax.ShapeDtypeStruct((B, T, c), dt)
    tm = lambda c: jax.ShapeDtypeStruct((T, B * c), F32)
    outs = ([(tm(BW), tspec(BW))] * 6 + [(tm(2 * BW), tspec(2 * BW))]
            + [(tm(BW), tspec(BW))] * 4
            + [(bm(QP, BF16), bspec(QP)), (bm(KV_LORA), bspec(KV_LORA)), (bm(LANES), bspec(LANES)),
               (bm(BW), bspec(BW)),
               (jax.ShapeDtypeStruct((B, QP, T), BF16), pl.BlockSpec((1, QP, tt), lambda t, b: (b, 0, t))),
               (bm(QP, BF16), bspec(QP))]
            + [(bm(BW), bspec(BW))] * 4
            + [(jax.ShapeDtypeStruct((nt, B, 1, SHIFT_COLS), F32),
                pl.BlockSpec((1, 1, 1, SHIFT_COLS), lambda t, b: (t, b, 0, 0)))])
    return pl.pallas_call(
        functools.partial(_proj_kernel, layer, tt),
        grid=(nt, B),
        in_specs=[bspec(D_MODEL), full(shift0), pl.BlockSpec((tt, TAB_COLS), lambda t, b: (t, 0)), full(par),
                  full(w_rest), full(wl_hi), full(wl_lo), full(seg), full(wq), full(wkt), full(wv), full(placet)],
        out_specs=tuple(o[1] for o in outs),
        out_shape=tuple(o[0] for o in outs),
        scratch_shapes=[pltpu.VMEM((B, 1, SHIFT_COLS), F32)],
        compiler_params=_cparams(("arbitrary", "arbitrary")),
        name="proj",
    )(h, shift0, tab, par, w_rest, wl_hi, wl_lo, seg, wq, wkt, wv, placet)


N_ACC = 1


N_COL = 8
LOOKAHEAD = 2
N_SLOTS = 4


def _seq_kernel(tc, bh, r_ref, w_ref, k_ref, kk_ref, b_ref, va_ref, q_ref, f_ref, vb_ref, sa0_ref, sb0_ref,
                ya_ref, yb_ref, sa_out_ref, sb_out_ref, st_a, st_b, *ring):
    slots, vslots, yslots = ring[0:N_SLOTS], ring[N_SLOTS:2 * N_SLOTS], ring[2 * N_SLOTS:3 * N_SLOTS]
    g_sz = bh // 2
    rep = LANES // bh
    vr = HEAD_DIM // rep
    n_pairs = tc // 2
    lane = lax.broadcasted_iota(jnp.int32, (1, LANES), 1)
    low = lane < HEAD_DIM
    hi_id = lane // bh
    grid_step = pl.program_id(0)

    @pl.when(grid_step == 0)
    def _():
        st_a[...] = sa0_ref[...]
        st_b[...] = sb0_ref[...]

    def paired(ref, p):
        a, b = ref[2 * p], ref[2 * p + 1]
        s0 = jnp.where(low, a, pltpu.roll(b, HEAD_DIM, axis=1))
        s1 = jnp.where(low, pltpu.roll(a, HEAD_DIM, axis=1), b)
        return jnp.concatenate([s0, s1] * rep, axis=0).T

    def prepare(p, slot, vslot):
        for o, ref in enumerate((r_ref, w_ref, k_ref, kk_ref, b_ref, q_ref, f_ref)):
            xt = paired(ref, p)
            slot[o] = xt
            if ref is f_ref:
                slot[o + 1] = 1.0 - xt
        for ri, ref in enumerate((va_ref, vb_ref)):
            xt = paired(ref, p)
            for s in range(2):
                vv = xt[s * HEAD_DIM:s * HEAD_DIM + vr]
                for g in range(1, rep):
                    vv = jnp.where(hi_id == g, xt[s * HEAD_DIM + g * vr:s * HEAD_DIM + (g + 1) * vr], vv)
                vslot[ri, s] = vv

    def emit(p, yslot):
        for ri, y_ref in enumerate((ya_ref, yb_ref)):
            blocks = [jnp.where(hi_id == g, yslot[ri, s], 0.0) for s in range(2) for g in range(rep)]
            zt = jnp.concatenate(blocks, axis=0).T
            yp = zt[0:bh]
            for g in range(1, rep):
                yp = yp + zt[g * bh:(g + 1) * bh]
            p0, p1 = yp[:g_sz], yp[g_sz:]
            y_ref[2 * p] = jnp.where(low, p0, pltpu.roll(p1, HEAD_DIM, axis=1))
            y_ref[2 * p + 1] = jnp.where(low, pltpu.roll(p0, HEAD_DIM, axis=1), p1)

    def tree(xs):
        while len(xs) > 1:
            xs = [xs[i] + xs[i + 1] for i in range(0, len(xs), 2)]
        return xs[0]

    def bc(slot, o, row):
        return jnp.broadcast_to(slot[o, pl.ds(row, 1), :], (vr, LANES))

    def accumulate(acc, k, p):
        acc[k % N_ACC] = p if acc[k % N_ACC] is None else acc[k % N_ACC] + p

    def first_sa(slot):
        acc = [None] * N_ACC
        for k in range(HEAD_DIM):
            accumulate(acc, k, st_a[k] * bc(slot, 3, k))
        return -tree(acc)

    def rwkv_step(s, slot, kk_slot, kk_row, vslot, yslot, sa):
        vv = vslot[0, s]
        acc_y, acc_s = [None] * N_ACC, [None] * N_ACC
        for k in range(HEAD_DIM):
            row = s * HEAD_DIM + k
            new = st_a[k] * bc(slot, 1, row) + sa * bc(slot, 4, row) + vv * bc(slot, 2, row)
            st_a[k] = new
            accumulate(acc_y, k, new * bc(slot, 0, row))
            accumulate(acc_s, k, new * bc(kk_slot, 3, kk_row + k))
        yslot[0, s] = tree(acc_y)
        return -tree(acc_s)

    def hgrn_pair(slot, vslot, yslot):
        v0, v1 = vslot[1, 0], vslot[1, 1]
        acc0, acc1 = [None] * N_ACC, [None] * N_ACC
        for k in range(HEAD_DIM):
            mid = st_b[k] * bc(slot, 6, k) + v0 * bc(slot, 7, k)
            accumulate(acc0, k, mid * bc(slot, 5, k))
            new = mid * bc(slot, 6, HEAD_DIM + k) + v1 * bc(slot, 7, HEAD_DIM + k)
            st_b[k] = new
            accumulate(acc1, k, new * bc(slot, 5, HEAD_DIM + k))
        yslot[1, 0] = tree(acc0)
        yslot[1, 1] = tree(acc1)

    def pair_of_steps(p, j, sa):
        emit(jnp.maximum(p - 1, 0), yslots[(j - 1) % N_SLOTS])
        nxt = (j + LOOKAHEAD) % N_SLOTS
        prepare(jnp.minimum(p + LOOKAHEAD, n_pairs - 1), slots[nxt], vslots[nxt])
        sa = rwkv_step(0, slots[j], slots[j], HEAD_DIM, vslots[j], yslots[j], sa)
        sa = rwkv_step(1, slots[j], slots[(j + 1) % N_SLOTS], 0, vslots[j], yslots[j], sa)
        hgrn_pair(slots[j], vslots[j], yslots[j])
        return sa

    for j in range(LOOKAHEAD):
        prepare(j, slots[j], vslots[j])
    yslots[N_SLOTS - 1][...] = jnp.zeros(yslots[N_SLOTS - 1].shape, F32)

    def body(i, sa):
        for j in range(N_SLOTS):
            sa = pair_of_steps(N_SLOTS * i + j, j, sa)
        return sa

    lax.fori_loop(0, n_pairs // N_SLOTS, body, first_sa(slots[0]))
    emit(n_pairs - 1, yslots[N_SLOTS - 1])

    @pl.when(grid_step == pl.num_programs(0) - 1)
    def _():
        sa_out_ref[...] = st_a[...]
        sb_out_ref[...] = st_b[...]


def _seq(cols_a, va, cols_b, vb, sa0, sb0, bh):
    T = va.shape[0]
    g_sz = bh // 2
    vr = sa0.shape[1]
    tc = min(128, T)
    assert tc % (2 * N_SLOTS) == 0
    view = lambda a: a.reshape(T, g_sz, LANES)
    ospec = pl.BlockSpec((tc, g_sz, LANES), lambda g: (g, 0, 0))
    sspec = pl.BlockSpec((HEAD_DIM, vr, LANES), lambda g: (0, 0, 0))
    args = [view(c) for c in (*cols_a, va, *cols_b, vb)] + [sa0, sb0]
    y_shape = jax.ShapeDtypeStruct((T, g_sz, LANES), F32)
    s_shape = jax.ShapeDtypeStruct((HEAD_DIM, vr, LANES), F32)
    slot = pltpu.VMEM((N_COL, 2 * HEAD_DIM, LANES), F32)
    small = pltpu.VMEM((2, 2, vr, LANES), F32)
    ya, yb, sa, sb = pl.pallas_call(
        functools.partial(_seq_kernel, tc, bh),
        grid=(T // tc,),
        in_specs=[ospec] * (len(args) - 2) + [sspec, sspec],
        out_specs=(ospec, ospec, sspec, sspec),
        out_shape=(y_shape, y_shape, s_shape, s_shape),
        scratch_shapes=[pltpu.VMEM((HEAD_DIM, vr, LANES), F32), pltpu.VMEM((HEAD_DIM, vr, LANES), F32),
                        *([slot] * N_SLOTS + [small] * (2 * N_SLOTS))],
        compiler_params=_cparams(("arbitrary",)),
        name="seq",
    )(*args)
    return ya.reshape(T, bh * HEAD_DIM), yb.reshape(T, bh * HEAD_DIM), sa, sb


def _ret_kernel(L, q_ref, k_ref, v_ref, s0_ref, o_ref, s_out_ref, st, dm, gq, gk):
    i = pl.program_id(1)
    log2_gamma = [float(np.log2(1.0 - 2.0 ** (-5.0 - h))) for h in range(N_HEADS)]

    @pl.when(i == 0)
    def _():
        st[...] = s0_ref[0]
        t_idx = lax.broadcasted_iota(jnp.int32, (L, L), 0)
        s_idx = lax.broadcasted_iota(jnp.int32, (L, L), 1)
        row = lax.broadcasted_iota(jnp.int32, (L, HEAD_DIM), 0).astype(F32)
        for h in range(N_HEADS):
            d = (t_idx - s_idx).astype(F32)
            dm[h] = jnp.where(t_idx >= s_idx, jnp.exp2(d * log2_gamma[h]), 0.0)
            gq[h] = jnp.exp2((row + 1.0) * log2_gamma[h])
            gk[h] = jnp.exp2((L - 1.0 - row) * log2_gamma[h])

    for h in range(N_HEADS):
        hs = slice(h * HEAD_DIM, (h + 1) * HEAD_DIM)
        q, k, v = q_ref[0, :, hs], k_ref[0, :, hs], v_ref[0, :, hs].astype(BF16)
        a = lax.dot_general(q.astype(BF16), k.astype(BF16), (((1,), (1,)), ((), ())),
                            preferred_element_type=F32) * dm[h]
        s_prev = st[h]
        o_ref[0, :, hs] = _dot(a.astype(BF16), v) + _dot((q * gq[h]).astype(BF16), s_prev.astype(BF16))
        kd = (k * gk[h]).astype(BF16)
        st[h] = s_prev * (2.0 ** (L * log2_gamma[h])) + lax.dot_general(
            kd, v, (((0,), (0,)), ((), ())), preferred_element_type=F32)

    @pl.when(i == pl.num_programs(1) - 1)
    def _():
        s_out_ref[0] = st[...]


def _ret(q, k, v, s0):
    B, T, _ = q.shape
    L = min(512, T)
    tspec = pl.BlockSpec((1, L, BW), lambda b, i: (b, i, 0))
    sspec = pl.BlockSpec((1, N_HEADS, HEAD_DIM, HEAD_DIM), lambda b, i: (b, 0, 0, 0))
    return pl.pallas_call(
        functools.partial(_ret_kernel, L),
        grid=(B, T // L),
        in_specs=[tspec, tspec, tspec, sspec],
        out_specs=(tspec, sspec),
        out_shape=(jax.ShapeDtypeStruct((B, T, BW), F32),
                   jax.ShapeDtypeStruct((B, N_HEADS, HEAD_DIM, HEAD_DIM), F32)),
        scratch_shapes=[pltpu.VMEM((N_HEADS, HEAD_DIM, HEAD_DIM), F32), pltpu.VMEM((N_HEADS, L, L), F32),
                        pltpu.VMEM((N_HEADS, L, HEAD_DIM), F32), pltpu.VMEM((N_HEADS, L, HEAD_DIM), F32)],
        compiler_params=_cparams(("arbitrary", "arbitrary")),
        name="ret",
    )(q, k, v, s0)


def _keys_values(lat, kr, wkt_ref, wv_ref, placet):
    lat = lat.astype(BF16)
    nt = (((1,), (1,)), ((), ()))
    kt = (lax.dot_general(wkt_ref[...], lat, nt, preferred_element_type=F32)
          + lax.dot_general(placet, kr.astype(BF16), nt, preferred_element_type=F32)).astype(BF16)
    lane = lax.broadcasted_iota(jnp.int32, (1, QP), 1) % HEAD_PAD
    return kt, jnp.where(lane == V_D, 1.0, _dot(lat, wv_ref[...])).astype(BF16)


def _attn_cached_kernel(past, T, q_ref, plat_ref, pkr_ref, lat_ref, kr_ref, wkt_ref, wv_ref, placet_ref, o_ref,
                        kt_s, v_s):
    tk = past + T
    tk_pad = kt_s.shape[1]
    kt_s[:, 0:past], v_s[0:past, :] = _keys_values(plat_ref[0], pkr_ref[0], wkt_ref, wv_ref, placet_ref[:, 0:ROPE_D])
    kt_s[:, past:tk], v_s[past:tk, :] = _keys_values(lat_ref[0], kr_ref[0], wkt_ref, wv_ref, placet_ref[...])
    if tk_pad != tk:
        kt_s[:, tk:] = jnp.zeros((QP, tk_pad - tk), BF16)
        v_s[tk:, :] = jnp.zeros((tk_pad - tk, QP), BF16)
    qi = past + lax.broadcasted_iota(jnp.int32, (T, tk_pad), 0)
    ki = lax.broadcasted_iota(jnp.int32, (T, tk_pad), 1)
    keep = jnp.logical_and(ki // CHUNK <= qi // CHUNK, ki < tk)
    for h in range(N_HEADS):
        hs = slice(h * HEAD_PAD, (h + 1) * HEAD_PAD)
        s = jnp.where(keep, _dot(q_ref[0, :, hs], kt_s[hs, :]), -jnp.inf)
        p = jnp.exp2(s - jnp.max(s, axis=-1, keepdims=True))
        a = _dot(p.astype(BF16), v_s[:, hs])
        o_ref[0, :, h * V_D:(h + 1) * V_D] = a[:, :V_D] / a[:, V_D:V_D + 1]


def _attn_cached(qp, past_lat, past_kr, lat, kr, wkt, wv, placet):
    B, past, _ = past_lat.shape
    T = lat.shape[1]
    tk_pad = -(-(past + T) // LANES) * LANES
    assert tk_pad * QP * 2 * 2 <= VMEM_LIMIT // 4 and past % LANES == 0
    whole = lambda a: pl.BlockSpec((1,) + a.shape[1:], lambda b: (b, 0, 0))
    full = lambda a: pl.BlockSpec(a.shape, lambda b: (0,) * a.ndim)
    return pl.pallas_call(
        functools.partial(_attn_cached_kernel, past, T),
        grid=(B,),
        in_specs=[whole(qp), whole(past_lat), whole(past_kr), whole(lat), whole(kr), full(wkt), full(wv), full(placet)],
        out_specs=pl.BlockSpec((1, T, BW), lambda b: (b, 0, 0)),
        out_shape=jax.ShapeDtypeStruct((B, T, BW), F32),
        scratch_shapes=[pltpu.VMEM((QP, tk_pad), BF16), pltpu.VMEM((tk_pad, QP), BF16)],
        compiler_params=_cparams(("arbitrary",)),
        name="attn_cached",
    )(qp, past_lat, past_kr, lat, kr, wkt, wv, placet)


A_FIRST, A_LAST, A_MASKED = 1, 2, 4


def _attn_kernel(bq, bk, q0, kv_len, qi_ref, kj_ref, flag_ref, q_ref, kt_ref, v_ref, o_ref, m_ref, acc_ref):
    step = pl.program_id(1)
    i = qi_ref[step]
    j = kj_ref[step]
    flags = flag_ref[step]

    @pl.when((flags & A_FIRST) != 0)
    def _():
        m_ref[...] = jnp.full(m_ref.shape, -jnp.inf, F32)
        acc_ref[...] = jnp.zeros(acc_ref.shape, F32)

    def block(masked):
        if masked:
            qi = q0 + i * bq + lax.broadcasted_iota(jnp.int32, (bq, bk), 0)
            ki = j * bk + lax.broadcasted_iota(jnp.int32, (bq, bk), 1)
            keep = jnp.logical_and(ki // CHUNK <= qi // CHUNK, ki < kv_len)
        for h in range(N_HEADS):
            hs = slice(h * HEAD_PAD, (h + 1) * HEAD_PAD)
            s = _dot(q_ref[0, :, hs], kt_ref[0, hs, :])
            if masked:
                s = jnp.where(keep, s, -jnp.inf)
            m_old = m_ref[h]
            m_new = jnp.maximum(m_old, jnp.max(s, axis=-1, keepdims=True))
            alpha = jnp.exp2(m_old - m_new)
            m_full = jnp.concatenate([m_new] * (bk // LANES), axis=1) if bk % LANES == 0 else m_new[:, 0:1]
            p = jnp.exp2(s - m_full)
            acc_ref[:, hs] = alpha * acc_ref[:, hs] + _dot(p.astype(BF16), v_ref[0, :, hs])
            m_ref[h] = m_new

    @pl.when((flags & A_MASKED) == 0)
    def _():
        block(False)

    @pl.when((flags & A_MASKED) != 0)
    def _():
        block(True)

    @pl.when((flags & A_LAST) != 0)
    def _():
        for h in range(N_HEADS):
            a = acc_ref[:, h * HEAD_PAD:(h + 1) * HEAD_PAD]
            o_ref[0, :, h * V_D:(h + 1) * V_D] = a[:, :V_D] / a[:, V_D:V_D + 1]


def _attn(qp, ktp, vp, q0, kv_len, bk):
    B, T, _ = qp.shape
    tk = vp.shape[1]
    bq = min(1024, T)
    nq, nk = T // bq, tk // bk
    qi, kj, flags = [], [], []
    for i in range(nq):
        first_q, last_q = q0 + i * bq, q0 + (i + 1) * bq - 1
        n_blocks = min(nk, (min(kv_len, (last_q // CHUNK + 1) * CHUNK) - 1) // bk + 1)
        for j in range(n_blocks):
            fully_visible = (j + 1) * bk <= min(kv_len, (first_q // CHUNK + 1) * CHUNK)
            qi.append(i)
            kj.append(j)
            flags.append((A_FIRST if j == 0 else 0) | (A_LAST if j == n_blocks - 1 else 0)
                         | (0 if fully_visible else A_MASKED))
    sched = [jnp.asarray(np.asarray(a, np.int32)) for a in (qi, kj, flags)]
    grid_spec = pltpu.PrefetchScalarGridSpec(
        num_scalar_prefetch=3,
        grid=(B, len(qi)),
        in_specs=[pl.BlockSpec((1, bq, QP), lambda b, s, qi, kj, fl: (b, qi[s], 0)),
                  pl.BlockSpec((1, QP, bk), lambda b, s, qi, kj, fl: (b, 0, kj[s])),
                  pl.BlockSpec((1, bk, QP), lambda b, s, qi, kj, fl: (b, kj[s], 0))],
        out_specs=pl.BlockSpec((1, bq, BW), lambda b, s, qi, kj, fl: (b, qi[s], 0)),
        scratch_shapes=[pltpu.VMEM((N_HEADS, bq, LANES), F32), pltpu.VMEM((bq, QP), F32)])
    return pl.pallas_call(
        functools.partial(_attn_kernel, bq, bk, q0, kv_len),
        grid_spec=grid_spec,
        out_shape=jax.ShapeDtypeStruct((B, T, BW), F32),
        compiler_params=_cparams(("arbitrary", "arbitrary")),
        name="attn",
    )(*sched, qp, ktp, vp)


def _merge_kernel(final, h_ref, ya_ref, posta_ref, ob_ref, gb_ref, oc_ref, gc_ref, od_ref, gd_ref,
                  par_ref, seg_ref, wg_ref, wb_ref, wo_ref, out_ref):
    par = lambda r, n: par_ref[r:r + 1, 0:n]
    seg = seg_ref[...]
    h = h_ref[0]
    u = _rms(h, par(R_NORM, D_MODEL), NORM_EPS).astype(BF16)
    inv = 1.0 / HEAD_DIM

    ya = ya_ref[...]
    xc = ya - _dot_lhs2(ya, seg) * inv
    yn = xc * lax.rsqrt(_dot_lhs2(xc * xc, seg) * inv + GN_EPS) * par(R_LNW, BW) + par(R_LNB, BW)
    oa = (yn + posta_ref[:, :BW]) * posta_ref[:, BW:]

    def head_rms(o, g):
        return o * lax.rsqrt(_dot_lhs2(o * o, seg) * inv + HEAD_NORM_EPS) * g

    ob = head_rms(ob_ref[...], par(R_HG, BW)) * gb_ref[...]
    oc = oc_ref[0] * gc_ref[0]
    od = head_rms(od_ref[0], par(R_RG, BW)) * gd_ref[0]

    merged = None
    for n, o in enumerate((oa, ob, oc, od)):
        gate = _sigmoid(_dot(u, wg_ref[:, n * D_MODEL:(n + 1) * D_MODEL]))
        term = gate * _dot(o.astype(BF16), wb_ref[n * BW:(n + 1) * BW, :])
        merged = term if merged is None else merged + term
    out = h + _dot(merged.astype(BF16), wo_ref[...])
    if final:
        out = _rms(out, par(R_FG, D_MODEL), NORM_EPS)
    out_ref[0] = out


def _merge(final, h, ya, posta, ob, gb, oc, gc, od, gd, par, seg, wg, wb, wo):
    B, T, _ = h.shape
    tt = min(512, T)
    bspec = lambda c: pl.BlockSpec((1, tt, c), lambda t, b: (b, t, 0))
    tspec = lambda c: pl.BlockSpec((tt, c), lambda t, b: (t, b))
    full = lambda a: pl.BlockSpec(a.shape, lambda t, b: (0,) * a.ndim)
    consts = (par, seg, wg, wb, wo)
    return pl.pallas_call(
        functools.partial(_merge_kernel, final),
        grid=(T // tt, B),
        in_specs=[bspec(D_MODEL), tspec(BW), tspec(2 * BW), tspec(BW), tspec(BW), bspec(BW), bspec(BW), bspec(BW),
                  bspec(BW)] + [full(a) for a in consts],
        out_specs=bspec(D_MODEL),
        out_shape=jax.ShapeDtypeStruct((B, T, D_MODEL), F32),
        compiler_params=_cparams(("arbitrary", "arbitrary")),
        name="merge",
    )(h, ya, posta, ob, gb, oc, gc, od, gd, *consts)


def _rope_tables(past, T):
    pos = (past + jnp.arange(T, dtype=jnp.int32)).astype(F32)[:, None]

    def tables(width, group, start, d):
        lane = np.arange(width) % group - start
        on = (lane >= 0) & (lane < d)
        idx = np.where(on, lane % (d // 2), 0)
        inv_freq = jnp.power(ROPE_BASE, -jnp.arange(0, d, 2, dtype=F32) / d)
        ang = pos * inv_freq[None, :]
        cos = jnp.where(on[None, :], jnp.cos(ang)[:, idx], 1.0)
        sign = np.where(lane < d // 2, -1.0, 1.0).astype(np.float32)
        sin = jnp.where(on[None, :], jnp.sin(ang)[:, idx] * sign[None, :], 0.0)
        return cos, sin

    cd, sd = tables(LANES, HEAD_DIM, 0, HEAD_DIM)
    cq, sq = tables(LANES, HEAD_PAD, NOPE, ROPE_D)
    ck, sk = tables(LANES, LANES, 0, ROPE_D)
    return jnp.concatenate([cd, sd, cq, sq, ck, sk], axis=1)


def _layer_weights(l, p):
    w_in = p["w_in"][l]
    gate_cols = N_BRANCH * D_MODEL
    c_kr = gate_cols + SHIFT_COLS + BW + 4 * BW + Q_LORA + KV_LORA
    c_g = c_kr + ROPE_D
    w_rest = jnp.concatenate([
        w_in[:, gate_cols:c_kr],
        jnp.pad(w_in[:, c_kr:c_g], ((0, 0), (0, LANES - ROPE_D))),
        w_in[:, c_g:],
    ], axis=1).astype(BF16)
    wg = w_in[:, :gate_cols].astype(BF16)

    z = jnp.zeros((64, BW), F32)
    wl = jnp.concatenate([jnp.concatenate([p["rwkv_w_up"][l], z], axis=1),
                          jnp.concatenate([z, p["rwkv_a_up"][l]], axis=1)], axis=0)
    wl_hi = wl.astype(BF16)
    wl_lo = (wl - wl_hi.astype(F32)).astype(BF16)

    wq = p["mla_w_q_up"][l].reshape(Q_LORA, N_HEADS, QK_D)
    wq = jnp.pad(wq, ((0, 0), (0, 0), (0, HEAD_PAD - QK_D))).reshape(Q_LORA, QP).astype(BF16)
    wkv = p["mla_w_kv_up"][l].reshape(KV_LORA, N_HEADS, NOPE + V_D)
    wkt = jnp.pad(wkv[:, :, :NOPE], ((0, 0), (0, 0), (0, HEAD_PAD - NOPE))).reshape(KV_LORA, QP).T.astype(BF16)
    wv = jnp.pad(wkv[:, :, NOPE:], ((0, 0), (0, 0), (0, HEAD_PAD - V_D))).reshape(KV_LORA, QP).astype(BF16)

    wb = p["w_branch"][l].reshape(N_BRANCH * BW, D_MODEL).astype(BF16)
    wo = p["w_out"][l].astype(BF16)

    return dict(w_rest=w_rest, wg=wg, wl_hi=wl_hi, wl_lo=wl_lo, wq=wq, wkt=wkt, wv=wv, wb=wb, wo=wo)


def _param_tables(p):
    def rows(v, n=1):
        v = v.reshape(DEPTH, n, -1).astype(F32)
        return jnp.pad(v, ((0, 0), (0, 0), (0, D_MODEL - v.shape[-1])))

    every = lambda v, n: jnp.broadcast_to(v.reshape(1, n, -1), (DEPTH, n, v.shape[-1]))
    parts = [rows(p["norm_g"]), rows(p["rwkv_mu"]), rows(p["rwkv_w0"]), rows(p["rwkv_a0"]), rows(p["rwkv_k_k"]),
             rows(p["rwkv_k_a"]), rows(p["rwkv_r_k"]), rows(every(p["hgrn_lb_logits"], DEPTH), DEPTH),
             rows(p["mla_q_norm_g"]), rows(p["mla_kv_norm_g"]), rows(p["rwkv_ln_w"]), rows(p["rwkv_ln_b"]),
             rows(p["hgrn_norm_g"]), rows(p["ret_norm_g"]), rows(every(p["final_norm_g"], 1))]
    used = sum(a.shape[1] for a in parts)
    return jnp.concatenate(parts + [jnp.zeros((DEPTH, N_ROWS - used, D_MODEL), F32)], axis=1)


def _all_weights(p):
    tables = _param_tables(p)
    return [dict(_layer_weights(l, p), par=tables[l]) for l in range(DEPTH)]


def _constants():
    head = np.arange(BW) // HEAD_DIM
    seg = jnp.asarray((head[:, None] == head[None, :]).astype(np.float32), dtype=BF16)
    place = np.zeros((LANES, QP), np.float32)
    for h in range(N_HEADS):
        for j in range(ROPE_D):
            place[j, h * HEAD_PAD + NOPE + j] = 1.0
    return seg, jnp.asarray(place.T, dtype=BF16)


def _state_in(s, B, rep, key_last):
    vr = HEAD_DIM // rep
    if key_last:
        y = s.reshape(B, 2, 2, rep, vr, HEAD_DIM).transpose(5, 4, 3, 2, 0, 1)
    else:
        y = s.reshape(B, 2, 2, HEAD_DIM, rep, vr).transpose(3, 5, 4, 2, 0, 1)
    return y.reshape(HEAD_DIM, vr, LANES).astype(F32)


def _state_out(y, B, rep, key_last):
    vr = HEAD_DIM // rep
    y = y.reshape(HEAD_DIM, vr, rep, 2, B, 2)
    if key_last:
        y = y.transpose(4, 5, 3, 2, 1, 0)
    else:
        y = y.transpose(4, 5, 3, 0, 2, 1)
    return y.reshape(B, N_HEADS, HEAD_DIM, HEAD_DIM)


def _run_trunk(x, shift0, rwkv0, hgrn0, ret0, lat0, kr0, p, weights, consts):
    B, T, _ = x.shape
    past = lat0.shape[2]
    bh = B * N_HEADS
    rep = LANES // bh
    assert rep * bh == LANES and HEAD_DIM % rep == 0 and (HEAD_DIM // rep) % SUBLANES == 0 and N_HEADS == 4
    seg, placet = consts
    tab = _rope_tables(past, T)
    h = x
    per_layer = []
    for l in range(DEPTH):
        w = weights[l]
        (ar, aw, ak, av, akk, ab, posta, bq, bf, bv, gb, qp, lat, kr, gc, ktp, vp, dq, dk, dv, gd, shn) = _proj(
            l, h, shift0[l][:, None, :], tab, w["par"], w["w_rest"], w["wl_hi"], w["wl_lo"], seg, w["wq"],
            w["wkt"], w["wv"], placet)

        ya, ob, s_rwkv, s_hgrn = _seq([ar, aw, ak, akk, ab], av, [bq, bf], bv, _state_in(rwkv0[l], B, rep, True),
                                      _state_in(hgrn0[l], B, rep, False), bh)
        od, s_ret = _ret(dq, dk, dv, ret0[l].astype(F32))

        if past == 0:
            oc = _attn(qp, ktp, vp, 0, T, min(512, T))
        else:
            oc = _attn_cached(qp, lat0[l], kr0[l], lat, kr, w["wkt"], w["wv"], placet)

        h = _merge(l == DEPTH - 1, h, ya, posta, ob, gb, oc, gc, od, gd, w["par"], seg, w["wg"], w["wb"], w["wo"])
        per_layer.append((shn[-1, :, 0, :], _state_out(s_rwkv, B, rep, True), _state_out(s_hgrn, B, rep, False),
                          s_ret, lat, kr[:, :, :ROPE_D]))
    stacked = [jnp.stack([st[j] for st in per_layer]) for j in range(6)]
    return h, stacked


def kernel(x_prompt, x_sample, state_rwkv_shift, state_rwkv, state_hgrn, cache_mla_latent, cache_mla_krope, state_ret, norm_g, w_in, rwkv_mu, rwkv_w0, rwkv_w_up, rwkv_a0, rwkv_a_up, rwkv_k_k, rwkv_k_a, rwkv_r_k, rwkv_ln_w, rwkv_ln_b, hgrn_lb_logits, hgrn_norm_g, mla_q_norm_g, mla_w_q_up, mla_kv_norm_g, mla_w_kv_up, ret_norm_g, w_branch, w_out, final_norm_g):
    p = dict(norm_g=norm_g, w_in=w_in, rwkv_mu=rwkv_mu, rwkv_w0=rwkv_w0, rwkv_w_up=rwkv_w_up, rwkv_a0=rwkv_a0,
             rwkv_a_up=rwkv_a_up, rwkv_k_k=rwkv_k_k, rwkv_k_a=rwkv_k_a, rwkv_r_k=rwkv_r_k, rwkv_ln_w=rwkv_ln_w,
             rwkv_ln_b=rwkv_ln_b, hgrn_lb_logits=hgrn_lb_logits, hgrn_norm_g=hgrn_norm_g, mla_q_norm_g=mla_q_norm_g,
             mla_w_q_up=mla_w_q_up, mla_kv_norm_g=mla_kv_norm_g, mla_w_kv_up=mla_w_kv_up, ret_norm_g=ret_norm_g,
             w_branch=w_branch, w_out=w_out, final_norm_g=final_norm_g)
    weights = _all_weights(p)
    consts = _constants()
    bp, dt = x_prompt.shape[0], x_prompt.dtype
    zs = jnp.zeros((DEPTH, bp, N_HEADS, HEAD_DIM, HEAD_DIM), dt)
    y_p, (p_shift, p_rwkv, p_hgrn, p_ret, p_lat, p_kr) = _run_trunk(
        x_prompt, jnp.zeros((DEPTH, bp, SHIFT_COLS), dt), zs, zs, zs, jnp.zeros((DEPTH, bp, 0, KV_LORA), dt),
        jnp.zeros((DEPTH, bp, 0, ROPE_D), dt), p, weights, consts)
    y_s, (s_shift, s_rwkv, s_hgrn, s_ret, s_lat, s_kr) = _run_trunk(
        x_sample, state_rwkv_shift, state_rwkv, state_hgrn, state_ret, cache_mla_latent, cache_mla_krope,
        p, weights, consts)
    return (y_p, y_s, p_shift, s_shift, p_rwkv, s_rwkv, p_hgrn, s_hgrn, p_lat, s_lat, p_kr, s_kr, p_ret, s_ret)
```

```python
import functools

import jax
import jax.numpy as jnp
import numpy as np
from jax import lax
from jax.experimental import pallas as pl
from jax.experimental.pallas import tpu as pltpu

F32 = jnp.float32
BF16 = jnp.bfloat16

D_MODEL = 1024
DEPTH = 4
N_HEADS = 4
HEAD_DIM = 64
BW = N_HEADS * HEAD_DIM
N_BRANCH = 4
CHUNK = 64
SHIFT_COLS = 3 * BW + 64 + 64
GN_EPS = 64e-5
NORM_EPS = 1e-6
HEAD_NORM_EPS = 1e-5
ROPE_BASE = 10000.0
Q_LORA, KV_LORA, NOPE, ROPE_D, V_D = 256, 128, 64, 32, 64
QK_D = NOPE + ROPE_D
HEAD_PAD = 128
Q_SCALE = QK_D ** -0.5 * 1.4426950408889634
QP = N_HEADS * HEAD_PAD
LANES = 128
SUBLANES = 8

A0, A1 = 0, SHIFT_COLS + BW
B0, B1 = A1, A1 + 4 * BW
C0, C1 = B1, B1 + Q_LORA + KV_LORA + LANES + BW
D0, D1 = C1, C1 + 4 * BW
PROJ_COLS = D1

T_CD, T_SD, T_CQ, T_SQ, T_CK, T_SK, TAB_COLS = 0, 128, 256, 384, 512, 640, 768

(R_NORM, R_MU, R_W0, R_A0, R_KK, R_KA, R_RK, R_LB, R_QG, R_KVG,
 R_LNW, R_LNB, R_HG, R_RG, R_FG, N_ROWS) = (0, 1, 2, 3, 4, 5, 6, 7, 11, 12, 13, 14, 15, 16, 17, 24)

VMEM_LIMIT = 56 * 1024 * 1024


def _cparams(sem):
    return pltpu.CompilerParams(dimension_semantics=sem, vmem_limit_bytes=VMEM_LIMIT)


def _sigmoid(x):
    return 1.0 / (1.0 + jnp.exp(-x))


def _silu(x):
    return x * _sigmoid(x)


def _softplus(x):
    return jnp.maximum(x, 0.0) + jnp.log(1.0 + jnp.exp(-jnp.abs(x)))


def _dot(a, b):
    return jnp.dot(a, b, preferred_element_type=F32)


def _split(x):
    hi = x.astype(BF16)
    lo = (x - hi.astype(F32)).astype(BF16)
    return hi, lo


def _dot_lhs2(x, m_bf16):
    hi, lo = _split(x)
    return _dot(hi, m_bf16) + _dot(lo, m_bf16)


def _dot3(x, m_hi, m_lo):
    hi, lo = _split(x)
    return _dot(hi, m_hi) + _dot(lo, m_hi) + _dot(hi, m_lo)


def _rms(x, g, eps):
    return x * lax.rsqrt(jnp.mean(x * x, axis=-1, keepdims=True) + eps) * g


def _rope(x, cos, sin_signed, half, group, start=0):
    w = x.shape[-1]
    lane = lax.broadcasted_iota(jnp.int32, (1, w), 1) % group
    up = pltpu.roll(x, w - half, axis=1)
    dn = pltpu.roll(x, half, axis=1)
    return x * cos + jnp.where(lane < start + half, up, dn) * sin_signed


def _proj_kernel(layer, tt, h_ref, sh0_ref, tab_ref, par_ref, w_ref, wl_hi_ref, wl_lo_ref, seg_ref, wq_ref,
                 wkt_ref, wv_ref, placet_ref,
                 ar_ref, aw_ref, ak_ref, av_ref, akk_ref, ab_ref, posta_ref,
                 bq_ref, bf_ref, bv_ref, gb_ref,
                 qp_ref, lat_ref, kr_ref, gc_ref, kt_ref, vkv_ref,
                 dq_ref, dk_ref, dv_ref, gd_ref, shn_ref, carry_ref):
    t = pl.program_id(0)
    b = pl.program_id(1)
    par = lambda r, n: par_ref[r:r + 1, 0:n]
    seg = seg_ref[...]

    u = _rms(h_ref[0], par(R_NORM, D_MODEL), NORM_EPS).astype(BF16)

    pa_all = _dot(u, w_ref[:, A0:A1])
    pa = pa_all[:, :SHIFT_COLS]
    prev_row = jnp.where(t == 0, sh0_ref[b], carry_ref[b])
    row = lax.broadcasted_iota(jnp.int32, (tt, 1), 0)
    p_prev = jnp.where(row == 0, prev_row, pltpu.roll(pa, 1, axis=0))
    last = pa[tt - 1:tt, :]
    carry_ref[b] = last
    shn_ref[0, 0] = last
    xs = pa + (p_prev - pa) * par(R_MU, SHIFT_COLS)
    r, k, v, wdad = xs[:, :BW], xs[:, BW:2 * BW], xs[:, 2 * BW:3 * BW], xs[:, 3 * BW:]
    lane = lax.broadcasted_iota(jnp.int32, (1, LANES), 1)
    lora = _dot3(jnp.where(lane < 64, jnp.tanh(wdad), wdad), wl_hi_ref[...], wl_lo_ref[...])
    w_raw = -_softplus(-(par(R_W0, BW) + lora[:, :BW])) - 0.5
    a = _sigmoid(par(R_A0, BW) + lora[:, BW:])
    kk = k * par(R_KK, BW)
    kk = kk / jnp.maximum(jnp.sqrt(_dot_lhs2(kk * kk, seg)), 1e-12)
    k2 = k * (1.0 + (a - 1.0) * par(R_KA, BW))
    ar_ref[...] = r
    aw_ref[...] = jnp.exp(-jnp.exp(w_raw))
    ak_ref[...] = k2
    av_ref[...] = v
    akk_ref[...] = kk
    ab_ref[...] = kk * a
    posta_ref[:, :BW] = _dot_lhs2(r * k2 * par(R_RK, BW), seg) * v
    posta_ref[:, BW:] = _silu(pa_all[:, SHIFT_COLS:])

    pb = _dot(u, w_ref[:, B0:B1])
    lg = par_ref[R_LB:R_LB + DEPTH, 0:BW]
    e = jnp.exp(lg - jnp.max(lg, axis=0, keepdims=True))
    lb = jnp.zeros((1, BW), F32)
    for j in range(1, layer + 1):
        lb = lb + e[j:j + 1]
    lb = lb / jnp.sum(e, axis=0, keepdims=True)
    z = pb[:, BW:2 * BW]
    bq_ref[...] = _silu(pb[:, :BW]) * (HEAD_DIM ** -0.5)
    bf_ref[...] = lb + (1.0 - lb) * _sigmoid(z)
    bv_ref[...] = pb[:, 2 * BW:3 * BW]
    gb_ref[...] = _silu(pb[:, 3 * BW:])

    pc = _dot(u, w_ref[:, C0:C1])
    qn = _rms(pc[:, :Q_LORA], par(R_QG, Q_LORA), NORM_EPS).astype(BF16)
    q = _dot(qn, wq_ref[...]) * Q_SCALE
    tab = lambda c, n: jnp.concatenate([tab_ref[:, c:c + LANES]] * n, axis=1)
    qp_ref[0] = _rope(q, tab(T_CQ, N_HEADS), tab(T_SQ, N_HEADS), ROPE_D // 2, HEAD_PAD, NOPE).astype(BF16)
    lat = _rms(pc[:, Q_LORA:Q_LORA + KV_LORA], par(R_KVG, KV_LORA), NORM_EPS)
    kr = _rope(pc[:, Q_LORA + KV_LORA:Q_LORA + KV_LORA + LANES], tab_ref[:, T_CK:T_CK + LANES],
               tab_ref[:, T_SK:T_SK + LANES], ROPE_D // 2, LANES)
    lat_ref[0] = lat
    kr_ref[0] = kr
    kt_ref[0], vkv_ref[0] = _keys_values(lat, kr, wkt_ref, wv_ref, placet_ref[...])
    gc_ref[0] = _silu(pc[:, Q_LORA + KV_LORA + LANES:])

    pd = _dot(u, w_ref[:, D0:D1])
    cos, sin = tab(T_CD, BW // LANES), tab(T_SD, BW // LANES)
    dq_ref[0] = _rope(pd[:, :BW], cos, sin, HEAD_DIM // 2, HEAD_DIM)
    dk_ref[0] = _rope(pd[:, BW:2 * BW], cos, sin, HEAD_DIM // 2, HEAD_DIM) * (HEAD_DIM ** -0.5)
    dv_ref[0] = pd[:, 2 * BW:3 * BW]
    gd_ref[0] = _silu(pd[:, 3 * BW:])


def _proj(layer, h, shift0, tab, par, w_rest, wl_hi, wl_lo, seg, wq, wkt, wv, placet):
    B, T, _ = h.shape
    tt = min(512, T)
    nt = T // tt
    bspec = lambda c: pl.BlockSpec((1, tt, c), lambda t, b: (b, t, 0))
    tspec = lambda c: pl.BlockSpec((tt, c), lambda t, b: (t, b))
    full = lambda a: pl.BlockSpec(a.shape, lambda t, b: (0,) * a.ndim)
    bm = lambda c, dt=F32: jax.ShapeDtypeStruct((B, T, c), dt)
    tm = lambda c: jax.ShapeDtypeStruct((T, B * c), F32)
    outs = ([(tm(BW), tspec(BW))] * 6 + [(tm(2 * BW), tspec(2 * BW))]
            + [(tm(BW), tspec(BW))] * 4
            + [(bm(QP, BF16), bspec(QP)), (bm(KV_LORA), bspec(KV_LORA)), (bm(LANES), bspec(LANES)),
               (bm(BW), bspec(BW)),
               (jax.ShapeDtypeStruct((B, QP, T), BF16), pl.BlockSpec((1, QP, tt), lambda t, b: (b, 0, t))),
               (bm(QP, BF16), bspec(QP))]
            + [(bm(BW), bspec(BW))] * 4
            + [(jax.ShapeDtypeStruct((nt, B, 1, SHIFT_COLS), F32),
                pl.BlockSpec((1, 1, 1, SHIFT_COLS), lambda t, b: (t, b, 0, 0)))])
    return pl.pallas_call(
        functools.partial(_proj_kernel, layer, tt),
        grid=(nt, B),
        in_specs=[bspec(D_MODEL), full(shift0), pl.BlockSpec((tt, TAB_COLS), lambda t, b: (t, 0)), full(par),
                  full(w_rest), full(wl_hi), full(wl_lo), full(seg), full(wq), full(wkt), full(wv), full(placet)],
        out_specs=tuple(o[1] for o in outs),
        out_shape=tuple(o[0] for o in outs),
        scratch_shapes=[pltpu.VMEM((B, 1, SHIFT_COLS), F32)],
        compiler_params=_cparams(("arbitrary", "arbitrary")),
        name="proj",
    )(h, shift0, tab, par, w_rest, wl_hi, wl_lo, seg, wq, wkt, wv, placet)


N_ACC = 1


N_COL = 8
LOOKAHEAD = 2
N_SLOTS = 4


def _seq_kernel(tc, bh, r_ref, w_ref, k_ref, kk_ref, b_ref, va_ref, q_ref, f_ref, vb_ref, sa0_ref, sb0_ref,
                ya_ref, yb_ref, sa_out_ref, sb_out_ref, st_a, st_b, *ring):
    slots, vslots, yslots = ring[0:N_SLOTS], ring[N_SLOTS:2 * N_SLOTS], ring[2 * N_SLOTS:3 * N_SLOTS]
    g_sz = bh // 2
    rep = LANES // bh
    vr = HEAD_DIM // rep
    n_pairs = tc // 2
    lane = lax.broadcasted_iota(jnp.int32, (1, LANES), 1)
    low = lane < HEAD_DIM
    hi_id = lane // bh
    grid_step = pl.program_id(0)

    @pl.when(grid_step == 0)
    def _():
        st_a[...] = sa0_ref[...]
        st_b[...] = sb0_ref[...]

    def paired(ref, p):
        a, b = ref[2 * p], ref[2 * p + 1]
        s0 = jnp.where(low, a, pltpu.roll(b, HEAD_DIM, axis=1))
        s1 = jnp.where(low, pltpu.roll(a, HEAD_DIM, axis=1), b)
        return jnp.concatenate([s0, s1] * rep, axis=0).T

    def prepare(p, slot, vslot):
        for o, ref in enumerate((r_ref, w_ref, k_ref, kk_ref, b_ref, q_ref, f_ref)):
            xt = paired(ref, p)
            slot[o] = xt
            if ref is f_ref:
                slot[o + 1] = 1.0 - xt
        for ri, ref in enumerate((va_ref, vb_ref)):
            xt = paired(ref, p)
            for s in range(2):
                vv = xt[s * HEAD_DIM:s * HEAD_DIM + vr]
                for g in range(1, rep):
                    vv = jnp.where(hi_id == g, xt[s * HEAD_DIM + g * vr:s * HEAD_DIM + (g + 1) * vr], vv)
                vslot[ri, s] = vv

    def emit(p, yslot):
        for ri, y_ref in enumerate((ya_ref, yb_ref)):
            blocks = [jnp.where(hi_id == g, yslot[ri, s], 0.0) for s in range(2) for g in range(rep)]
            zt = jnp.concatenate(blocks, axis=0).T
            yp = zt[0:bh]
            for g in range(1, rep):
                yp = yp + zt[g * bh:(g + 1) * bh]
            p0, p1 = yp[:g_sz], yp[g_sz:]
            y_ref[2 * p] = jnp.where(low, p0, pltpu.roll(p1, HEAD_DIM, axis=1))
            y_ref[2 * p + 1] = jnp.where(low, pltpu.roll(p0, HEAD_DIM, axis=1), p1)

    def tree(xs):
        while len(xs) > 1:
            xs = [xs[i] + xs[i + 1] for i in range(0, len(xs), 2)]
        return xs[0]

    def bc(slot, o, row):
        return jnp.broadcast_to(slot[o, pl.ds(row, 1), :], (vr, LANES))

    def accumulate(acc, k, p):
        acc[k % N_ACC] = p if acc[k % N_ACC] is None else acc[k % N_ACC] + p

    def first_sa(slot):
        acc = [None] * N_ACC
        for k in range(HEAD_DIM):
            accumulate(acc, k, st_a[k] * bc(slot, 3, k))
        return -tree(acc)

    def rwkv_step(s, slot, kk_slot, kk_row, vslot, yslot, sa):
        vv = vslot[0, s]
        acc_y, acc_s = [None] * N_ACC, [None] * N_ACC
        for k in range(HEAD_DIM):
            row = s * HEAD_DIM + k
            new = st_a[k] * bc(slot, 1, row) + sa * bc(slot, 4, row) + vv * bc(slot, 2, row)
            st_a[k] = new
            accumulate(acc_y, k, new * bc(slot, 0, row))
            accumulate(acc_s, k, new * bc(kk_slot, 3, kk_row + k))
        yslot[0, s] = tree(acc_y)
        return -tree(acc_s)

    def hgrn_pair(slot, vslot, yslot):
        v0, v1 = vslot[1, 0], vslot[1, 1]
        acc0, acc1 = [None] * N_ACC, [None] * N_ACC
        for k in range(HEAD_DIM):
            mid = st_b[k] * bc(slot, 6, k) + v0 * bc(slot, 7, k)
            accumulate(acc0, k, mid * bc(slot, 5, k))
            new = mid * bc(slot, 6, HEAD_DIM + k) + v1 * bc(slot, 7, HEAD_DIM + k)
            st_b[k] = new
            accumulate(acc1, k, new * bc(slot, 5, HEAD_DIM + k))
        yslot[1, 0] = tree(acc0)
        yslot[1, 1] = tree(acc1)

    def pair_of_steps(p, j, sa):
        emit(jnp.maximum(p - 1, 0), yslots[(j - 1) % N_SLOTS])
        nxt = (j + LOOKAHEAD) % N_SLOTS
        prepare(jnp.minimum(p + LOOKAHEAD, n_pairs - 1), slots[nxt], vslots[nxt])
        sa = rwkv_step(0, slots[j], slots[j], HEAD_DIM, vslots[j], yslots[j], sa)
        sa = rwkv_step(1, slots[j], slots[(j + 1) % N_SLOTS], 0, vslots[j], yslots[j], sa)
        hgrn_pair(slots[j], vslots[j], yslots[j])
        return sa

    for j in range(LOOKAHEAD):
        prepare(j, slots[j], vslots[j])
    yslots[N_SLOTS - 1][...] = jnp.zeros(yslots[N_SLOTS - 1].shape, F32)

    def body(i, sa):
        for j in range(N_SLOTS):
            sa = pair_of_steps(N_SLOTS * i + j, j, sa)
        return sa

    lax.fori_loop(0, n_pairs // N_SLOTS, body, first_sa(slots[0]))
    emit(n_pairs - 1, yslots[N_SLOTS - 1])

    @pl.when(grid_step == pl.num_programs(0) - 1)
    def _():
        sa_out_ref[...] = st_a[...]
        sb_out_ref[...] = st_b[...]


def _seq(cols_a, va, cols_b, vb, sa0, sb0, bh):
    T = va.shape[0]
    g_sz = bh // 2
    vr = sa0.shape[1]
    tc = min(128, T)
    assert tc % (2 * N_SLOTS) == 0
    view = lambda a: a.reshape(T, g_sz, LANES)
    ospec = pl.BlockSpec((tc, g_sz, LANES), lambda g: (g, 0, 0))
    sspec = pl.BlockSpec((HEAD_DIM, vr, LANES), lambda g: (0, 0, 0))
    args = [view(c) for c in (*cols_a, va, *cols_b, vb)] + [sa0, sb0]
    y_shape = jax.ShapeDtypeStruct((T, g_sz, LANES), F32)
    s_shape = jax.ShapeDtypeStruct((HEAD_DIM, vr, LANES), F32)
    slot = pltpu.VMEM((N_COL, 2 * HEAD_DIM, LANES), F32)
    small = pltpu.VMEM((2, 2, vr, LANES), F32)
    ya, yb, sa, sb = pl.pallas_call(
        functools.partial(_seq_kernel, tc, bh),
        grid=(T // tc,),
        in_specs=[ospec] * (len(args) - 2) + [sspec, sspec],
        out_specs=(ospec, ospec, sspec, sspec),
        out_shape=(y_shape, y_shape, s_shape, s_shape),
        scratch_shapes=[pltpu.VMEM((HEAD_DIM, vr, LANES), F32), pltpu.VMEM((HEAD_DIM, vr, LANES), F32),
                        *([slot] * N_SLOTS + [small] * (2 * N_SLOTS))],
        compiler_params=_cparams(("arbitrary",)),
        name="seq",
    )(*args)
    return ya.reshape(T, bh * HEAD_DIM), yb.reshape(T, bh * HEAD_DIM), sa, sb


def _ret_kernel(L, q_ref, k_ref, v_ref, s0_ref, o_ref, s_out_ref, st, dm, gq, gk):
    i = pl.program_id(1)
    log2_gamma = [float(np.log2(1.0 - 2.0 ** (-5.0 - h))) for h in range(N_HEADS)]

    @pl.when(i == 0)
    def _():
        st[...] = s0_ref[0]
        t_idx = lax.broadcasted_iota(jnp.int32, (L, L), 0)
        s_idx = lax.broadcasted_iota(jnp.int32, (L, L), 1)
        row = lax.broadcasted_iota(jnp.int32, (L, HEAD_DIM), 0).astype(F32)
        for h in range(N_HEADS):
            d = (t_idx - s_idx).astype(F32)
            dm[h] = jnp.where(t_idx >= s_idx, jnp.exp2(d * log2_gamma[h]), 0.0)
            gq[h] = jnp.exp2((row + 1.0) * log2_gamma[h])
            gk[h] = jnp.exp2((L - 1.0 - row) * log2_gamma[h])

    for h in range(N_HEADS):
        hs = slice(h * HEAD_DIM, (h + 1) * HEAD_DIM)
        q, k, v = q_ref[0, :, hs], k_ref[0, :, hs], v_ref[0, :, hs].astype(BF16)
        a = lax.dot_general(q.astype(BF16), k.astype(BF16), (((1,), (1,)), ((), ())),
                            preferred_element_type=F32) * dm[h]
        s_prev = st[h]
        o_ref[0, :, hs] = _dot(a.astype(BF16), v) + _dot((q * gq[h]).astype(BF16), s_prev.astype(BF16))
        kd = (k * gk[h]).astype(BF16)
        st[h] = s_prev * (2.0 ** (L * log2_gamma[h])) + lax.dot_general(
            kd, v, (((0,), (0,)), ((), ())), preferred_element_type=F32)

    @pl.when(i == pl.num_programs(1) - 1)
    def _():
        s_out_ref[0] = st[...]


def _ret(q, k, v, s0):
    B, T, _ = q.shape
    L = min(256, T)
    tspec = pl.BlockSpec((1, L, BW), lambda b, i: (b, i, 0))
    sspec = pl.BlockSpec((1, N_HEADS, HEAD_DIM, HEAD_DIM), lambda b, i: (b, 0, 0, 0))
    return pl.pallas_call(
        functools.partial(_ret_kernel, L),
        grid=(B, T // L),
        in_specs=[tspec, tspec, tspec, sspec],
        out_specs=(tspec, sspec),
        out_shape=(jax.ShapeDtypeStruct((B, T, BW), F32),
                   jax.ShapeDtypeStruct((B, N_HEADS, HEAD_DIM, HEAD_DIM), F32)),
        scratch_shapes=[pltpu.VMEM((N_HEADS, HEAD_DIM, HEAD_DIM), F32), pltpu.VMEM((N_HEADS, L, L), F32),
                        pltpu.VMEM((N_HEADS, L, HEAD_DIM), F32), pltpu.VMEM((N_HEADS, L, HEAD_DIM), F32)],
        compiler_params=_cparams(("arbitrary", "arbitrary")),
        name="ret",
    )(q, k, v, s0)


def _keys_values(lat, kr, wkt_ref, wv_ref, placet):
    lat = lat.astype(BF16)
    nt = (((1,), (1,)), ((), ()))
    kt = (lax.dot_general(wkt_ref[...], lat, nt, preferred_element_type=F32)
          + lax.dot_general(placet, kr.astype(BF16), nt, preferred_element_type=F32)).astype(BF16)
    lane = lax.broadcasted_iota(jnp.int32, (1, QP), 1) % HEAD_PAD
    return kt, jnp.where(lane == V_D, 1.0, _dot(lat, wv_ref[...])).astype(BF16)


def _attn_cached_kernel(past, T, q_ref, plat_ref, pkr_ref, lat_ref, kr_ref, wkt_ref, wv_ref, placet_ref, o_ref,
                        kt_s, v_s):
    tk = past + T
    tk_pad = kt_s.shape[1]
    kt_s[:, 0:past], v_s[0:past, :] = _keys_values(plat_ref[0], pkr_ref[0], wkt_ref, wv_ref, placet_ref[:, 0:ROPE_D])
    kt_s[:, past:tk], v_s[past:tk, :] = _keys_values(lat_ref[0], kr_ref[0], wkt_ref, wv_ref, placet_ref[...])
    if tk_pad != tk:
        kt_s[:, tk:] = jnp.zeros((QP, tk_pad - tk), BF16)
        v_s[tk:, :] = jnp.zeros((tk_pad - tk, QP), BF16)
    qi = past + lax.broadcasted_iota(jnp.int32, (T, tk_pad), 0)
    ki = lax.broadcasted_iota(jnp.int32, (T, tk_pad), 1)
    keep = jnp.logical_and(ki // CHUNK <= qi // CHUNK, ki < tk)
    for h in range(N_HEADS):
        hs = slice(h * HEAD_PAD, (h + 1) * HEAD_PAD)
        s = jnp.where(keep, _dot(q_ref[0, :, hs], kt_s[hs, :]), -jnp.inf)
        p = jnp.exp2(s - jnp.max(s, axis=-1, keepdims=True))
        a = _dot(p.astype(BF16), v_s[:, hs])
        o_ref[0, :, h * V_D:(h + 1) * V_D] = a[:, :V_D] / a[:, V_D:V_D + 1]


def _attn_cached(qp, past_lat, past_kr, lat, kr, wkt, wv, placet):
    B, past, _ = past_lat.shape
    T = lat.shape[1]
    tk_pad = -(-(past + T) // LANES) * LANES
    assert tk_pad * QP * 2 * 2 <= VMEM_LIMIT // 4 and past % LANES == 0
    whole = lambda a: pl.BlockSpec((1,) + a.shape[1:], lambda b: (b, 0, 0))
    full = lambda a: pl.BlockSpec(a.shape, lambda b: (0,) * a.ndim)
    return pl.pallas_call(
        functools.partial(_attn_cached_kernel, past, T),
        grid=(B,),
        in_specs=[whole(qp), whole(past_lat), whole(past_kr), whole(lat), whole(kr), full(wkt), full(wv), full(placet)],
        out_specs=pl.BlockSpec((1, T, BW), lambda b: (b, 0, 0)),
        out_shape=jax.ShapeDtypeStruct((B, T, BW), F32),
        scratch_shapes=[pltpu.VMEM((QP, tk_pad), BF16), pltpu.VMEM((tk_pad, QP), BF16)],
        compiler_params=_cparams(("arbitrary",)),
        name="attn_cached",
    )(qp, past_lat, past_kr, lat, kr, wkt, wv, placet)


A_FIRST, A_LAST, A_MASKED = 1, 2, 4


def _attn_kernel(bq, bk, q0, kv_len, qi_ref, kj_ref, flag_ref, q_ref, kt_ref, v_ref, o_ref, m_ref, acc_ref):
    step = pl.program_id(1)
    i = qi_ref[step]
    j = kj_ref[step]
    flags = flag_ref[step]

    @pl.when((flags & A_FIRST) != 0)
    def _():
        m_ref[...] = jnp.full(m_ref.shape, -jnp.inf, F32)
        acc_ref[...] = jnp.zeros(acc_ref.shape, F32)

    def block(masked):
        if masked:
            qi = q0 + i * bq + lax.broadcasted_iota(jnp.int32, (bq, bk), 0)
            ki = j * bk + lax.broadcasted_iota(jnp.int32, (bq, bk), 1)
            keep = jnp.logical_and(ki // CHUNK <= qi // CHUNK, ki < kv_len)
        for h in range(N_HEADS):
            hs = slice(h * HEAD_PAD, (h + 1) * HEAD_PAD)
            s = _dot(q_ref[0, :, hs], kt_ref[0, hs, :])
            if masked:
                s = jnp.where(keep, s, -jnp.inf)
            m_old = m_ref[h]
            m_new = jnp.maximum(m_old, jnp.max(s, axis=-1, keepdims=True))
            alpha = jnp.exp2(m_old - m_new)
            m_full = jnp.concatenate([m_new] * (bk // LANES), axis=1) if bk % LANES == 0 else m_new[:, 0:1]
            p = jnp.exp2(s - m_full)
            acc_ref[:, hs] = alpha * acc_ref[:, hs] + _dot(p.astype(BF16), v_ref[0, :, hs])
            m_ref[h] = m_new

    @pl.when((flags & A_MASKED) == 0)
    def _():
        block(False)

    @pl.when((flags & A_MASKED) != 0)
    def _():
        block(True)

    @pl.when((flags & A_LAST) != 0)
    def _():
        for h in range(N_HEADS):
            a = acc_ref[:, h * HEAD_PAD:(h + 1) * HEAD_PAD]
            o_ref[0, :, h * V_D:(h + 1) * V_D] = a[:, :V_D] / a[:, V_D:V_D + 1]


def _attn(qp, ktp, vp, q0, kv_len, bk):
    B, T, _ = qp.shape
    tk = vp.shape[1]
    bq = min(1024, T)
    nq, nk = T // bq, tk // bk
    qi, kj, flags = [], [], []
    for i in range(nq):
        first_q, last_q = q0 + i * bq, q0 + (i + 1) * bq - 1
        n_blocks = min(nk, (min(kv_len, (last_q // CHUNK + 1) * CHUNK) - 1) // bk + 1)
        for j in range(n_blocks):
            fully_visible = (j + 1) * bk <= min(kv_len, (first_q // CHUNK + 1) * CHUNK)
            qi.append(i)
            kj.append(j)
            flags.append((A_FIRST if j == 0 else 0) | (A_LAST if j == n_blocks - 1 else 0)
                         | (0 if fully_visible else A_MASKED))
    sched = [jnp.asarray(np.asarray(a, np.int32)) for a in (qi, kj, flags)]
    grid_spec = pltpu.PrefetchScalarGridSpec(
        num_scalar_prefetch=3,
        grid=(B, len(qi)),
        in_specs=[pl.BlockSpec((1, bq, QP), lambda b, s, qi, kj, fl: (b, qi[s], 0)),
                  pl.BlockSpec((1, QP, bk), lambda b, s, qi, kj, fl: (b, 0, kj[s])),
                  pl.BlockSpec((1, bk, QP), lambda b, s, qi, kj, fl: (b, kj[s], 0))],
        out_specs=pl.BlockSpec((1, bq, BW), lambda b, s, qi, kj, fl: (b, qi[s], 0)),
        scratch_shapes=[pltpu.VMEM((N_HEADS, bq, LANES), F32), pltpu.VMEM((bq, QP), F32)])
    return pl.pallas_call(
        functools.partial(_attn_kernel, bq, bk, q0, kv_len),
        grid_spec=grid_spec,
        out_shape=jax.ShapeDtypeStruct((B, T, BW), F32),
        compiler_params=_cparams(("arbitrary", "arbitrary")),
        name="attn",
    )(*sched, qp, ktp, vp)


def _merge_kernel(final, h_ref, ya_ref, posta_ref, ob_ref, gb_ref, oc_ref, gc_ref, od_ref, gd_ref,
                  par_ref, seg_ref, wg_ref, wb_ref, wo_ref, out_ref):
    par = lambda r, n: par_ref[r:r + 1, 0:n]
    seg = seg_ref[...]
    h = h_ref[0]
    u = _rms(h, par(R_NORM, D_MODEL), NORM_EPS).astype(BF16)
    inv = 1.0 / HEAD_DIM

    ya = ya_ref[...]
    xc = ya - _dot_lhs2(ya, seg) * inv
    yn = xc * lax.rsqrt(_dot_lhs2(xc * xc, seg) * inv + GN_EPS) * par(R_LNW, BW) + par(R_LNB, BW)
    oa = (yn + posta_ref[:, :BW]) * posta_ref[:, BW:]

    def head_rms(o, g):
        return o * lax.rsqrt(_dot_lhs2(o * o, seg) * inv + HEAD_NORM_EPS) * g

    ob = head_rms(ob_ref[...], par(R_HG, BW)) * gb_ref[...]
    oc = oc_ref[0] * gc_ref[0]
    od = head_rms(od_ref[0], par(R_RG, BW)) * gd_ref[0]

    merged = None
    for n, o in enumerate((oa, ob, oc, od)):
        gate = _sigmoid(_dot(u, wg_ref[:, n * D_MODEL:(n + 1) * D_MODEL]))
        term = gate * _dot(o.astype(BF16), wb_ref[n * BW:(n + 1) * BW, :])
        merged = term if merged is None else merged + term
    out = h + _dot(merged.astype(BF16), wo_ref[...])
    if final:
        out = _rms(out, par(R_FG, D_MODEL), NORM_EPS)
    out_ref[0] = out


def _merge(final, h, ya, posta, ob, gb, oc, gc, od, gd, par, seg, wg, wb, wo):
    B, T, _ = h.shape
    tt = min(512, T)
    bspec = lambda c: pl.BlockSpec((1, tt, c), lambda t, b: (b, t, 0))
    tspec = lambda c: pl.BlockSpec((tt, c), lambda t, b: (t, b))
    full = lambda a: pl.BlockSpec(a.shape, lambda t, b: (0,) * a.ndim)
    consts = (par, seg, wg, wb, wo)
    return pl.pallas_call(
        functools.partial(_merge_kernel, final),
        grid=(T // tt, B),
        in_specs=[bspec(D_MODEL), tspec(BW), tspec(2 * BW), tspec(BW), tspec(BW), bspec(BW), bspec(BW), bspec(BW),
                  bspec(BW)] + [full(a) for a in consts],
        out_specs=bspec(D_MODEL),
        out_shape=jax.ShapeDtypeStruct((B, T, D_MODEL), F32),
        compiler_params=_cparams(("arbitrary", "arbitrary")),
        name="merge",
    )(h, ya, posta, ob, gb, oc, gc, od, gd, *consts)


def _rope_tables(past, T):
    pos = (past + jnp.arange(T, dtype=jnp.int32)).astype(F32)[:, None]

    def tables(width, group, start, d):
        lane = np.arange(width) % group - start
        on = (lane >= 0) & (lane < d)
        idx = np.where(on, lane % (d // 2), 0)
        inv_freq = jnp.power(ROPE_BASE, -jnp.arange(0, d, 2, dtype=F32) / d)
        ang = pos * inv_freq[None, :]
        cos = jnp.where(on[None, :], jnp.cos(ang)[:, idx], 1.0)
        sign = np.where(lane < d // 2, -1.0, 1.0).astype(np.float32)
        sin = jnp.where(on[None, :], jnp.sin(ang)[:, idx] * sign[None, :], 0.0)
        return cos, sin

    cd, sd = tables(LANES, HEAD_DIM, 0, HEAD_DIM)
    cq, sq = tables(LANES, HEAD_PAD, NOPE, ROPE_D)
    ck, sk = tables(LANES, LANES, 0, ROPE_D)
    return jnp.concatenate([cd, sd, cq, sq, ck, sk], axis=1)


def _layer_weights(l, p):
    w_in = p["w_in"][l]
    gate_cols = N_BRANCH * D_MODEL
    c_kr = gate_cols + SHIFT_COLS + BW + 4 * BW + Q_LORA + KV_LORA
    c_g = c_kr + ROPE_D
    w_rest = jnp.concatenate([
        w_in[:, gate_cols:c_kr],
        jnp.pad(w_in[:, c_kr:c_g], ((0, 0), (0, LANES - ROPE_D))),
        w_in[:, c_g:],
    ], axis=1).astype(BF16)
    wg = w_in[:, :gate_cols].astype(BF16)

    z = jnp.zeros((64, BW), F32)
    wl = jnp.concatenate([jnp.concatenate([p["rwkv_w_up"][l], z], axis=1),
                          jnp.concatenate([z, p["rwkv_a_up"][l]], axis=1)], axis=0)
    wl_hi = wl.astype(BF16)
    wl_lo = (wl - wl_hi.astype(F32)).astype(BF16)

    wq = p["mla_w_q_up"][l].reshape(Q_LORA, N_HEADS, QK_D)
    wq = jnp.pad(wq, ((0, 0), (0, 0), (0, HEAD_PAD - QK_D))).reshape(Q_LORA, QP).astype(BF16)
    wkv = p["mla_w_kv_up"][l].reshape(KV_LORA, N_HEADS, NOPE + V_D)
    wkt = jnp.pad(wkv[:, :, :NOPE], ((0, 0), (0, 0), (0, HEAD_PAD - NOPE))).reshape(KV_LORA, QP).T.astype(BF16)
    wv = jnp.pad(wkv[:, :, NOPE:], ((0, 0), (0, 0), (0, HEAD_PAD - V_D))).reshape(KV_LORA, QP).astype(BF16)

    wb = p["w_branch"][l].reshape(N_BRANCH * BW, D_MODEL).astype(BF16)
    wo = p["w_out"][l].astype(BF16)

    return dict(w_rest=w_rest, wg=wg, wl_hi=wl_hi, wl_lo=wl_lo, wq=wq, wkt=wkt, wv=wv, wb=wb, wo=wo)


def _param_tables(p):
    def rows(v, n=1):
        v = v.reshape(DEPTH, n, -1).astype(F32)
        return jnp.pad(v, ((0, 0), (0, 0), (0, D_MODEL - v.shape[-1])))

    every = lambda v, n: jnp.broadcast_to(v.reshape(1, n, -1), (DEPTH, n, v.shape[-1]))
    parts = [rows(p["norm_g"]), rows(p["rwkv_mu"]), rows(p["rwkv_w0"]), rows(p["rwkv_a0"]), rows(p["rwkv_k_k"]),
             rows(p["rwkv_k_a"]), rows(p["rwkv_r_k"]), rows(every(p["hgrn_lb_logits"], DEPTH), DEPTH),
             rows(p["mla_q_norm_g"]), rows(p["mla_kv_norm_g"]), rows(p["rwkv_ln_w"]), rows(p["rwkv_ln_b"]),
             rows(p["hgrn_norm_g"]), rows(p["ret_norm_g"]), rows(every(p["final_norm_g"], 1))]
    used = sum(a.shape[1] for a in parts)
    return jnp.concatenate(parts + [jnp.zeros((DEPTH, N_ROWS - used, D_MODEL), F32)], axis=1)


def _all_weights(p):
    tables = _param_tables(p)
    return [dict(_layer_weights(l, p), par=tables[l]) for l in range(DEPTH)]


def _constants():
    head = np.arange(BW) // HEAD_DIM
    seg = jnp.asarray((head[:, None] == head[None, :]).astype(np.float32), dtype=BF16)
    place = np.zeros((LANES, QP), np.float32)
    for h in range(N_HEADS):
        for j in range(ROPE_D):
            place[j, h * HEAD_PAD + NOPE + j] = 1.0
    return seg, jnp.asarray(place.T, dtype=BF16)


def _state_in(s, B, rep, key_last):
    vr = HEAD_DIM // rep
    if key_last:
        y = s.reshape(B, 2, 2, rep, vr, HEAD_DIM).transpose(5, 4, 3, 2, 0, 1)
    else:
        y = s.reshape(B, 2, 2, HEAD_DIM, rep, vr).transpose(3, 5, 4, 2, 0, 1)
    return y.reshape(HEAD_DIM, vr, LANES).astype(F32)


def _state_out(y, B, rep, key_last):
    vr = HEAD_DIM // rep
    y = y.reshape(HEAD_DIM, vr, rep, 2, B, 2)
    if key_last:
        y = y.transpose(4, 5, 3, 2, 1, 0)
    else:
        y = y.transpose(4, 5, 3, 0, 2, 1)
    return y.reshape(B, N_HEADS, HEAD_DIM, HEAD_DIM)


def _run_trunk(x, shift0, rwkv0, hgrn0, ret0, lat0, kr0, p, weights, consts):
    B, T, _ = x.shape
    past = lat0.shape[2]
    bh = B * N_HEADS
    rep = LANES // bh
    assert rep * bh == LANES and HEAD_DIM % rep == 0 and (HEAD_DIM // rep) % SUBLANES == 0 and N_HEADS == 4
    seg, placet = consts
    tab = _rope_tables(past, T)
    h = x
    per_layer = []
    for l in range(DEPTH):
        w = weights[l]
        (ar, aw, ak, av, akk, ab, posta, bq, bf, bv, gb, qp, lat, kr, gc, ktp, vp, dq, dk, dv, gd, shn) = _proj(
            l, h, shift0[l][:, None, :], tab, w["par"], w["w_rest"], w["wl_hi"], w["wl_lo"], seg, w["wq"],
            w["wkt"], w["wv"], placet)

        ya, ob, s_rwkv, s_hgrn = _seq([ar, aw, ak, akk, ab], av, [bq, bf], bv, _state_in(rwkv0[l], B, rep, True),
                                      _state_in(hgrn0[l], B, rep, False), bh)
        od, s_ret = _ret(dq, dk, dv, ret0[l].astype(F32))

        if past == 0:
            oc = _attn(qp, ktp, vp, 0, T, min(512, T))
        else:
            oc = _attn_cached(qp, lat0[l], kr0[l], lat, kr, w["wkt"], w["wv"], placet)

        h = _merge(l == DEPTH - 1, h, ya, posta, ob, gb, oc, gc, od, gd, w["par"], seg, w["wg"], w["wb"], w["wo"])
        per_layer.append((shn[-1, :, 0, :], _state_out(s_rwkv, B, rep, True), _state_out(s_hgrn, B, rep, False),
                          s_ret, lat, kr[:, :, :ROPE_D]))
    stacked = [jnp.stack([st[j] for st in per_layer]) for j in range(6)]
    return h, stacked


def kernel(x_prompt, x_sample, state_rwkv_shift, state_rwkv, state_hgrn, cache_mla_latent, cache_mla_krope, state_ret, norm_g, w_in, rwkv_mu, rwkv_w0, rwkv_w_up, rwkv_a0, rwkv_a_up, rwkv_k_k, rwkv_k_a, rwkv_r_k, rwkv_ln_w, rwkv_ln_b, hgrn_lb_logits, hgrn_norm_g, mla_q_norm_g, mla_w_q_up, mla_kv_norm_g, mla_w_kv_up, ret_norm_g, w_branch, w_out, final_norm_g):
    p = dict(norm_g=norm_g, w_in=w_in, rwkv_mu=rwkv_mu, rwkv_w0=rwkv_w0, rwkv_w_up=rwkv_w_up, rwkv_a0=rwkv_a0,
             rwkv_a_up=rwkv_a_up, rwkv_k_k=rwkv_k_k, rwkv_k_a=rwkv_k_a, rwkv_r_k=rwkv_r_k, rwkv_ln_w=rwkv_ln_w,
             rwkv_ln_b=rwkv_ln_b, hgrn_lb_logits=hgrn_lb_logits, hgrn_norm_g=hgrn_norm_g, mla_q_norm_g=mla_q_norm_g,
             mla_w_q_up=mla_w_q_up, mla_kv_norm_g=mla_kv_norm_g, mla_w_kv_up=mla_w_kv_up, ret_norm_g=ret_norm_g,
             w_branch=w_branch, w_out=w_out, final_norm_g=final_norm_g)
    weights = _all_weights(p)
    consts = _constants()
    bp, dt = x_prompt.shape[0], x_prompt.dtype
    zs = jnp.zeros((DEPTH, bp, N_HEADS, HEAD_DIM, HEAD_DIM), dt)
    y_p, (p_shift, p_rwkv, p_hgrn, p_ret, p_lat, p_kr) = _run_trunk(
        x_prompt, jnp.zeros((DEPTH, bp, SHIFT_COLS), dt), zs, zs, zs, jnp.zeros((DEPTH, bp, 0, KV_LORA), dt),
        jnp.zeros((DEPTH, bp, 0, ROPE_D), dt), p, weights, consts)
    y_s, (s_shift, s_rwkv, s_hgrn, s_ret, s_lat, s_kr) = _run_trunk(
        x_sample, state_rwkv_shift, state_rwkv, state_hgrn, state_ret, cache_mla_latent, cache_mla_krope,
        p, weights, consts)
    return (y_p, y_s, p_shift, s_shift, p_rwkv, s_rwkv, p_hgrn, s_hgrn, p_lat, s_lat, p_kr, s_kr, p_ret, s_ret)
```

```python
import functools

import jax
import jax.numpy as jnp
import numpy as np
from jax import lax
from jax.experimental import pallas as pl
from jax.experimental.pallas import tpu as pltpu

F32 = jnp.float32
BF16 = jnp.bfloat16

D_MODEL = 1024
DEPTH = 4
N_HEADS = 4
HEAD_DIM = 64
BW = N_HEADS * HEAD_DIM
N_BRANCH = 4
CHUNK = 64
SHIFT_COLS = 3 * BW + 64 + 64
GN_EPS = 64e-5
NORM_EPS = 1e-6
HEAD_NORM_EPS = 1e-5
ROPE_BASE = 10000.0
Q_LORA, KV_LORA, NOPE, ROPE_D, V_D = 256, 128, 64, 32, 64
QK_D = NOPE + ROPE_D
HEAD_PAD = 128
Q_SCALE = QK_D ** -0.5 * 1.4426950408889634
QP = N_HEADS * HEAD_PAD
LANES = 128
SUBLANES = 8

A0, A1 = 0, SHIFT_COLS + BW
B0, B1 = A1, A1 + 4 * BW
C0, C1 = B1, B1 + Q_LORA + KV_LORA + LANES + BW
D0, D1 = C1, C1 + 4 * BW
PROJ_COLS = D1

T_CD, T_SD, T_CQ, T_SQ, T_CK, T_SK, TAB_COLS = 0, 128, 256, 384, 512, 640, 768

(R_NORM, R_MU, R_W0, R_A0, R_KK, R_KA, R_RK, R_LB, R_QG, R_KVG,
 R_LNW, R_LNB, R_HG, R_RG, R_FG, N_ROWS) = (0, 1, 2, 3, 4, 5, 6, 7, 11, 12, 13, 14, 15, 16, 17, 24)

VMEM_LIMIT = 56 * 1024 * 1024


def _cparams(sem):
    return pltpu.CompilerParams(dimension_semantics=sem, vmem_limit_bytes=VMEM_LIMIT)


def _sigmoid(x):
    return 1.0 / (1.0 + jnp.exp(-x))


def _silu(x):
    return x * _sigmoid(x)


def _softplus(x):
    return jnp.maximum(x, 0.0) + jnp.log(1.0 + jnp.exp(-jnp.abs(x)))


def _dot(a, b):
    return jnp.dot(a, b, preferred_element_type=F32)


def _split(x):
    hi = x.astype(BF16)
    lo = (x - hi.astype(F32)).astype(BF16)
    return hi, lo


def _dot_lhs2(x, m_bf16):
    hi, lo = _split(x)
    return _dot(hi, m_bf16) + _dot(lo, m_bf16)


def _dot3(x, m_hi, m_lo):
    hi, lo = _split(x)
    return _dot(hi, m_hi) + _dot(lo, m_hi) + _dot(hi, m_lo)


def _rms(x, g, eps):
    return x * lax.rsqrt(jnp.mean(x * x, axis=-1, keepdims=True) + eps) * g


def _rope(x, cos, sin_signed, half, group, start=0):
    w = x.shape[-1]
    lane = lax.broadcasted_iota(jnp.int32, (1, w), 1) % group
    up = pltpu.roll(x, w - half, axis=1)
    dn = pltpu.roll(x, half, axis=1)
    return x * cos + jnp.where(lane < start + half, up, dn) * sin_signed


def _proj_kernel(layer, tt, h_ref, sh0_ref, tab_ref, par_ref, w_ref, wl_hi_ref, wl_lo_ref, seg_ref, wq_ref,
                 wkt_ref, wv_ref, placet_ref,
                 ar_ref, aw_ref, ak_ref, av_ref, akk_ref, ab_ref, posta_ref,
                 bq_ref, bf_ref, bv_ref, gb_ref,
                 qp_ref, lat_ref, kr_ref, gc_ref, kt_ref, vkv_ref,
                 dq_ref, dk_ref, dv_ref, gd_ref, shn_ref, carry_ref):
    t = pl.program_id(0)
    b = pl.program_id(1)
    par = lambda r, n: par_ref[r:r + 1, 0:n]
    seg = seg_ref[...]

    u = _rms(h_ref[0], par(R_NORM, D_MODEL), NORM_EPS).astype(BF16)

    pa_all = _dot(u, w_ref[:, A0:A1])
    pa = pa_all[:, :SHIFT_COLS]
    prev_row = jnp.where(t == 0, sh0_ref[b], carry_ref[b])
    row = lax.broadcasted_iota(jnp.int32, (tt, 1), 0)
    p_prev = jnp.where(row == 0, prev_row, pltpu.roll(pa, 1, axis=0))
    last = pa[tt - 1:tt, :]
    carry_ref[b] = last
    shn_ref[0, 0] = last
    xs = pa + (p_prev - pa) * par(R_MU, SHIFT_COLS)
    r, k, v, wdad = xs[:, :BW], xs[:, BW:2 * BW], xs[:, 2 * BW:3 * BW], xs[:, 3 * BW:]
    lane = lax.broadcasted_iota(jnp.int32, (1, LANES), 1)
    lora = _dot3(jnp.where(lane < 64, jnp.tanh(wdad), wdad), wl_hi_ref[...], wl_lo_ref[...])
    w_raw = -_softplus(-(par(R_W0, BW) + lora[:, :BW])) - 0.5
    a = _sigmoid(par(R_A0, BW) + lora[:, BW:])
    kk = k * par(R_KK, BW)
    kk = kk / jnp.maximum(jnp.sqrt(_dot_lhs2(kk * kk, seg)), 1e-12)
    k2 = k * (1.0 + (a - 1.0) * par(R_KA, BW))
    ar_ref[...] = r
    aw_ref[...] = jnp.exp(-jnp.exp(w_raw))
    ak_ref[...] = k2
    av_ref[...] = v
    akk_ref[...] = kk
    ab_ref[...] = kk * a
    posta_ref[:, :BW] = _dot_lhs2(r * k2 * par(R_RK, BW), seg) * v
    posta_ref[:, BW:] = _silu(pa_all[:, SHIFT_COLS:])

    pb = _dot(u, w_ref[:, B0:B1])
    lg = par_ref[R_LB:R_LB + DEPTH, 0:BW]
    e = jnp.exp(lg - jnp.max(lg, axis=0, keepdims=True))
    lb = jnp.zeros((1, BW), F32)
    for j in range(1, layer + 1):
        lb = lb + e[j:j + 1]
    lb = lb / jnp.sum(e, axis=0, keepdims=True)
    z = pb[:, BW:2 * BW]
    bq_ref[...] = _silu(pb[:, :BW]) * (HEAD_DIM ** -0.5)
    bf_ref[...] = lb + (1.0 - lb) * _sigmoid(z)
    bv_ref[...] = pb[:, 2 * BW:3 * BW]
    gb_ref[...] = _silu(pb[:, 3 * BW:])

    pc = _dot(u, w_ref[:, C0:C1])
    qn = _rms(pc[:, :Q_LORA], par(R_QG, Q_LORA), NORM_EPS).astype(BF16)
    q = _dot(qn, wq_ref[...]) * Q_SCALE
    tab = lambda c, n: jnp.concatenate([tab_ref[:, c:c + LANES]] * n, axis=1)
    qp_ref[0] = _rope(q, tab(T_CQ, N_HEADS), tab(T_SQ, N_HEADS), ROPE_D // 2, HEAD_PAD, NOPE).astype(BF16)
    lat = _rms(pc[:, Q_LORA:Q_LORA + KV_LORA], par(R_KVG, KV_LORA), NORM_EPS)
    kr = _rope(pc[:, Q_LORA + KV_LORA:Q_LORA + KV_LORA + LANES], tab_ref[:, T_CK:T_CK + LANES],
               tab_ref[:, T_SK:T_SK + LANES], ROPE_D // 2, LANES)
    lat_ref[0] = lat
    kr_ref[0] = kr
    kt_ref[0], vkv_ref[0] = _keys_values(lat, kr, wkt_ref, wv_ref, placet_ref[...])
    gc_ref[0] = _silu(pc[:, Q_LORA + KV_LORA + LANES:])

    pd = _dot(u, w_ref[:, D0:D1])
    cos, sin = tab(T_CD, BW // LANES), tab(T_SD, BW // LANES)
    dq_ref[0] = _rope(pd[:, :BW], cos, sin, HEAD_DIM // 2, HEAD_DIM)
    dk_ref[0] = _rope(pd[:, BW:2 * BW], cos, sin, HEAD_DIM // 2, HEAD_DIM) * (HEAD_DIM ** -0.5)
    dv_ref[0] = pd[:, 2 * BW:3 * BW]
    gd_ref[0] = _silu(pd[:, 3 * BW:])


def _proj(layer, h, shift0, tab, par, w_rest, wl_hi, wl_lo, seg, wq, wkt, wv, placet):
    B, T, _ = h.shape
    tt = min(512, T)
    nt = T // tt
    bspec = lambda c: pl.BlockSpec((1, tt, c), lambda t, b: (b, t, 0))
    tspec = lambda c: pl.BlockSpec((tt, c), lambda t, b: (t, b))
    full = lambda a: pl.BlockSpec(a.shape, lambda t, b: (0,) * a.ndim)
    bm = lambda c, dt=F32: jax.ShapeDtypeStruct((B, T, c), dt)
    tm = lambda c: jax.ShapeDtypeStruct((T, B * c), F32)
    outs = ([(tm(BW), tspec(BW))] * 6 + [(tm(2 * BW), tspec(2 * BW))]
            + [(tm(BW), tspec(BW))] * 4
            + [(bm(QP, BF16), bspec(QP)), (bm(KV_LORA), bspec(KV_LORA)), (bm(LANES), bspec(LANES)),
               (bm(BW), bspec(BW)),
               (jax.ShapeDtypeStruct((B, QP, T), BF16), pl.BlockSpec((1, QP, tt), lambda t, b: (b, 0, t))),
               (bm(QP, BF16), bspec(QP))]
            + [(bm(BW), bspec(BW))] * 4
            + [(jax.ShapeDtypeStruct((nt, B, 1, SHIFT_COLS), F32),
                pl.BlockSpec((1, 1, 1, SHIFT_COLS), lambda t, b: (t, b, 0, 0)))])
    return pl.pallas_call(
        functools.partial(_proj_kernel, layer, tt),
        grid=(nt, B),
        in_specs=[bspec(D_MODEL), full(shift0), pl.BlockSpec((tt, TAB_COLS), lambda t, b: (t, 0)), full(par),
                  full(w_rest), full(wl_hi), full(wl_lo), full(seg), full(wq), full(wkt), full(wv), full(placet)],
        out_specs=tuple(o[1] for o in outs),
        out_shape=tuple(o[0] for o in outs),
        scratch_shapes=[pltpu.VMEM((B, 1, SHIFT_COLS), F32)],
        compiler_params=_cparams(("arbitrary", "arbitrary")),
        name="proj",
    )(h, shift0, tab, par, w_rest, wl_hi, wl_lo, seg, wq, wkt, wv, placet)


N_ACC = 1


N_COL = 8
LOOKAHEAD = 2
N_SLOTS = 4


def _seq_kernel(tc, bh, r_ref, w_ref, k_ref, kk_ref, b_ref, va_ref, q_ref, f_ref, vb_ref, sa0_ref, sb0_ref,
                ya_ref, yb_ref, sa_out_ref, sb_out_ref, st_a, st_b, *ring):
    slots, vslots, yslots = ring[0:N_SLOTS], ring[N_SLOTS:2 * N_SLOTS], ring[2 * N_SLOTS:3 * N_SLOTS]
    g_sz = bh // 2
    rep = LANES // bh
    vr = HEAD_DIM // rep
    n_pairs = tc // 2
    lane = lax.broadcasted_iota(jnp.int32, (1, LANES), 1)
    low = lane < HEAD_DIM
    hi_id = lane // bh
    grid_step = pl.program_id(0)

    @pl.when(grid_step == 0)
    def _():
        st_a[...] = sa0_ref[...]
        st_b[...] = sb0_ref[...]

    def paired(ref, p):
        a, b = ref[2 * p], ref[2 * p + 1]
        s0 = jnp.where(low, a, pltpu.roll(b, HEAD_DIM, axis=1))
        s1 = jnp.where(low, pltpu.roll(a, HEAD_DIM, axis=1), b)
        return jnp.concatenate([s0, s1] * rep, axis=0).T

    def prepare(p, slot, vslot):
        for o, ref in enumerate((r_ref, w_ref, k_ref, kk_ref, b_ref, q_ref, f_ref)):
            xt = paired(ref, p)
            slot[o] = xt
            if ref is f_ref:
                slot[o + 1] = 1.0 - xt
        for ri, ref in enumerate((va_ref, vb_ref)):
            xt = paired(ref, p)
            for s in range(2):
                vv = xt[s * HEAD_DIM:s * HEAD_DIM + vr]
                for g in range(1, rep):
                    vv = jnp.where(hi_id == g, xt[s * HEAD_DIM + g * vr:s * HEAD_DIM + (g + 1) * vr], vv)
                vslot[ri, s] = vv

    def emit(p, yslot):
        for ri, y_ref in enumerate((ya_ref, yb_ref)):
            blocks = [jnp.where(hi_id == g, yslot[ri, s], 0.0) for s in range(2) for g in range(rep)]
            zt = jnp.concatenate(blocks, axis=0).T
            yp = zt[0:bh]
            for g in range(1, rep):
                yp = yp + zt[g * bh:(g + 1) * bh]
            p0, p1 = yp[:g_sz], yp[g_sz:]
            y_ref[2 * p] = jnp.where(low, p0, pltpu.roll(p1, HEAD_DIM, axis=1))
            y_ref[2 * p + 1] = jnp.where(low, pltpu.roll(p0, HEAD_DIM, axis=1), p1)

    def tree(xs):
        while len(xs) > 1:
            xs = [xs[i] + xs[i + 1] for i in range(0, len(xs), 2)]
        return xs[0]

    def bc(slot, o, row):
        return jnp.broadcast_to(slot[o, pl.ds(row, 1), :], (vr, LANES))

    def accumulate(acc, k, p):
        acc[k % N_ACC] = p if acc[k % N_ACC] is None else acc[k % N_ACC] + p

    def first_sa(slot):
        acc = [None] * N_ACC
        for k in range(HEAD_DIM):
            accumulate(acc, k, st_a[k] * bc(slot, 3, k))
        return -tree(acc)

    def rwkv_step(s, slot, kk_slot, kk_row, vslot, yslot, sa):
        vv = vslot[0, s]
        acc_y, acc_s = [None] * N_ACC, [None] * N_ACC
        for k in range(HEAD_DIM):
            row = s * HEAD_DIM + k
            new = st_a[k] * bc(slot, 1, row) + sa * bc(slot, 4, row) + vv * bc(slot, 2, row)
            st_a[k] = new
            accumulate(acc_y, k, new * bc(slot, 0, row))
            accumulate(acc_s, k, new * bc(kk_slot, 3, kk_row + k))
        yslot[0, s] = tree(acc_y)
        return -tree(acc_s)

    def hgrn_pair(slot, vslot, yslot):
        v0, v1 = vslot[1, 0], vslot[1, 1]
        acc0, acc1 = [None] * N_ACC, [None] * N_ACC
        for k in range(HEAD_DIM):
            mid = st_b[k] * bc(slot, 6, k) + v0 * bc(slot, 7, k)
            accumulate(acc0, k, mid * bc(slot, 5, k))
            new = mid * bc(slot, 6, HEAD_DIM + k) + v1 * bc(slot, 7, HEAD_DIM + k)
            st_b[k] = new
            accumulate(acc1, k, new * bc(slot, 5, HEAD_DIM + k))
        yslot[1, 0] = tree(acc0)
        yslot[1, 1] = tree(acc1)

    def pair_of_steps(p, j, sa):
        emit(jnp.maximum(p - 1, 0), yslots[(j - 1) % N_SLOTS])
        nxt = (j + LOOKAHEAD) % N_SLOTS
        prepare(jnp.minimum(p + LOOKAHEAD, n_pairs - 1), slots[nxt], vslots[nxt])
        sa = rwkv_step(0, slots[j], slots[j], HEAD_DIM, vslots[j], yslots[j], sa)
        sa = rwkv_step(1, slots[j], slots[(j + 1) % N_SLOTS], 0, vslots[j], yslots[j], sa)
        hgrn_pair(slots[j], vslots[j], yslots[j])
        return sa

    for j in range(LOOKAHEAD):
        prepare(j, slots[j], vslots[j])
    yslots[N_SLOTS - 1][...] = jnp.zeros(yslots[N_SLOTS - 1].shape, F32)

    def body(i, sa):
        for j in range(N_SLOTS):
            sa = pair_of_steps(N_SLOTS * i + j, j, sa)
        return sa

    lax.fori_loop(0, n_pairs // N_SLOTS, body, first_sa(slots[0]))
    emit(n_pairs - 1, yslots[N_SLOTS - 1])

    @pl.when(grid_step == pl.num_programs(0) - 1)
    def _():
        sa_out_ref[...] = st_a[...]
        sb_out_ref[...] = st_b[...]


def _seq(cols_a, va, cols_b, vb, sa0, sb0, bh):
    T = va.shape[0]
    g_sz = bh // 2
    vr = sa0.shape[1]
    tc = min(128, T)
    assert tc % (2 * N_SLOTS) == 0
    view = lambda a: a.reshape(T, g_sz, LANES)
    ospec = pl.BlockSpec((tc, g_sz, LANES), lambda g: (g, 0, 0))
    sspec = pl.BlockSpec((HEAD_DIM, vr, LANES), lambda g: (0, 0, 0))
    args = [view(c) for c in (*cols_a, va, *cols_b, vb)] + [sa0, sb0]
    y_shape = jax.ShapeDtypeStruct((T, g_sz, LANES), F32)
    s_shape = jax.ShapeDtypeStruct((HEAD_DIM, vr, LANES), F32)
    slot = pltpu.VMEM((N_COL, 2 * HEAD_DIM, LANES), F32)
    small = pltpu.VMEM((2, 2, vr, LANES), F32)
    ya, yb, sa, sb = pl.pallas_call(
        functools.partial(_seq_kernel, tc, bh),
        grid=(T // tc,),
        in_specs=[ospec] * (len(args) - 2) + [sspec, sspec],
        out_specs=(ospec, ospec, sspec, sspec),
        out_shape=(y_shape, y_shape, s_shape, s_shape),
        scratch_shapes=[pltpu.VMEM((HEAD_DIM, vr, LANES), F32), pltpu.VMEM((HEAD_DIM, vr, LANES), F32),
                        *([slot] * N_SLOTS + [small] * (2 * N_SLOTS))],
        compiler_params=_cparams(("arbitrary",)),
        name="seq",
    )(*args)
    return ya.reshape(T, bh * HEAD_DIM), yb.reshape(T, bh * HEAD_DIM), sa, sb


def _ret_kernel(L, q_ref, k_ref, v_ref, s0_ref, o_ref, s_out_ref, st, dm, gq, gk):
    i = pl.program_id(1)
    log2_gamma = [float(np.log2(1.0 - 2.0 ** (-5.0 - h))) for h in range(N_HEADS)]

    @pl.when(i == 0)
    def _():
        st[...] = s0_ref[0]
        t_idx = lax.broadcasted_iota(jnp.int32, (L, L), 0)
        s_idx = lax.broadcasted_iota(jnp.int32, (L, L), 1)
        row = lax.broadcasted_iota(jnp.int32, (L, HEAD_DIM), 0).astype(F32)
        for h in range(N_HEADS):
            d = (t_idx - s_idx).astype(F32)
            dm[h] = jnp.where(t_idx >= s_idx, jnp.exp2(d * log2_gamma[h]), 0.0)
            gq[h] = jnp.exp2((row + 1.0) * log2_gamma[h])
            gk[h] = jnp.exp2((L - 1.0 - row) * log2_gamma[h])

    for h in range(N_HEADS):
        hs = slice(h * HEAD_DIM, (h + 1) * HEAD_DIM)
        q, k, v = q_ref[0, :, hs], k_ref[0, :, hs], v_ref[0, :, hs].astype(BF16)
        a = lax.dot_general(q.astype(BF16), k.astype(BF16), (((1,), (1,)), ((), ())),
                            preferred_element_type=F32) * dm[h]
        s_prev = st[h]
        o_ref[0, :, hs] = _dot(a.astype(BF16), v) + _dot((q * gq[h]).astype(BF16), s_prev.astype(BF16))
        kd = (k * gk[h]).astype(BF16)
        st[h] = s_prev * (2.0 ** (L * log2_gamma[h])) + lax.dot_general(
            kd, v, (((0,), (0,)), ((), ())), preferred_element_type=F32)

    @pl.when(i == pl.num_programs(1) - 1)
    def _():
        s_out_ref[0] = st[...]


def _ret(q, k, v, s0):
    B, T, _ = q.shape
    L = min(512, T)
    tspec = pl.BlockSpec((1, L, BW), lambda b, i: (b, i, 0))
    sspec = pl.BlockSpec((1, N_HEADS, HEAD_DIM, HEAD_DIM), lambda b, i: (b, 0, 0, 0))
    return pl.pallas_call(
        functools.partial(_ret_kernel, L),
        grid=(B, T // L),
        in_specs=[tspec, tspec, tspec, sspec],
        out_specs=(tspec, sspec),
        out_shape=(jax.ShapeDtypeStruct((B, T, BW), F32),
                   jax.ShapeDtypeStruct((B, N_HEADS, HEAD_DIM, HEAD_DIM), F32)),
        scratch_shapes=[pltpu.VMEM((N_HEADS, HEAD_DIM, HEAD_DIM), F32), pltpu.VMEM((N_HEADS, L, L), F32),
                        pltpu.VMEM((N_HEADS, L, HEAD_DIM), F32), pltpu.VMEM((N_HEADS, L, HEAD_DIM), F32)],
        compiler_params=_cparams(("arbitrary", "arbitrary")),
        name="ret",
    )(q, k, v, s0)


def _keys_values(lat, kr, wkt_ref, wv_ref, placet):
    lat = lat.astype(BF16)
    nt = (((1,), (1,)), ((), ()))
    kt = (lax.dot_general(wkt_ref[...], lat, nt, preferred_element_type=F32)
          + lax.dot_general(placet, kr.astype(BF16), nt, preferred_element_type=F32)).astype(BF16)
    lane = lax.broadcasted_iota(jnp.int32, (1, QP), 1) % HEAD_PAD
    return kt, jnp.where(lane == V_D, 1.0, _dot(lat, wv_ref[...])).astype(BF16)


def _attn_cached_kernel(past, T, q_ref, plat_ref, pkr_ref, lat_ref, kr_ref, wkt_ref, wv_ref, placet_ref, o_ref,
                        kt_s, v_s):
    tk = past + T
    tk_pad = kt_s.shape[1]
    kt_s[:, 0:past], v_s[0:past, :] = _keys_values(plat_ref[0], pkr_ref[0], wkt_ref, wv_ref, placet_ref[:, 0:ROPE_D])
    kt_s[:, past:tk], v_s[past:tk, :] = _keys_values(lat_ref[0], kr_ref[0], wkt_ref, wv_ref, placet_ref[...])
    if tk_pad != tk:
        kt_s[:, tk:] = jnp.zeros((QP, tk_pad - tk), BF16)
        v_s[tk:, :] = jnp.zeros((tk_pad - tk, QP), BF16)
    qi = past + lax.broadcasted_iota(jnp.int32, (T, tk_pad), 0)
    ki = lax.broadcasted_iota(jnp.int32, (T, tk_pad), 1)
    keep = jnp.logical_and(ki // CHUNK <= qi // CHUNK, ki < tk)
    for h in range(N_HEADS):
        hs = slice(h * HEAD_PAD, (h + 1) * HEAD_PAD)
        s = jnp.where(keep, _dot(q_ref[0, :, hs], kt_s[hs, :]), -jnp.inf)
        p = jnp.exp2(s - jnp.max(s, axis=-1, keepdims=True))
        a = _dot(p.astype(BF16), v_s[:, hs])
        o_ref[0, :, h * V_D:(h + 1) * V_D] = a[:, :V_D] / a[:, V_D:V_D + 1]


def _attn_cached(qp, past_lat, past_kr, lat, kr, wkt, wv, placet):
    B, past, _ = past_lat.shape
    T = lat.shape[1]
    tk_pad = -(-(past + T) // LANES) * LANES
    assert tk_pad * QP * 2 * 2 <= VMEM_LIMIT // 4 and past % LANES == 0
    whole = lambda a: pl.BlockSpec((1,) + a.shape[1:], lambda b: (b, 0, 0))
    full = lambda a: pl.BlockSpec(a.shape, lambda b: (0,) * a.ndim)
    return pl.pallas_call(
        functools.partial(_attn_cached_kernel, past, T),
        grid=(B,),
        in_specs=[whole(qp), whole(past_lat), whole(past_kr), whole(lat), whole(kr), full(wkt), full(wv), full(placet)],
        out_specs=pl.BlockSpec((1, T, BW), lambda b: (b, 0, 0)),
        out_shape=jax.ShapeDtypeStruct((B, T, BW), F32),
        scratch_shapes=[pltpu.VMEM((QP, tk_pad), BF16), pltpu.VMEM((tk_pad, QP), BF16)],
        compiler_params=_cparams(("arbitrary",)),
        name="attn_cached",
    )(qp, past_lat, past_kr, lat, kr, wkt, wv, placet)


A_FIRST, A_LAST, A_MASKED = 1, 2, 4


def _attn_kernel(bq, bk, q0, kv_len, qi_ref, kj_ref, flag_ref, q_ref, kt_ref, v_ref, o_ref, m_ref, acc_ref):
    step = pl.program_id(1)
    i = qi_ref[step]
    j = kj_ref[step]
    flags = flag_ref[step]

    @pl.when((flags & A_FIRST) != 0)
    def _():
        m_ref[...] = jnp.full(m_ref.shape, -jnp.inf, F32)
        acc_ref[...] = jnp.zeros(acc_ref.shape, F32)

    def block(masked):
        if masked:
            qi = q0 + i * bq + lax.broadcasted_iota(jnp.int32, (bq, bk), 0)
            ki = j * bk + lax.broadcasted_iota(jnp.int32, (bq, bk), 1)
            keep = jnp.logical_and(ki // CHUNK <= qi // CHUNK, ki < kv_len)
        for h in range(N_HEADS):
            hs = slice(h * HEAD_PAD, (h + 1) * HEAD_PAD)
            s = _dot(q_ref[0, :, hs], kt_ref[0, hs, :])
            if masked:
                s = jnp.where(keep, s, -jnp.inf)
            m_old = m_ref[h]
            m_new = jnp.maximum(m_old, jnp.max(s, axis=-1, keepdims=True))
            alpha = jnp.exp2(m_old - m_new)
            m_full = jnp.concatenate([m_new] * (bk // LANES), axis=1) if bk % LANES == 0 else m_new[:, 0:1]
            p = jnp.exp2(s - m_full)
            acc_ref[:, hs] = alpha * acc_ref[:, hs] + _dot(p.astype(BF16), v_ref[0, :, hs])
            m_ref[h] = m_new

    @pl.when((flags & A_MASKED) == 0)
    def _():
        block(False)

    @pl.when((flags & A_MASKED) != 0)
    def _():
        block(True)

    @pl.when((flags & A_LAST) != 0)
    def _():
        for h in range(N_HEADS):
            a = acc_ref[:, h * HEAD_PAD:(h + 1) * HEAD_PAD]
            o_ref[0, :, h * V_D:(h + 1) * V_D] = a[:, :V_D] / a[:, V_D:V_D + 1]


def _attn(qp, ktp, vp, q0, kv_len, bk):
    B, T, _ = qp.shape
    tk = vp.shape[1]
    bq = min(1024, T)
    nq, nk = T // bq, tk // bk
    qi, kj, flags = [], [], []
    for i in range(nq):
        first_q, last_q = q0 + i * bq, q0 + (i + 1) * bq - 1
        n_blocks = min(nk, (min(kv_len, (last_q // CHUNK + 1) * CHUNK) - 1) // bk + 1)
        for j in range(n_blocks):
            fully_visible = (j + 1) * bk <= min(kv_len, (first_q // CHUNK + 1) * CHUNK)
            qi.append(i)
            kj.append(j)
            flags.append((A_FIRST if j == 0 else 0) | (A_LAST if j == n_blocks - 1 else 0)
                         | (0 if fully_visible else A_MASKED))
    sched = [jnp.asarray(np.asarray(a, np.int32)) for a in (qi, kj, flags)]
    grid_spec = pltpu.PrefetchScalarGridSpec(
        num_scalar_prefetch=3,
        grid=(B, len(qi)),
        in_specs=[pl.BlockSpec((1, bq, QP), lambda b, s, qi, kj, fl: (b, qi[s], 0)),
                  pl.BlockSpec((1, QP, bk), lambda b, s, qi, kj, fl: (b, 0, kj[s])),
                  pl.BlockSpec((1, bk, QP), lambda b, s, qi, kj, fl: (b, kj[s], 0))],
        out_specs=pl.BlockSpec((1, bq, BW), lambda b, s, qi, kj, fl: (b, qi[s], 0)),
        scratch_shapes=[pltpu.VMEM((N_HEADS, bq, LANES), F32), pltpu.VMEM((bq, QP), F32)])
    return pl.pallas_call(
        functools.partial(_attn_kernel, bq, bk, q0, kv_len),
        grid_spec=grid_spec,
        out_shape=jax.ShapeDtypeStruct((B, T, BW), F32),
        compiler_params=_cparams(("arbitrary", "arbitrary")),
        name="attn",
    )(*sched, qp, ktp, vp)


def _merge_kernel(final, h_ref, ya_ref, posta_ref, ob_ref, gb_ref, oc_ref, gc_ref, od_ref, gd_ref,
                  par_ref, seg_ref, wg_ref, wb_ref, wo_ref, out_ref):
    par = lambda r, n: par_ref[r:r + 1, 0:n]
    seg = seg_ref[...]
    h = h_ref[0]
    u = _rms(h, par(R_NORM, D_MODEL), NORM_EPS).astype(BF16)
    inv = 1.0 / HEAD_DIM

    ya = ya_ref[...]
    xc = ya - _dot_lhs2(ya, seg) * inv
    yn = xc * lax.rsqrt(_dot_lhs2(xc * xc, seg) * inv + GN_EPS) * par(R_LNW, BW) + par(R_LNB, BW)
    oa = (yn + posta_ref[:, :BW]) * posta_ref[:, BW:]

    def head_rms(o, g):
        return o * lax.rsqrt(_dot_lhs2(o * o, seg) * inv + HEAD_NORM_EPS) * g

    ob = head_rms(ob_ref[...], par(R_HG, BW)) * gb_ref[...]
    oc = oc_ref[0] * gc_ref[0]
    od = head_rms(od_ref[0], par(R_RG, BW)) * gd_ref[0]

    merged = None
    for n, o in enumerate((oa, ob, oc, od)):
        gate = _sigmoid(_dot(u, wg_ref[:, n * D_MODEL:(n + 1) * D_MODEL]))
        term = gate * _dot(o.astype(BF16), wb_ref[n * BW:(n + 1) * BW, :])
        merged = term if merged is None else merged + term
    out = h + _dot(merged.astype(BF16), wo_ref[...])
    if final:
        out = _rms(out, par(R_FG, D_MODEL), NORM_EPS)
    out_ref[0] = out


def _merge(final, h, ya, posta, ob, gb, oc, gc, od, gd, par, seg, wg, wb, wo):
    B, T, _ = h.shape
    tt = min(512, T)
    bspec = lambda c: pl.BlockSpec((1, tt, c), lambda t, b: (b, t, 0))
    tspec = lambda c: pl.BlockSpec((tt, c), lambda t, b: (t, b))
    full = lambda a: pl.BlockSpec(a.shape, lambda t, b: (0,) * a.ndim)
    consts = (par, seg, wg, wb, wo)
    return pl.pallas_call(
        functools.partial(_merge_kernel, final),
        grid=(T // tt, B),
        in_specs=[bspec(D_MODEL), tspec(BW), tspec(2 * BW), tspec(BW), tspec(BW), bspec(BW), bspec(BW), bspec(BW),
                  bspec(BW)] + [full(a) for a in consts],
        out_specs=bspec(D_MODEL),
        out_shape=jax.ShapeDtypeStruct((B, T, D_MODEL), F32),
        compiler_params=_cparams(("arbitrary", "arbitrary")),
        name="merge",
    )(h, ya, posta, ob, gb, oc, gc, od, gd, *consts)


def _rope_tables(past, T):
    pos = (past + jnp.arange(T, dtype=jnp.int32)).astype(F32)[:, None]

    def tables(width, group, start, d):
        lane = np.arange(width) % group - start
        on = (lane >= 0) & (lane < d)
        idx = np.where(on, lane % (d // 2), 0)
        inv_freq = jnp.power(ROPE_BASE, -jnp.arange(0, d, 2, dtype=F32) / d)
        ang = pos * inv_freq[None, :]
        cos = jnp.where(on[None, :], jnp.cos(ang)[:, idx], 1.0)
        sign = np.where(lane < d // 2, -1.0, 1.0).astype(np.float32)
        sin = jnp.where(on[None, :], jnp.sin(ang)[:, idx] * sign[None, :], 0.0)
        return cos, sin

    cd, sd = tables(LANES, HEAD_DIM, 0, HEAD_DIM)
    cq, sq = tables(LANES, HEAD_PAD, NOPE, ROPE_D)
    ck, sk = tables(LANES, LANES, 0, ROPE_D)
    return jnp.concatenate([cd, sd, cq, sq, ck, sk], axis=1)


def _layer_weights(l, p):
    w_in = p["w_in"][l]
    gate_cols = N_BRANCH * D_MODEL
    c_kr = gate_cols + SHIFT_COLS + BW + 4 * BW + Q_LORA + KV_LORA
    c_g = c_kr + ROPE_D
    w_rest = jnp.concatenate([
        w_in[:, gate_cols:c_kr],
        jnp.pad(w_in[:, c_kr:c_g], ((0, 0), (0, LANES - ROPE_D))),
        w_in[:, c_g:],
    ], axis=1).astype(BF16)
    wg = w_in[:, :gate_cols].astype(BF16)

    z = jnp.zeros((64, BW), F32)
    wl = jnp.concatenate([jnp.concatenate([p["rwkv_w_up"][l], z], axis=1),
                          jnp.concatenate([z, p["rwkv_a_up"][l]], axis=1)], axis=0)
    wl_hi = wl.astype(BF16)
    wl_lo = (wl - wl_hi.astype(F32)).astype(BF16)

    wq = p["mla_w_q_up"][l].reshape(Q_LORA, N_HEADS, QK_D)
    wq = jnp.pad(wq, ((0, 0), (0, 0), (0, HEAD_PAD - QK_D))).reshape(Q_LORA, QP).astype(BF16)
    wkv = p["mla_w_kv_up"][l].reshape(KV_LORA, N_HEADS, NOPE + V_D)
    wkt = jnp.pad(wkv[:, :, :NOPE], ((0, 0), (0, 0), (0, HEAD_PAD - NOPE))).reshape(KV_LORA, QP).T.astype(BF16)
    wv = jnp.pad(wkv[:, :, NOPE:], ((0, 0), (0, 0), (0, HEAD_PAD - V_D))).reshape(KV_LORA, QP).astype(BF16)

    wb = p["w_branch"][l].reshape(N_BRANCH * BW, D_MODEL).astype(BF16)
    wo = p["w_out"][l].astype(BF16)

    return dict(w_rest=w_rest, wg=wg, wl_hi=wl_hi, wl_lo=wl_lo, wq=wq, wkt=wkt, wv=wv, wb=wb, wo=wo)


def _param_tables(p):
    def rows(v, n=1):
        v = v.reshape(DEPTH, n, -1).astype(F32)
        return jnp.pad(v, ((0, 0), (0, 0), (0, D_MODEL - v.shape[-1])))

    every = lambda v, n: jnp.broadcast_to(v.reshape(1, n, -1), (DEPTH, n, v.shape[-1]))
    parts = [rows(p["norm_g"]), rows(p["rwkv_mu"]), rows(p["rwkv_w0"]), rows(p["rwkv_a0"]), rows(p["rwkv_k_k"]),
             rows(p["rwkv_k_a"]), rows(p["rwkv_r_k"]), rows(every(p["hgrn_lb_logits"], DEPTH), DEPTH),
             rows(p["mla_q_norm_g"]), rows(p["mla_kv_norm_g"]), rows(p["rwkv_ln_w"]), rows(p["rwkv_ln_b"]),
             rows(p["hgrn_norm_g"]), rows(p["ret_norm_g"]), rows(every(p["final_norm_g"], 1))]
    used = sum(a.shape[1] for a in parts)
    return jnp.concatenate(parts + [jnp.zeros((DEPTH, N_ROWS - used, D_MODEL), F32)], axis=1)


def _all_weights(p):
    tables = _param_tables(p)
    return [dict(_layer_weights(l, p), par=tables[l]) for l in range(DEPTH)]


def _constants():
    head = np.arange(BW) // HEAD_DIM
    seg = jnp.asarray((head[:, None] == head[None, :]).astype(np.float32), dtype=BF16)
    place = np.zeros((LANES, QP), np.float32)
    for h in range(N_HEADS):
        for j in range(ROPE_D):
            place[j, h * HEAD_PAD + NOPE + j] = 1.0
    return seg, jnp.asarray(place.T, dtype=BF16)


def _state_in(s, B, rep, key_last):
    vr = HEAD_DIM // rep
    if key_last:
        y = s.reshape(B, 2, 2, rep, vr, HEAD_DIM).transpose(5, 4, 3, 2, 0, 1)
    else:
        y = s.reshape(B, 2, 2, HEAD_DIM, rep, vr).transpose(3, 5, 4, 2, 0, 1)
    return y.reshape(HEAD_DIM, vr, LANES).astype(F32)


def _state_out(y, B, rep, key_last):
    vr = HEAD_DIM // rep
    y = y.reshape(HEAD_DIM, vr, rep, 2, B, 2)
    if key_last:
        y = y.transpose(4, 5, 3, 2, 1, 0)
    else:
        y = y.transpose(4, 5, 3, 0, 2, 1)
    return y.reshape(B, N_HEADS, HEAD_DIM, HEAD_DIM)


def _run_trunk(x, shift0, rwkv0, hgrn0, ret0, lat0, kr0, p, weights, consts):
    B, T, _ = x.shape
    past = lat0.shape[2]
    bh = B * N_HEADS
    rep = LANES // bh
    assert rep * bh == LANES and HEAD_DIM % rep == 0 and (HEAD_DIM // rep) % SUBLANES == 0 and N_HEADS == 4
    seg, placet = consts
    tab = _rope_tables(past, T)
    h = x
    per_layer = []
    for l in range(DEPTH):
        w = weights[l]
        (ar, aw, ak, av, akk, ab, posta, bq, bf, bv, gb, qp, lat, kr, gc, ktp, vp, dq, dk, dv, gd, shn) = _proj(
            l, h, shift0[l][:, None, :], tab, w["par"], w["w_rest"], w["wl_hi"], w["wl_lo"], seg, w["wq"],
            w["wkt"], w["wv"], placet)

        ya, ob, s_rwkv, s_hgrn = _seq([ar, aw, ak, akk, ab], av, [bq, bf], bv, _state_in(rwkv0[l], B, rep, True),
                                      _state_in(hgrn0[l], B, rep, False), bh)
        od, s_ret = _ret(dq, dk, dv, ret0[l].astype(F32))

        if past == 0:
            oc = _attn(qp, ktp, vp, 0, T, min(1024, T))
        else:
            oc = _attn_cached(qp, lat0[l], kr0[l], lat, kr, w["wkt"], w["wv"], placet)

        h = _merge(l == DEPTH - 1, h, ya, posta, ob, gb, oc, gc, od, gd, w["par"], seg, w["wg"], w["wb"], w["wo"])
        per_layer.append((shn[-1, :, 0, :], _state_out(s_rwkv, B, rep, True), _state_out(s_hgrn, B, rep, False),
                          s_ret, lat, kr[:, :, :ROPE_D]))
    stacked = [jnp.stack([st[j] for st in per_layer]) for j in range(6)]
    return h, stacked


def kernel(x_prompt, x_sample, state_rwkv_shift, state_rwkv, state_hgrn, cache_mla_latent, cache_mla_krope, state_ret, norm_g, w_in, rwkv_mu, rwkv_w0, rwkv_w_up, rwkv_a0, rwkv_a_up, rwkv_k_k, rwkv_k_a, rwkv_r_k, rwkv_ln_w, rwkv_ln_b, hgrn_lb_logits, hgrn_norm_g, mla_q_norm_g, mla_w_q_up, mla_kv_norm_g, mla_w_kv_up, ret_norm_g, w_branch, w_out, final_norm_g):
    p = dict(norm_g=norm_g, w_in=w_in, rwkv_mu=rwkv_mu, rwkv_w0=rwkv_w0, rwkv_w_up=rwkv_w_up, rwkv_a0=rwkv_a0,
             rwkv_a_up=rwkv_a_up, rwkv_k_k=rwkv_k_k, rwkv_k_a=rwkv_k_a, rwkv_r_k=rwkv_r_k, rwkv_ln_w=rwkv_ln_w,
             rwkv_ln_b=rwkv_ln_b, hgrn_lb_logits=hgrn_lb_logits, hgrn_norm_g=hgrn_norm_g, mla_q_norm_g=mla_q_norm_g,
             mla_w_q_up=mla_w_q_up, mla_kv_norm_g=mla_kv_norm_g, mla_w_kv_up=mla_w_kv_up, ret_norm_g=ret_norm_g,
             w_branch=w_branch, w_out=w_out, final_norm_g=final_norm_g)
    weights = _all_weights(p)
    consts = _constants()
    bp, dt = x_prompt.shape[0], x_prompt.dtype
    zs = jnp.zeros((DEPTH, bp, N_HEADS, HEAD_DIM, HEAD_DIM), dt)
    y_p, (p_shift, p_rwkv, p_hgrn, p_ret, p_lat, p_kr) = _run_trunk(
        x_prompt, jnp.zeros((DEPTH, bp, SHIFT_COLS), dt), zs, zs, zs, jnp.zeros((DEPTH, bp, 0, KV_LORA), dt),
        jnp.zeros((DEPTH, bp, 0, ROPE_D), dt), p, weights, consts)
    y_s, (s_shift, s_rwkv, s_hgrn, s_ret, s_lat, s_kr) = _run_trunk(
        x_sample, state_rwkv_shift, state_rwkv, state_hgrn, state_ret, cache_mla_latent, cache_mla_krope,
        p, weights, consts)
    return (y_p, y_s, p_shift, s_shift, p_rwkv, s_rwkv, p_hgrn, s_hgrn, p_lat, s_lat, p_kr, s_kr, p_ret, s_ret)
```
